```python
import jax, jax.numpy as jnp
from jax import lax
import numpy as np

D_MODEL = 1024
BATCH = 16
SEQ = 2048
DEPTH = 4

GRID_W = 64
CTX_LEN = 256
N_MIXERS = 2
N_A_LAYERS = (DEPTH + N_MIXERS - 1) // N_MIXERS
N_B_LAYERS = DEPTH // N_MIXERS
DN_ALPHA = (2.0 * DEPTH) ** 0.25
DN_BETA = (8.0 * DEPTH) ** -0.25
LN_EPS = 1e-5
CHUNK = 128
ROWS_PER_CHUNK = CHUNK // GRID_W
A_WIDTH = D_MODEL
A_HEADS = 8
A_GROUP = A_WIDTH // A_HEADS
GLA_HEADS = 4
GLA_DK = D_MODEL // 2 // GLA_HEADS
GLA_DV = D_MODEL // GLA_HEADS
GLA_QK = GLA_HEADS * GLA_DK
GLA_V = GLA_HEADS * GLA_DV
GLA_GATE_RANK = 16
GLA_TAU = 16.0
GLA_CHUNK = 64
GLA_IN = 2 * GLA_QK + 2 * GLA_V + 2 * GLA_GATE_RANK
PEER_HEADS = 8
PEER_NKEYS = 128
PEER_EXPERTS = PEER_NKEYS * PEER_NKEYS
PEER_QDIM = 128
PEER_HALF = PEER_QDIM // 2
PEER_TOPK = 16
PEER_BLOCK = 128

kernel_name = "hybrid_chunkmlp_gla_peer_prefix_dit"


def layer_norm(x, g, b):
    xf = x.astype(jnp.float32)
    mu = jnp.mean(xf, axis=-1, keepdims=True)
    var = jnp.mean(jnp.square(xf - mu), axis=-1, keepdims=True)
    y = (xf - mu) * lax.rsqrt(var + LN_EPS)
    return (y * g + b).astype(x.dtype)


def chunk_mlp(h, n_chunks, w_in, norm_g, norm_b, w_s, b_s, w_out):
    bsz, t, _ = h.shape
    z = jax.nn.gelu(h @ w_in)
    u, v = jnp.split(z, 2, axis=-1)
    v = layer_norm(v, norm_g, norm_b)
    vb = v.reshape(bsz, n_chunks, CHUNK, A_HEADS, A_GROUP)
    s = jnp.einsum('hpq,bnqhc->bnphc', w_s, vb) + b_s.T[None, None, :, :, None]
    return (u * s.reshape(bsz, t, A_WIDTH)) @ w_out


def gla_chunked(q, k, v, g, s0):
    bsz, t, h, dk = q.shape
    dv = v.shape[-1]
    n = t // GLA_CHUNK
    f32 = jnp.float32
    q = q.astype(f32).reshape(bsz, n, GLA_CHUNK, h, dk)
    k = k.astype(f32).reshape(bsz, n, GLA_CHUNK, h, dk)
    v = v.astype(f32).reshape(bsz, n, GLA_CHUNK, h, dv)
    b = jnp.cumsum(g.astype(f32).reshape(bsz, n, GLA_CHUNK, h, dk), axis=2)
    b_end = b[:, :, -1:]
    q_in = q * jnp.exp(b)
    k_in = k * jnp.exp(-b)
    k_end = k * jnp.exp(b_end - b)
    att = jnp.einsum('bnihd,bnjhd->bnhij', q_in, k_in)
    mask = jnp.tril(jnp.ones((GLA_CHUNK, GLA_CHUNK), dtype=bool))
    att = jnp.where(mask, att, 0.0)
    o_intra = jnp.einsum('bnhij,bnjhe->bnihe', att, v)

    def step(state, inp):
        qc, kc, vc, dc = inp
        o = jnp.einsum('bihd,bhde->bihe', qc, state)
        state = dc[..., None] * state + jnp.einsum('bjhd,bjhe->bhde', kc, vc)
        return state, o

    xs = (jnp.moveaxis(q_in, 1, 0), jnp.moveaxis(k_end, 1, 0), jnp.moveaxis(v, 1, 0),
          jnp.moveaxis(jnp.exp(b_end[:, :, 0]), 1, 0))
    s_fin, o_inter = lax.scan(step, s0.astype(f32), xs)
    o = o_intra + jnp.moveaxis(o_inter, 0, 1)
    return o.reshape(bsz, t, h, dv), s_fin


def gla_reverse(q, k, v, g, s0):
    o, s_fin = gla_chunked(jnp.flip(q, 1), jnp.flip(k, 1), jnp.flip(v, 1), jnp.flip(g, 1), s0)
    return jnp.flip(o, 1), s_fin


def gla_project(h, w_in, w_gate, gate_bias):
    bsz, t, _ = h.shape
    z = h @ w_in
    q, k, v, r, gl = jnp.split(z, [GLA_QK, 2 * GLA_QK, 2 * GLA_QK + GLA_V, 2 * GLA_QK + 2 * GLA_V], axis=-1)
    q = q.reshape(bsz, t, GLA_HEADS, GLA_DK) * (GLA_DK ** -0.5)
    k = k.reshape(bsz, t, GLA_HEADS, GLA_DK)
    v = v.reshape(bsz, t, GLA_HEADS, GLA_DV)
    gl = gl.reshape(bsz, t, 2, GLA_GATE_RANK)
    zg = (jnp.einsum('btjr,jrd->btjd', gl, w_gate) + gate_bias).astype(jnp.float32)
    glog = (jax.nn.log_sigmoid(zg) / GLA_TAU).reshape(bsz, t, 2, GLA_HEADS, GLA_DK)
    return q, k, v, r, glog


def gla_readout(o, r, gn_g, w_out):
    bsz, t = o.shape[:2]
    mu = jnp.mean(o, axis=-1, keepdims=True)
    var = jnp.mean(jnp.square(o - mu), axis=-1, keepdims=True)
    y = ((o - mu) * lax.rsqrt(var + LN_EPS)).reshape(bsz, t, GLA_V) * gn_g
    return (y.astype(r.dtype) * jax.nn.silu(r)) @ w_out


def gla_mixer(h_lat, h_ctx, w_in, w_gate, gate_bias, gn_g, w_out, need_ctx_out):
    qc, kc, vc, rc, gc = gla_project(h_ctx, w_in, w_gate, gate_bias)
    ql, kl, vl, rl, gll = gla_project(h_lat, w_in, w_gate, gate_bias)
    s0 = jnp.zeros((h_ctx.shape[0], GLA_HEADS, GLA_DK, GLA_DV), jnp.float32)
    o_cf, s_f = gla_chunked(qc, kc, vc, gc[:, :, 0], s0)
    o_cb, s_b = gla_reverse(qc, kc, vc, gc[:, :, 1], s0)
    o_lf, _ = gla_chunked(ql, kl, vl, gll[:, :, 0], s_f)
    o_lb, _ = gla_reverse(ql, kl, vl, gll[:, :, 1], s_b)
    out_lat = gla_readout(o_lf + o_lb, rl, gn_g, w_out)
    out_ctx = gla_readout(o_cf + o_cb, rc, gn_g, w_out) if need_ctx_out else None
    return out_lat, out_ctx


def peer_ffn(h, w_q, sub_keys, u_tab, v_tab):
    bsz, t, d = h.shape
    blocks = h.reshape(-1, PEER_BLOCK, d)

    def one_block(xb):
        q = (xb @ w_q).reshape(PEER_BLOCK, PEER_HEADS, 2, PEER_HALF)
        s = jnp.einsum('thpd,pkd->thpk', q, sub_keys).astype(jnp.float32)
        s1, i1 = lax.top_k(s[:, :, 0], PEER_TOPK)
        s2, i2 = lax.top_k(s[:, :, 1], PEER_TOPK)
        cand_s = (s1[..., :, None] + s2[..., None, :]).reshape(PEER_BLOCK, PEER_HEADS, PEER_TOPK * PEER_TOPK)
        cand_i = (i1[..., :, None] * PEER_NKEYS + i2[..., None, :]).reshape(PEER_BLOCK, PEER_HEADS, PEER_TOPK * PEER_TOPK)
        top_s, pos = lax.top_k(cand_s, PEER_TOPK)
        idx = jnp.take_along_axis(cand_i, pos, axis=-1)
        wts = jax.nn.softmax(top_s, axis=-1).astype(xb.dtype)
        act = jax.nn.gelu(jnp.einsum('thkd,td->thk', u_tab[idx], xb)) * wts
        return jnp.einsum('thk,thkd->td', act, v_tab[idx])

    return lax.map(one_block, blocks).reshape(bsz, t, d)


def setup_inputs(seed: int = 0) -> dict:
    key = jax.random.key(seed)
    ks = jax.random.split(key, 24)
    f32 = jnp.float32

    def nrm(k, shape, s):
        return jax.random.normal(k, shape, f32) * s

    d = D_MODEL
    return {
        "x": nrm(ks[0], (BATCH, SEQ, d), 1.0),
        "c": nrm(ks[1], (BATCH, d), 1.0),
        "ctx": nrm(ks[2], (BATCH, CTX_LEN, d), 1.0),
        "c_ctx": nrm(ks[3], (d,), 1.0),
        "w_mod": nrm(ks[4], (DEPTH, d, 6 * d), 0.5 * d ** -0.5),
        "b_mod": nrm(ks[5], (DEPTH, 6 * d), 0.01),
        "ln_g": 1.0 + nrm(ks[6], (DEPTH, 2, d), 0.01),
        "ln_b": nrm(ks[7], (DEPTH, 2, d), 0.01),
        "a_w_in": nrm(ks[8], (N_A_LAYERS, d, 2 * A_WIDTH), d ** -0.5),
        "a_norm_g": 1.0 + nrm(ks[9], (N_A_LAYERS, A_WIDTH), 0.01),
        "a_norm_b": nrm(ks[10], (N_A_LAYERS, A_WIDTH), 0.01),
        "a_w_s": nrm(ks[11], (N_A_LAYERS, A_HEADS, CHUNK, CHUNK), CHUNK ** -0.5),
        "a_b_s": 1.0 + nrm(ks[12], (N_A_LAYERS, A_HEADS, CHUNK), 0.01),
        "a_w_out": nrm(ks[13], (N_A_LAYERS, A_WIDTH, d), DN_BETA * A_WIDTH ** -0.5),
        "b_w_in": nrm(ks[14], (N_B_LAYERS, d, GLA_IN), d ** -0.5),
        "b_w_gate": nrm(ks[15], (N_B_LAYERS, 2, GLA_GATE_RANK, GLA_QK), GLA_GATE_RANK ** -0.5),
        "b_gate_bias": nrm(ks[16], (N_B_LAYERS, 2, GLA_QK), 0.01),
        "b_gn_g": 1.0 + nrm(ks[17], (N_B_LAYERS, GLA_V), 0.01),
        "b_w_out": nrm(ks[18], (N_B_LAYERS, GLA_V, d), DN_BETA * GLA_V ** -0.5),
        "p_w_q": nrm(ks[19], (DEPTH, d, PEER_HEADS * PEER_QDIM), d ** -0.5),
        "p_keys": nrm(ks[20], (DEPTH, 2, PEER_NKEYS, PEER_HALF), PEER_HALF ** -0.5),
        "p_u": nrm(ks[21], (DEPTH, PEER_EXPERTS, d), d ** -0.5),
        "p_v": nrm(ks[22], (DEPTH, PEER_EXPERTS, d), DN_BETA),
    }


def reference(x, c, ctx, c_ctx, w_mod, b_mod, ln_g, ln_b,
              a_w_in, a_norm_g, a_norm_b, a_w_s, a_b_s, a_w_out,
              b_w_in, b_w_gate, b_gate_bias, b_gn_g, b_w_out,
              p_w_q, p_keys, p_u, p_v):
    bsz, seq, d = x.shape
    rows = seq // GRID_W
    lat_chunks = rows // ROWS_PER_CHUNK
    ctx_chunks = ctx.shape[1] // CHUNK
    sc = jax.nn.silu(c)
    sc_ctx = jax.nn.silu(c_ctx)
    for i in range(DEPTH):
        last = i == DEPTH - 1
        m = (sc @ w_mod[i] + b_mod[i]).reshape(bsz, 6, d)
        mc = (sc_ctx @ w_mod[i] + b_mod[i]).reshape(6, d)
        sh1, s1, g1, sh2, s2, g2 = [m[:, j, None, :] for j in range(6)]
        sh1c, s1c, g1c, sh2c, s2c, g2c = [mc[j] for j in range(6)]
        h_lat = x * (1.0 + s1) + sh1
        if i % N_MIXERS == 0:
            j = i // N_MIXERS
            prm = (a_w_in[j], a_norm_g[j], a_norm_b[j], a_w_s[j], a_b_s[j], a_w_out[j])
            mix_lat = chunk_mlp(h_lat, lat_chunks, *prm)
            mix_ctx = None if last else chunk_mlp(ctx * (1.0 + s1c) + sh1c, ctx_chunks, *prm)
        else:
            j = i // N_MIXERS
            h_ctx = ctx * (1.0 + s1c) + sh1c
            mix_lat, mix_ctx = gla_mixer(h_lat, h_ctx, b_w_in[j], b_w_gate[j], b_gate_bias[j],
                                         b_gn_g[j], b_w_out[j], not last)
        x = layer_norm(DN_ALPHA * x + g1 * mix_lat, ln_g[i, 0], ln_b[i, 0])
        ffn_lat = peer_ffn(x * (1.0 + s2) + sh2, p_w_q[i], p_keys[i], p_u[i], p_v[i])
        x = layer_norm(DN_ALPHA * x + g2 * ffn_lat, ln_g[i, 1], ln_b[i, 1])
        if not last:
            ctx = layer_norm(DN_ALPHA * ctx + g1c * mix_ctx, ln_g[i, 0], ln_b[i, 0])
            ffn_ctx = peer_ffn(ctx * (1.0 + s2c) + sh2c, p_w_q[i], p_keys[i], p_u[i], p_v[i])
            ctx = layer_norm(DN_ALPHA * ctx + g2c * ffn_ctx, ln_g[i, 1], ln_b[i, 1])
    return x
```

```python
import functools

import jax
import jax.numpy as jnp
from jax import lax
from jax.experimental import pallas as pl
from jax.experimental.pallas import tpu as pltpu

D_MODEL = 1024
DEPTH = 4
N_MIXERS = 2
DN_ALPHA = (2.0 * DEPTH) ** 0.25
LN_EPS = 1e-5
CHUNK = 128
A_WIDTH = D_MODEL
A_HEADS = 8
A_GROUP = A_WIDTH // A_HEADS
GLA_HEADS = 4
GLA_DK = D_MODEL // 2 // GLA_HEADS
GLA_DV = D_MODEL // GLA_HEADS
GLA_QK = GLA_HEADS * GLA_DK
GLA_V = GLA_HEADS * GLA_DV
GLA_GATE_RANK = 16
GLA_TAU = 16.0
GLA_CHUNK = 64
PEER_HEADS = 8
PEER_NKEYS = 128
PEER_QDIM = 128
PEER_HALF = PEER_QDIM // 2
PEER_TOPK = 16
PEER_PAIRS = PEER_HEADS * PEER_TOPK

PEER_TOKENS_PER_STEP = 128
PEER_ROW_BUFFERS = 8


def _layer_norm(x, g, b):
    mu = jnp.mean(x, axis=-1, keepdims=True)
    var = jnp.mean(jnp.square(x - mu), axis=-1, keepdims=True)
    return (x - mu) * lax.rsqrt(var + LN_EPS) * g + b


def _peer_expert_kernel(idx_ref, idx_next_ref, w_ref, x_ref, mod_ref, lng_ref, lnb_ref, tab_ref,
                        out_ref, rows, h_scr, ffn_scr, sems, *, tb, nbuf, d):
    step = pl.program_id(0)
    last = pl.num_programs(0) - 1
    npairs = rows.shape[1]
    ahead = nbuf - 1

    def issue(src_idx_ref, t, slot):
        for k in range(npairs):
            e = src_idx_ref[t, k]
            pltpu.make_async_copy(tab_ref.at[pl.ds(e, 1), :], rows.at[slot, pl.ds(k, 1), :],
                                  sems.at[slot]).start()

    def wait(slot):
        pltpu.make_async_copy(tab_ref.at[pl.ds(0, npairs), :], rows.at[slot], sems.at[slot]).wait()

    @pl.when(step == 0)
    def _():
        for t in range(ahead):
            issue(idx_ref, t, t)

    s2 = mod_ref[0, 4:5, :]
    sh2 = mod_ref[0, 3:4, :]
    g2 = mod_ref[0, 5:6, :]
    h_scr[...] = x_ref[...] * (1.0 + s2) + sh2

    def token(t, w_col):
        slot = lax.rem(t, nbuf)
        wait(slot)
        r = rows[slot]
        h = h_scr[pl.ds(t, 1), :]
        s = jnp.sum(r[:, :d] * h, axis=-1, keepdims=True)
        act = jax.nn.gelu(s) * w_col
        ffn_scr[pl.ds(t, 1), :] = jnp.sum(act * r[:, d:], axis=0, keepdims=True)

    def group(j8, src_idx_ref, base):
        w_t = w_ref[pl.ds(pl.multiple_of(j8 * 8, 8), 8), :].T
        for u in range(8):
            t = j8 * 8 + u
            tn = t + ahead
            issue(src_idx_ref, tn - base, lax.rem(tn, nbuf))
            token(t, w_t[:, u:u + 1])

    n_groups = tb // 8
    def main_body(j8, carry):
        group(j8, idx_ref, 0)
        return carry
    lax.fori_loop(0, n_groups - 1, main_body, 0)

    j8 = n_groups - 1
    w_t = w_ref[pl.ds(j8 * 8, 8), :].T
    for u in range(8):
        t = j8 * 8 + u
        tn = t + ahead
        if tn < tb:
            issue(idx_ref, tn, tn % nbuf)
        else:
            issue(idx_next_ref, tn - tb, tn % nbuf)
        token(t, w_t[:, u:u + 1])

    @pl.when(step == last)
    def _():
        for t in range(ahead):
            wait((tb + t) % nbuf)

    y = DN_ALPHA * x_ref[...] + g2 * ffn_scr[...]
    out_ref[...] = _layer_norm(y, lng_ref[...], lnb_ref[...])


def _peer_experts(x, mod, idx, w, tab, ln_g, ln_b, *, tokens_per_sample_block, interpret=False):
    t_total, d = x.shape
    npairs = idx.shape[1]
    tb = PEER_TOKENS_PER_STEP
    nbuf = PEER_ROW_BUFFERS
    assert t_total % tb == 0 and tb % 8 == 0 and tb % nbuf == 0 and nbuf - 1 <= 8
    n_steps = t_total // tb
    n_mod = mod.shape[0]
    kern = functools.partial(_peer_expert_kernel, tb=tb, nbuf=nbuf, d=d)
    return pl.pallas_call(
        kern,
        grid=(n_steps,),
        in_specs=[
            pl.BlockSpec((tb, npairs), lambda i: (i, 0), memory_space=pltpu.SMEM),
            pl.BlockSpec((tb, npairs), lambda i: (jnp.minimum(i + 1, n_steps - 1), 0), memory_space=pltpu.SMEM),
            pl.BlockSpec((tb, npairs), lambda i: (i, 0)),
            pl.BlockSpec((tb, d), lambda i: (i, 0)),
            pl.BlockSpec((1, 6, d), lambda i: (jnp.minimum(i // tokens_per_sample_block, n_mod - 1), 0, 0)),
            pl.BlockSpec((1, d), lambda i: (0, 0)),
            pl.BlockSpec((1, d), lambda i: (0, 0)),
            pl.BlockSpec(memory_space=pl.ANY),
        ],
        out_specs=pl.BlockSpec((tb, d), lambda i: (i, 0)),
        out_shape=jax.ShapeDtypeStruct((t_total, d), jnp.float32),
        scratch_shapes=[
            pltpu.VMEM((nbuf, npairs, 2 * d), jnp.float32),
            pltpu.VMEM((tb, d), jnp.float32),
            pltpu.VMEM((tb, d), jnp.float32),
            pltpu.SemaphoreType.DMA((nbuf,)),
        ],
        compiler_params=pltpu.CompilerParams(dimension_semantics=("arbitrary",),
                                             vmem_limit_bytes=40 * 1024 * 1024),
        interpret=interpret,
    )(idx, idx, w, x, mod, ln_g.reshape(1, d), ln_b.reshape(1, d), tab)


def _chunk_mlp(h, w_in, norm_g, norm_b, w_s, b_s, w_out):
    n = h.shape[0]
    z = jax.nn.gelu(h @ w_in)
    u, v = jnp.split(z, 2, axis=-1)
    v = _layer_norm(v, norm_g, norm_b)
    vb = v.reshape(n, CHUNK, A_HEADS, A_GROUP)
    s = jnp.einsum('hpq,nqhc->nphc', w_s, vb) + b_s.T[None, :, :, None]
    return (u * s.reshape(n, CHUNK, A_WIDTH)) @ w_out


def _gla_chunked(q, k, v, g, s0):
    bsz, t, h, dk = q.shape
    dv = v.shape[-1]
    n = t // GLA_CHUNK
    q = q.reshape(bsz, n, GLA_CHUNK, h, dk)
    k = k.reshape(bsz, n, GLA_CHUNK, h, dk)
    v = v.reshape(bsz, n, GLA_CHUNK, h, dv)
    b = jnp.cumsum(g.reshape(bsz, n, GLA_CHUNK, h, dk), axis=2)
    b_end = b[:, :, -1:]
    q_in = q * jnp.exp(b)
    k_in = k * jnp.exp(-b)
    k_end = k * jnp.exp(b_end - b)
    att = jnp.einsum('bnihd,bnjhd->bnhij', q_in, k_in)
    mask = jnp.tril(jnp.ones((GLA_CHUNK, GLA_CHUNK), dtype=bool))
    att = jnp.where(mask, att, 0.0)
    o_intra = jnp.einsum('bnhij,bnjhe->bnihe', att, v)

    def step(state, inp):
        qc, kc, vc, dc = inp
        o = jnp.einsum('bihd,bhde->bihe', qc, state)
        state = dc[..., None] * state + jnp.einsum('bjhd,bjhe->bhde', kc, vc)
        return state, o

    xs = (jnp.moveaxis(q_in, 1, 0), jnp.moveaxis(k_end, 1, 0), jnp.moveaxis(v, 1, 0),
          jnp.moveaxis(jnp.exp(b_end[:, :, 0]), 1, 0))
    s_fin, o_inter = lax.scan(step, s0, xs)
    o = o_intra + jnp.moveaxis(o_inter, 0, 1)
    return o.reshape(bsz, t, h, dv), s_fin


def _gla_reverse(q, k, v, g, s0):
    o, s_fin = _gla_chunked(jnp.flip(q, 1), jnp.flip(k, 1), jnp.flip(v, 1), jnp.flip(g, 1), s0)
    return jnp.flip(o, 1), s_fin


def _gla_project(h, w_in, w_gate, gate_bias):
    bsz, t, _ = h.shape
    z = h @ w_in
    q, k, v, r, gl = jnp.split(z, [GLA_QK, 2 * GLA_QK, 2 * GLA_QK + GLA_V, 2 * GLA_QK + 2 * GLA_V], axis=-1)
    q = q.reshape(bsz, t, GLA_HEADS, GLA_DK) * (GLA_DK ** -0.5)
    k = k.reshape(bsz, t, GLA_HEADS, GLA_DK)
    v = v.reshape(bsz, t, GLA_HEADS, GLA_DV)
    gl = gl.reshape(bsz, t, 2, GLA_GATE_RANK)
    zg = jnp.einsum('btjr,jrd->btjd', gl, w_gate) + gate_bias
    glog = (jax.nn.log_sigmoid(zg) / GLA_TAU).reshape(bsz, t, 2, GLA_HEADS, GLA_DK)
    return q, k, v, r, glog


def _gla_readout(o, r, gn_g, w_out):
    bsz, t = o.shape[:2]
    mu = jnp.mean(o, axis=-1, keepdims=True)
    var = jnp.mean(jnp.square(o - mu), axis=-1, keepdims=True)
    y = ((o - mu) * lax.rsqrt(var + LN_EPS)).reshape(bsz, t, GLA_V) * gn_g
    return (y * jax.nn.silu(r)) @ w_out


def _gla_mixer(h_lat, h_ctx, w_in, w_gate, gate_bias, gn_g, w_out, need_ctx_out):
    qc, kc, vc, rc, gc = _gla_project(h_ctx, w_in, w_gate, gate_bias)
    ql, kl, vl, rl, gll = _gla_project(h_lat, w_in, w_gate, gate_bias)
    s0 = jnp.zeros((h_ctx.shape[0], GLA_HEADS, GLA_DK, GLA_DV), jnp.float32)
    o_cf, s_f = _gla_chunked(qc, kc, vc, gc[:, :, 0], s0)
    o_cb, s_b = _gla_reverse(qc, kc, vc, gc[:, :, 1], s0)
    o_lf, _ = _gla_chunked(ql, kl, vl, gll[:, :, 0], s_f)
    o_lb, _ = _gla_reverse(ql, kl, vl, gll[:, :, 1], s_b)
    out_lat = _gla_readout(o_lf + o_lb, rl, gn_g, w_out)
    out_ctx = _gla_readout(o_cf + o_cb, rc, gn_g, w_out) if need_ctx_out else None
    return out_lat, out_ctx


def _peer_route(h, w_q, sub_keys):
    t = h.shape[0]
    q = (h @ w_q).reshape(t, PEER_HEADS, 2, PEER_HALF)
    s = jnp.einsum('thpd,pkd->thpk', q, sub_keys)
    s1, i1 = lax.top_k(s[:, :, 0], PEER_TOPK)
    s2, i2 = lax.top_k(s[:, :, 1], PEER_TOPK)
    cand_s = (s1[..., :, None] + s2[..., None, :]).reshape(t, PEER_HEADS, PEER_TOPK * PEER_TOPK)
    cand_i = (i1[..., :, None] * PEER_NKEYS + i2[..., None, :]).reshape(t, PEER_HEADS, PEER_TOPK * PEER_TOPK)
    top_s, pos = lax.top_k(cand_s, PEER_TOPK)
    idx = jnp.take_along_axis(cand_i, pos, axis=-1)
    wts = jax.nn.softmax(top_s, axis=-1)
    return idx.reshape(t, PEER_PAIRS).astype(jnp.int32), wts.reshape(t, PEER_PAIRS)


def kernel(x, c, ctx, c_ctx, w_mod, b_mod, ln_g, ln_b, a_w_in, a_norm_g, a_norm_b, a_w_s, a_b_s, a_w_out,
           b_w_in, b_w_gate, b_gate_bias, b_gn_g, b_w_out, p_w_q, p_keys, p_u, p_v):
    bsz, seq, d = x.shape
    ctx_len = ctx.shape[1]
    n_lat = bsz * seq
    xc = jnp.concatenate([x.reshape(n_lat, d), ctx.reshape(bsz * ctx_len, d)], axis=0)
    sc = jax.nn.silu(jnp.concatenate([c, c_ctx[None, :]], axis=0))
    lat_chunks = n_lat // CHUNK
    ctx_chunks = bsz * ctx_len // CHUNK
    chunk_sample = jnp.concatenate([jnp.arange(lat_chunks) // (seq // CHUNK),
                                    jnp.full((ctx_chunks,), bsz)]).astype(jnp.int32)
    for i in range(DEPTH):
        last = i == DEPTH - 1
        j = i // N_MIXERS
        m = (sc @ w_mod[i] + b_mod[i]).reshape(bsz + 1, 6, d)
        if last:
            n_tok = n_lat
        else:
            n_tok = xc.shape[0]
        n_chunks = n_tok // CHUNK
        mch = m[chunk_sample[:n_chunks]]
        xa = xc[:n_tok].reshape(n_chunks, CHUNK, d)
        if i % N_MIXERS == 0:
            h = xa * (1.0 + mch[:, 1, None, :]) + mch[:, 0, None, :]
            mix = _chunk_mlp(h, a_w_in[j], a_norm_g[j], a_norm_b[j], a_w_s[j], a_b_s[j], a_w_out[j])
        else:
            xall = xc.reshape(-1, CHUNK, d)
            mall = m[chunk_sample]
            h = xall * (1.0 + mall[:, 1, None, :]) + mall[:, 0, None, :]
            h_lat = h[:lat_chunks].reshape(bsz, seq, d)
            h_ctx = h[lat_chunks:].reshape(bsz, ctx_len, d)
            mix_lat, mix_ctx = _gla_mixer(h_lat, h_ctx, b_w_in[j], b_w_gate[j], b_gate_bias[j],
                                          b_gn_g[j], b_w_out[j], not last)
            mix = mix_lat.reshape(lat_chunks, CHUNK, d)
            if not last:
                mix = jnp.concatenate([mix, mix_ctx.reshape(ctx_chunks, CHUNK, d)], axis=0)
        x1 = _layer_norm(DN_ALPHA * xa + mch[:, 2, None, :] * mix, ln_g[i, 0], ln_b[i, 0]).reshape(n_tok, d)
        hq = (x1.reshape(n_chunks, CHUNK, d) * (1.0 + mch[:, 4, None, :]) + mch[:, 3, None, :]).reshape(n_tok, d)
        idx, wts = _peer_route(hq, p_w_q[i], p_keys[i])
        tab = jnp.concatenate([p_u[i], p_v[i]], axis=1)
        xc = _peer_experts(x1, m, idx, wts, tab, ln_g[i, 1], ln_b[i, 1],
                           tokens_per_sample_block=seq // PEER_TOKENS_PER_STEP)
    return xc[:n_lat].reshape(bsz, seq, d)
```

```python
import functools

import jax
import jax.numpy as jnp
from jax import lax
from jax.experimental import pallas as pl
from jax.experimental.pallas import tpu as pltpu

D_MODEL = 1024
DEPTH = 4
N_MIXERS = 2
DN_ALPHA = (2.0 * DEPTH) ** 0.25
LN_EPS = 1e-5
CHUNK = 128
A_WIDTH = D_MODEL
A_HEADS = 8
A_GROUP = A_WIDTH // A_HEADS
GLA_HEADS = 4
GLA_DK = D_MODEL // 2 // GLA_HEADS
GLA_DV = D_MODEL // GLA_HEADS
GLA_QK = GLA_HEADS * GLA_DK
GLA_V = GLA_HEADS * GLA_DV
GLA_GATE_RANK = 16
GLA_TAU = 16.0
GLA_CHUNK = 64
PEER_HEADS = 8
PEER_NKEYS = 128
PEER_QDIM = 128
PEER_HALF = PEER_QDIM // 2
PEER_TOPK = 16
PEER_PAIRS = PEER_HEADS * PEER_TOPK

PEER_TOKENS_PER_STEP = 128
PEER_ROW_BUFFERS = 8


def _layer_norm(x, g, b):
    mu = jnp.mean(x, axis=-1, keepdims=True)
    var = jnp.mean(jnp.square(x - mu), axis=-1, keepdims=True)
    return (x - mu) * lax.rsqrt(var + LN_EPS) * g + b


SUBLANES = 8
LANES = 128


def _sublane_sums(p):
    sub = lax.broadcasted_iota(jnp.int32, (SUBLANES, LANES), 0)

    def merge(a, b, sh, members):
        mask = functools.reduce(jnp.logical_or, [sub == j for j in members])
        return jnp.where(mask, a, b) + pltpu.roll(jnp.where(mask, b, a), sh, axis=0)

    c_e = merge(p[0], p[4], 4, (0, 5, 6, 7))
    d_e = merge(p[2], p[6], 4, (0, 1, 2, 7))
    c_f = merge(p[1], p[5], 4, (0, 1, 6, 7))
    d_f = merge(p[3], p[7], 4, (0, 1, 2, 3))
    e = merge(c_e, d_e, 2, (0, 3, 4, 7))
    f = merge(c_f, d_f, 2, (0, 1, 4, 5))
    return merge(e, f, 1, (0, 2, 4, 6))


def _peer_expert_kernel(idx_ref, idx_next_ref, w_ref, x_ref, mod_ref, lng_ref, lnb_ref, tab_ref,
                        out_ref, *scratch, tb, nbuf):
    rows = scratch[:nbuf]
    h_scr, ffn_scr, act_scr, sems = scratch[nbuf:]
    step = pl.program_id(0)
    last = pl.num_programs(0) - 1
    npairs = rows[0].shape[0]
    ahead = nbuf - 2
    sl = SUBLANES

    def issue(src_idx_ref, t, slot):
        for k in range(npairs):
            e = src_idx_ref[t, k]
            pltpu.make_async_copy(tab_ref.at[e], rows[slot].at[k], sems.at[slot]).start()

    def wait(slot):
        pltpu.make_async_copy(tab_ref.at[pl.ds(0, npairs)], rows[slot], sems.at[slot]).wait()

    @pl.when(step == 0)
    def _():
        for t in range(ahead):
            issue(idx_ref, t, t)

    s2 = mod_ref[0, 4]
    sh2 = mod_ref[0, 3]
    g2 = mod_ref[0, 5]
    h_scr[...] = x_ref[...] * (1.0 + s2) + sh2

    def token(t, slot, w_col):
        r = rows[slot]
        h = h_scr[t]
        sums = []
        for g in range(npairs // sl):
            sums.append(_sublane_sums([r[g * sl + j, 0:sl, :] * h for j in range(sl)]))
        s = jnp.sum(jnp.concatenate(sums, axis=0), axis=-1, keepdims=True)
        act_scr[...] = jnp.broadcast_to(jax.nn.gelu(s) * w_col, (npairs, LANES))
        accs = [jnp.zeros((sl, LANES), jnp.float32) for _ in range(4)]
        for k in range(npairs):
            a_k = jnp.broadcast_to(act_scr[pl.ds(k, 1), :], (sl, LANES))
            accs[k % 4] = accs[k % 4] + a_k * r[k, sl:2 * sl, :]
        ffn_scr[t] = (accs[0] + accs[1]) + (accs[2] + accs[3])

    n_groups = tb // nbuf

    def group(j8, is_last):
        w_t = w_ref[pl.ds(pl.multiple_of(j8 * nbuf, nbuf), nbuf), :].T
        for u in range(nbuf):
            t = j8 * nbuf + u
            wait(u)
            if not is_last or u + ahead < nbuf:
                issue(idx_ref, t + ahead, (u + ahead) % nbuf)
            else:
                issue(idx_next_ref, u + ahead - nbuf, (u + ahead) % nbuf)
            token(t, u, w_t[:, u:u + 1])

    def main_body(j8, carry):
        group(j8, False)
        return carry
    lax.fori_loop(0, n_groups - 1, main_body, 0)
    group(n_groups - 1, True)

    @pl.when(step == last)
    def _():
        for t in range(ahead):
            wait(t)

    y = DN_ALPHA * x_ref[...] + g2 * ffn_scr[...]
    n = y.shape[1] * y.shape[2]
    mu = jnp.sum(jnp.sum(y, axis=2, keepdims=True), axis=1, keepdims=True) / n
    yc = y - mu
    var = jnp.sum(jnp.sum(yc * yc, axis=2, keepdims=True), axis=1, keepdims=True) / n
    out_ref[...] = yc * lax.rsqrt(var + LN_EPS) * lng_ref[...] + lnb_ref[...]


def _peer_experts(x, mod, idx, w, tab, ln_g, ln_b, *, tokens_per_sample_block, interpret=False):
    t_total, d = x.shape
    npairs = idx.shape[1]
    tb = PEER_TOKENS_PER_STEP
    nbuf = PEER_ROW_BUFFERS
    sl = SUBLANES
    assert d == sl * LANES and nbuf == sl and t_total % tb == 0 and tb % nbuf == 0 and npairs % sl == 0
    n_steps = t_total // tb
    n_mod = mod.shape[0]
    kern = functools.partial(_peer_expert_kernel, tb=tb, nbuf=nbuf)
    out = pl.pallas_call(
        kern,
        grid=(n_steps,),
        in_specs=[
            pl.BlockSpec((tb, npairs), lambda i: (i, 0), memory_space=pltpu.SMEM),
            pl.BlockSpec((tb, npairs), lambda i: (jnp.minimum(i + 1, n_steps - 1), 0), memory_space=pltpu.SMEM),
            pl.BlockSpec((tb, npairs), lambda i: (i, 0)),
            pl.BlockSpec((tb, sl, LANES), lambda i: (i, 0, 0)),
            pl.BlockSpec((1, 6, sl, LANES),
                         lambda i: (jnp.minimum(i // tokens_per_sample_block, n_mod - 1), 0, 0, 0)),
            pl.BlockSpec((sl, LANES), lambda i: (0, 0)),
            pl.BlockSpec((sl, LANES), lambda i: (0, 0)),
            pl.BlockSpec(memory_space=pl.ANY),
        ],
        out_specs=pl.BlockSpec((tb, sl, LANES), lambda i: (i, 0, 0)),
        out_shape=jax.ShapeDtypeStruct((t_total, sl, LANES), jnp.float32),
        scratch_shapes=[pltpu.VMEM((npairs, 2 * sl, LANES), jnp.float32) for _ in range(nbuf)] + [
            pltpu.VMEM((tb, sl, LANES), jnp.float32),
            pltpu.VMEM((tb, sl, LANES), jnp.float32),
            pltpu.VMEM((npairs, LANES), jnp.float32),
            pltpu.SemaphoreType.DMA((nbuf,)),
        ],
        compiler_params=pltpu.CompilerParams(dimension_semantics=("arbitrary",),
                                             vmem_limit_bytes=40 * 1024 * 1024),
        interpret=interpret,
    )(idx, idx, w, x.reshape(t_total, sl, LANES), mod.reshape(n_mod, 6, sl, LANES),
      ln_g.reshape(sl, LANES), ln_b.reshape(sl, LANES), tab.reshape(tab.shape[0], 2 * sl, LANES))
    return out.reshape(t_total, d)


_NEG_INF = float("-inf")


def _topk_rows(s, k):
    n, width = s.shape
    rows = lax.broadcasted_iota(jnp.int32, s.shape, 0)
    out_rows = lax.broadcasted_iota(jnp.int32, (k, width), 0)
    vals = jnp.zeros((k, width), jnp.float32)
    idxs = jnp.zeros((k, width), jnp.int32)
    for r in range(k):
        m = jnp.max(s, axis=0, keepdims=True)
        first = jnp.min(jnp.where(s == m, rows, n), axis=0, keepdims=True)
        vals = jnp.where(out_rows == r, m, vals)
        idxs = jnp.where(out_rows == r, first, idxs)
        s = jnp.where(rows == first, _NEG_INF, s)
    return vals, idxs


def _pair_candidates(s1, s2):
    k = PEER_TOPK
    sub = lax.broadcasted_iota(jnp.int32, (8, 1), 0)
    groups = []

    def add(vals, a, b, keep):
        ok = keep & ((a + 1) * (b + 1) <= k)
        groups.append((jnp.where(ok, vals, _NEG_INF), a * k + b))

    for b0 in (0, 8):
        add(s1[0:1, :] + s2[b0:b0 + 8, :], 0 * sub, b0 + sub, sub >= 0)
    add(s1[8:16, :] + s2[0:1, :], 8 + sub, 0 * sub, sub >= 0)
    for b in range(5):
        add(s1[0:8, :] + s2[b:b + 1, :], sub, b + 0 * sub, sub >= 1)
    add(s1[1:2, :] + s2[0:8, :], 1 + 0 * sub, sub, sub >= 5)
    return groups


def _topk_groups(groups, k):
    width = groups[0][0].shape[1]
    big = PEER_TOPK * PEER_TOPK
    out_rows = lax.broadcasted_iota(jnp.int32, (k, width), 0)
    vals = jnp.zeros((k, width), jnp.float32)
    poss = jnp.zeros((k, width), jnp.int32)
    gv = [g[0] for g in groups]
    gp = [g[1] for g in groups]
    for r in range(k):
        m = functools.reduce(jnp.maximum, gv)
        m = jnp.max(m, axis=0, keepdims=True)
        f = functools.reduce(jnp.minimum, [jnp.where(v == m, p, big) for v, p in zip(gv, gp)])
        f = jnp.min(f, axis=0, keepdims=True)
        vals = jnp.where(out_rows == r, m, vals)
        poss = jnp.where(out_rows == r, f, poss)
        gv = [jnp.where(p == f, _NEG_INF, v) for v, p in zip(gv, gp)]
    return vals, poss


def _select_rows(table, sel):
    out = jnp.zeros_like(table)
    for a in range(table.shape[0]):
        out = jnp.where(sel == a, table[a:a + 1, :], out)
    return out


def _peer_route_kernel(x_ref, mod_ref, wq_ref, keys_ref, idx_ref, w_ref, idx_t, w_t):
    k = PEER_TOPK
    s2 = mod_ref[0, 4:5, :]
    sh2 = mod_ref[0, 3:4, :]
    h = (x_ref[...] * (1.0 + s2) + sh2).astype(jnp.bfloat16)

    def head(hd, carry):
        q = jnp.dot(h, wq_ref[hd], preferred_element_type=jnp.float32).astype(jnp.bfloat16)
        nt = (((1,), (1,)), ((), ()))
        sc1 = lax.dot_general(keys_ref[0], q, nt, preferred_element_type=jnp.float32)
        sc2 = lax.dot_general(keys_ref[1], q, nt, preferred_element_type=jnp.float32)
        v1, i1 = _topk_rows(sc1, k)
        v2, i2 = _topk_rows(sc2, k)
        top_s, pos = _topk_groups(_pair_candidates(v1, v2), k)
        e = jnp.exp(top_s - top_s[0:1, :])
        wts = e / jnp.sum(e, axis=0, keepdims=True)
        a = lax.shift_right_logical(pos, 4)
        b = lax.bitwise_and(pos, k - 1)
        ids = _select_rows(i1, a) * PEER_NKEYS + _select_rows(i2, b)
        row0 = pl.multiple_of(hd * k, k)
        idx_t[pl.ds(row0, k), :] = ids
        w_t[pl.ds(row0, k), :] = wts
        return carry

    lax.fori_loop(0, PEER_HEADS, head, 0)
    idx_ref[...] = idx_t[...].T
    w_ref[...] = w_t[...].T


def _peer_route_pallas(x, mod, w_q, sub_keys, *, tokens_per_sample_block, interpret=False):
    t_total, d = x.shape
    tb = 128
    n_steps = t_total // tb
    n_mod = mod.shape[0]
    wq = w_q.reshape(d, PEER_HEADS, PEER_QDIM).transpose(1, 0, 2).astype(jnp.bfloat16)
    zeros = jnp.zeros((PEER_NKEYS, PEER_HALF), sub_keys.dtype)
    keys = jnp.stack([jnp.concatenate([sub_keys[0], zeros], axis=1),
                      jnp.concatenate([zeros, sub_keys[1]], axis=1)]).astype(jnp.bfloat16)
    return pl.pallas_call(
        _peer_route_kernel,
        grid=(n_steps,),
        in_specs=[
            pl.BlockSpec((tb, d), lambda i: (i, 0)),
            pl.BlockSpec((1, 6, d), lambda i: (jnp.minimum(i // tokens_per_sample_block, n_mod - 1), 0, 0)),
            pl.BlockSpec((PEER_HEADS, d, PEER_QDIM), lambda i: (0, 0, 0)),
            pl.BlockSpec((2, PEER_NKEYS, PEER_QDIM), lambda i: (0, 0, 0)),
        ],
        out_specs=[pl.BlockSpec((tb, PEER_PAIRS), lambda i: (i, 0)),
                   pl.BlockSpec((tb, PEER_PAIRS), lambda i: (i, 0))],
        out_shape=[jax.ShapeDtypeStruct((t_total, PEER_PAIRS), jnp.int32),
                   jax.ShapeDtypeStruct((t_total, PEER_PAIRS), jnp.float32)],
        scratch_shapes=[pltpu.VMEM((PEER_PAIRS, tb), jnp.int32), pltpu.VMEM((PEER_PAIRS, tb), jnp.float32)],
        compiler_params=pltpu.CompilerParams(dimension_semantics=("arbitrary",)),
        interpret=interpret,
    )(x, mod, wq, keys)


def _chunk_mlp(h, w_in, norm_g, norm_b, w_s, b_s, w_out):
    n = h.shape[0]
    z = jax.nn.gelu(h @ w_in)
    u, v = jnp.split(z, 2, axis=-1)
    v = _layer_norm(v, norm_g, norm_b)
    vb = v.reshape(n, CHUNK, A_HEADS, A_GROUP)
    s = jnp.einsum('hpq,nqhc->nphc', w_s, vb) + b_s.T[None, :, :, None]
    return (u * s.reshape(n, CHUNK, A_WIDTH)) @ w_out


def _gla_chunked(q, k, v, g, s0):
    bsz, t, h, dk = q.shape
    dv = v.shape[-1]
    n = t // GLA_CHUNK
    q = q.reshape(bsz, n, GLA_CHUNK, h, dk)
    k = k.reshape(bsz, n, GLA_CHUNK, h, dk)
    v = v.reshape(bsz, n, GLA_CHUNK, h, dv)
    b = jnp.cumsum(g.reshape(bsz, n, GLA_CHUNK, h, dk), axis=2)
    b_end = b[:, :, -1:]
    q_in = q * jnp.exp(b)
    k_in = k * jnp.exp(-b)
    k_end = k * jnp.exp(b_end - b)
    att = jnp.einsum('bnihd,bnjhd->bnhij', q_in, k_in)
    mask = jnp.tril(jnp.ones((GLA_CHUNK, GLA_CHUNK), dtype=bool))
    att = jnp.where(mask, att, 0.0)
    o_intra = jnp.einsum('bnhij,bnjhe->bnihe', att, v)

    def step(state, inp):
        qc, kc, vc, dc = inp
        o = jnp.einsum('bihd,bhde->bihe', qc, state)
        state = dc[..., None] * state + jnp.einsum('bjhd,bjhe->bhde', kc, vc)
        return state, o

    xs = (jnp.moveaxis(q_in, 1, 0), jnp.moveaxis(k_end, 1, 0), jnp.moveaxis(v, 1, 0),
          jnp.moveaxis(jnp.exp(b_end[:, :, 0]), 1, 0))
    s_fin, o_inter = lax.scan(step, s0, xs)
    o = o_intra + jnp.moveaxis(o_inter, 0, 1)
    return o.reshape(bsz, t, h, dv), s_fin


def _gla_reverse(q, k, v, g, s0):
    o, s_fin = _gla_chunked(jnp.flip(q, 1), jnp.flip(k, 1), jnp.flip(v, 1), jnp.flip(g, 1), s0)
    return jnp.flip(o, 1), s_fin


def _gla_project(h, w_in, w_gate, gate_bias):
    bsz, t, _ = h.shape
    z = h @ w_in
    q, k, v, r, gl = jnp.split(z, [GLA_QK, 2 * GLA_QK, 2 * GLA_QK + GLA_V, 2 * GLA_QK + 2 * GLA_V], axis=-1)
    q = q.reshape(bsz, t, GLA_HEADS, GLA_DK) * (GLA_DK ** -0.5)
    k = k.reshape(bsz, t, GLA_HEADS, GLA_DK)
    v = v.reshape(bsz, t, GLA_HEADS, GLA_DV)
    gl = gl.reshape(bsz, t, 2, GLA_GATE_RANK)
    zg = jnp.einsum('btjr,jrd->btjd', gl, w_gate) + gate_bias
    glog = (jax.nn.log_sigmoid(zg) / GLA_TAU).reshape(bsz, t, 2, GLA_HEADS, GLA_DK)
    return q, k, v, r, glog


def _gla_readout(o, r, gn_g, w_out):
    bsz, t = o.shape[:2]
    mu = jnp.mean(o, axis=-1, keepdims=True)
    var = jnp.mean(jnp.square(o - mu), axis=-1, keepdims=True)
    y = ((o - mu) * lax.rsqrt(var + LN_EPS)).reshape(bsz, t, GLA_V) * gn_g
    return (y * jax.nn.silu(r)) @ w_out


def _gla_mixer(h_lat, h_ctx, w_in, w_gate, gate_bias, gn_g, w_out, need_ctx_out):
    qc, kc, vc, rc, gc = _gla_project(h_ctx, w_in, w_gate, gate_bias)
    ql, kl, vl, rl, gll = _gla_project(h_lat, w_in, w_gate, gate_bias)
    s0 = jnp.zeros((h_ctx.shape[0], GLA_HEADS, GLA_DK, GLA_DV), jnp.float32)
    o_cf, s_f = _gla_chunked(qc, kc, vc, gc[:, :, 0], s0)
    o_cb, s_b = _gla_reverse(qc, kc, vc, gc[:, :, 1], s0)
    o_lf, _ = _gla_chunked(ql, kl, vl, gll[:, :, 0], s_f)
    o_lb, _ = _gla_reverse(ql, kl, vl, gll[:, :, 1], s_b)
    out_lat = _gla_readout(o_lf + o_lb, rl, gn_g, w_out)
    out_ctx = _gla_readout(o_cf + o_cb, rc, gn_g, w_out) if need_ctx_out else None
    return out_lat, out_ctx


def kernel(x, c, ctx, c_ctx, w_mod, b_mod, ln_g, ln_b, a_w_in, a_norm_g, a_norm_b, a_w_s, a_b_s, a_w_out,
           b_w_in, b_w_gate, b_gate_bias, b_gn_g, b_w_out, p_w_q, p_keys, p_u, p_v):
    bsz, seq, d = x.shape
    ctx_len = ctx.shape[1]
    n_lat = bsz * seq
    xc = jnp.concatenate([x.reshape(n_lat, d), ctx.reshape(bsz * ctx_len, d)], axis=0)
    sc = jax.nn.silu(jnp.concatenate([c, c_ctx[None, :]], axis=0))
    lat_chunks = n_lat // CHUNK
    ctx_chunks = bsz * ctx_len // CHUNK
    chunk_sample = jnp.concatenate([jnp.arange(lat_chunks) // (seq // CHUNK),
                                    jnp.full((ctx_chunks,), bsz)]).astype(jnp.int32)
    for i in range(DEPTH):
        last = i == DEPTH - 1
        j = i // N_MIXERS
        m = (sc @ w_mod[i] + b_mod[i]).reshape(bsz + 1, 6, d)
        if last:
            n_tok = n_lat
        else:
            n_tok = xc.shape[0]
        n_chunks = n_tok // CHUNK
        mch = m[chunk_sample[:n_chunks]]
        xa = xc[:n_tok].reshape(n_chunks, CHUNK, d)
        if i % N_MIXERS == 0:
            h = xa * (1.0 + mch[:, 1, None, :]) + mch[:, 0, None, :]
            mix = _chunk_mlp(h, a_w_in[j], a_norm_g[j], a_norm_b[j], a_w_s[j], a_b_s[j], a_w_out[j])
        else:
            xall = xc.reshape(-1, CHUNK, d)
            mall = m[chunk_sample]
            h = xall * (1.0 + mall[:, 1, None, :]) + mall[:, 0, None, :]
            h_lat = h[:lat_chunks].reshape(bsz, seq, d)
            h_ctx = h[lat_chunks:].reshape(bsz, ctx_len, d)
            mix_lat, mix_ctx = _gla_mixer(h_lat, h_ctx, b_w_in[j], b_w_gate[j], b_gate_bias[j],
                                          b_gn_g[j], b_w_out[j], not last)
            mix = mix_lat.reshape(lat_chunks, CHUNK, d)
            if not last:
                mix = jnp.concatenate([mix, mix_ctx.reshape(ctx_chunks, CHUNK, d)], axis=0)
        x1 = _layer_norm(DN_ALPHA * xa + mch[:, 2, None, :] * mix, ln_g[i, 0], ln_b[i, 0]).reshape(n_tok, d)
        idx, wts = _peer_route_pallas(x1, m, p_w_q[i], p_keys[i], tokens_per_sample_block=seq // 128)
        tab = jnp.stack([p_u[i], p_v[i]], axis=1)
        xc = _peer_experts(x1, m, idx, wts, tab, ln_g[i, 1], ln_b[i, 1],
                           tokens_per_sample_block=seq // PEER_TOKENS_PER_STEP)
    return xc[:n_lat].reshape(bsz, seq, d)
```

```python
import functools

import jax
import jax.numpy as jnp
from jax import lax
from jax.experimental import pallas as pl
from jax.experimental.pallas import tpu as pltpu

D_MODEL = 1024
DEPTH = 4
N_MIXERS = 2
DN_ALPHA = (2.0 * DEPTH) ** 0.25
LN_EPS = 1e-5
CHUNK = 128
A_WIDTH = D_MODEL
A_HEADS = 8
A_GROUP = A_WIDTH // A_HEADS
GLA_HEADS = 4
GLA_DK = D_MODEL // 2 // GLA_HEADS
GLA_DV = D_MODEL // GLA_HEADS
GLA_QK = GLA_HEADS * GLA_DK
GLA_V = GLA_HEADS * GLA_DV
GLA_GATE_RANK = 16
GLA_TAU = 16.0
GLA_CHUNK = 64
PEER_HEADS = 8
PEER_NKEYS = 128
PEER_QDIM = 128
PEER_HALF = PEER_QDIM // 2
PEER_TOPK = 16
PEER_PAIRS = PEER_HEADS * PEER_TOPK

PEER_TOKENS_PER_STEP = 128
PEER_ROW_BUFFERS = 8


def _layer_norm(x, g, b):
    mu = jnp.mean(x, axis=-1, keepdims=True)
    var = jnp.mean(jnp.square(x - mu), axis=-1, keepdims=True)
    return (x - mu) * lax.rsqrt(var + LN_EPS) * g + b


SUBLANES = 8
LANES = 128


def _sublane_sums(p):
    sub = lax.broadcasted_iota(jnp.int32, (SUBLANES, LANES), 0)

    def merge(a, b, sh, members):
        mask = functools.reduce(jnp.logical_or, [sub == j for j in members])
        return jnp.where(mask, a, b) + pltpu.roll(jnp.where(mask, b, a), sh, axis=0)

    c_e = merge(p[0], p[4], 4, (0, 5, 6, 7))
    d_e = merge(p[2], p[6], 4, (0, 1, 2, 7))
    c_f = merge(p[1], p[5], 4, (0, 1, 6, 7))
    d_f = merge(p[3], p[7], 4, (0, 1, 2, 3))
    e = merge(c_e, d_e, 2, (0, 3, 4, 7))
    f = merge(c_f, d_f, 2, (0, 1, 4, 5))
    return merge(e, f, 1, (0, 2, 4, 6))


def _peer_expert_kernel(idx_ref, idx_next_ref, w_ref, x_ref, mod_ref, lng_ref, lnb_ref, tab_ref,
                        out_ref, *scratch, tb, nbuf):
    rows = scratch[:nbuf]
    h_scr, ffn_scr, act_scr, sems = scratch[nbuf:]
    step = pl.program_id(0)
    last = pl.num_programs(0) - 1
    npairs = rows[0].shape[0]
    ahead = nbuf - 2
    sl = SUBLANES

    def issue(src_idx_ref, t, slot):
        for k in range(npairs):
            e = src_idx_ref[t, k]
            pltpu.make_async_copy(tab_ref.at[e], rows[slot].at[k], sems.at[slot]).start(priority=k % 2)

    def wait(slot):
        pltpu.make_async_copy(tab_ref.at[pl.ds(0, npairs)], rows[slot], sems.at[slot]).wait()

    @pl.when(step == 0)
    def _():
        for t in range(ahead):
            issue(idx_ref, t, t)

    s2 = mod_ref[0, 4]
    sh2 = mod_ref[0, 3]
    g2 = mod_ref[0, 5]
    h_scr[...] = x_ref[...] * (1.0 + s2) + sh2

    def token(t, slot, w_col):
        r = rows[slot]
        h = h_scr[t]
        sums = []
        for g in range(npairs // sl):
            sums.append(_sublane_sums([r[g * sl + j, 0:sl, :] * h for j in range(sl)]))
        s = jnp.sum(jnp.concatenate(sums, axis=0), axis=-1, keepdims=True)
        act_scr[...] = jnp.broadcast_to(jax.nn.gelu(s) * w_col, (npairs, LANES))
        accs = [jnp.zeros((sl, LANES), jnp.float32) for _ in range(4)]
        for k in range(npairs):
            a_k = jnp.broadcast_to(act_scr[pl.ds(k, 1), :], (sl, LANES))
            accs[k % 4] = accs[k % 4] + a_k * r[k, sl:2 * sl, :]
        ffn_scr[t] = (accs[0] + accs[1]) + (accs[2] + accs[3])

    n_groups = tb // nbuf

    def group(j8, is_last):
        w_t = w_ref[pl.ds(pl.multiple_of(j8 * nbuf, nbuf), nbuf), :].T
        for u in range(nbuf):
            t = j8 * nbuf + u
            wait(u)
            if not is_last or u + ahead < nbuf:
                issue(idx_ref, t + ahead, (u + ahead) % nbuf)
            else:
                issue(idx_next_ref, u + ahead - nbuf, (u + ahead) % nbuf)
            token(t, u, w_t[:, u:u + 1])

    def main_body(j8, carry):
        group(j8, False)
        return carry
    lax.fori_loop(0, n_groups - 1, main_body, 0)
    group(n_groups - 1, True)

    @pl.when(step == last)
    def _():
        for t in range(ahead):
            wait(t)

    y = DN_ALPHA * x_ref[...] + g2 * ffn_scr[...]
    n = y.shape[1] * y.shape[2]
    mu = jnp.sum(jnp.sum(y, axis=2, keepdims=True), axis=1, keepdims=True) / n
    yc = y - mu
    var = jnp.sum(jnp.sum(yc * yc, axis=2, keepdims=True), axis=1, keepdims=True) / n
    out_ref[...] = yc * lax.rsqrt(var + LN_EPS) * lng_ref[...] + lnb_ref[...]


def _peer_experts(x, mod, idx, w, tab, ln_g, ln_b, *, tokens_per_sample_block):
    t_total, d = x.shape
    npairs = idx.shape[1]
    tb = PEER_TOKENS_PER_STEP
    nbuf = PEER_ROW_BUFFERS
    sl = SUBLANES
    assert d == sl * LANES and nbuf == sl and t_total % tb == 0 and tb % nbuf == 0 and npairs % sl == 0
    n_steps = t_total // tb
    n_mod = mod.shape[0]
    kern = functools.partial(_peer_expert_kernel, tb=tb, nbuf=nbuf)
    out = pl.pallas_call(
        kern,
        grid=(n_steps,),
        in_specs=[
            pl.BlockSpec((tb, npairs), lambda i: (i, 0), memory_space=pltpu.SMEM),
            pl.BlockSpec((tb, npairs), lambda i: (jnp.minimum(i + 1, n_steps - 1), 0), memory_space=pltpu.SMEM),
            pl.BlockSpec((tb, npairs), lambda i: (i, 0)),
            pl.BlockSpec((tb, sl, LANES), lambda i: (i, 0, 0)),
            pl.BlockSpec((1, 6, sl, LANES),
                         lambda i: (jnp.minimum(i // tokens_per_sample_block, n_mod - 1), 0, 0, 0)),
            pl.BlockSpec((sl, LANES), lambda i: (0, 0)),
            pl.BlockSpec((sl, LANES), lambda i: (0, 0)),
            pl.BlockSpec(memory_space=pl.ANY),
        ],
        out_specs=pl.BlockSpec((tb, sl, LANES), lambda i: (i, 0, 0)),
        out_shape=jax.ShapeDtypeStruct((t_total, sl, LANES), jnp.float32),
        scratch_shapes=[pltpu.VMEM((npairs, 2 * sl, LANES), jnp.float32) for _ in range(nbuf)] + [
            pltpu.VMEM((tb, sl, LANES), jnp.float32),
            pltpu.VMEM((tb, sl, LANES), jnp.float32),
            pltpu.VMEM((npairs, LANES), jnp.float32),
            pltpu.SemaphoreType.DMA((nbuf,)),
        ],
        compiler_params=pltpu.CompilerParams(dimension_semantics=("arbitrary",),
                                             vmem_limit_bytes=40 * 1024 * 1024),
    )(idx, idx, w, x.reshape(t_total, sl, LANES), mod.reshape(n_mod, 6, sl, LANES),
      ln_g.reshape(sl, LANES), ln_b.reshape(sl, LANES), tab.reshape(tab.shape[0], 2 * sl, LANES))
    return out.reshape(t_total, d)


_NEG_INF = float("-inf")


def _topk_rows(s, k):
    n, width = s.shape
    rows = lax.broadcasted_iota(jnp.int32, s.shape, 0)
    out_rows = lax.broadcasted_iota(jnp.int32, (k, width), 0)
    vals = jnp.zeros((k, width), jnp.float32)
    idxs = jnp.zeros((k, width), jnp.int32)
    for r in range(k):
        m = jnp.max(s, axis=0, keepdims=True)
        first = jnp.min(jnp.where(s == m, rows, n), axis=0, keepdims=True)
        vals = jnp.where(out_rows == r, m, vals)
        idxs = jnp.where(out_rows == r, first, idxs)
        s = jnp.where(rows == first, _NEG_INF, s)
    return vals, idxs


def _pair_candidates(s1, s2):
    k = PEER_TOPK
    sub = lax.broadcasted_iota(jnp.int32, (8, 1), 0)
    groups = []

    def add(vals, a, b, keep):
        ok = keep & ((a + 1) * (b + 1) <= k)
        groups.append((jnp.where(ok, vals, _NEG_INF), a * k + b))

    for b0 in (0, 8):
        add(s1[0:1, :] + s2[b0:b0 + 8, :], 0 * sub, b0 + sub, sub >= 0)
    add(s1[8:16, :] + s2[0:1, :], 8 + sub, 0 * sub, sub >= 0)
    for b in range(5):
        add(s1[0:8, :] + s2[b:b + 1, :], sub, b + 0 * sub, sub >= 1)
    add(s1[1:2, :] + s2[0:8, :], 1 + 0 * sub, sub, sub >= 5)
    return groups


def _topk_groups(groups, k):
    width = groups[0][0].shape[1]
    big = PEER_TOPK * PEER_TOPK
    out_rows = lax.broadcasted_iota(jnp.int32, (k, width), 0)
    vals = jnp.zeros((k, width), jnp.float32)
    poss = jnp.zeros((k, width), jnp.int32)
    gv = [g[0] for g in groups]
    gp = [g[1] for g in groups]
    for r in range(k):
        m = functools.reduce(jnp.maximum, gv)
        m = jnp.max(m, axis=0, keepdims=True)
        f = functools.reduce(jnp.minimum, [jnp.where(v == m, p, big) for v, p in zip(gv, gp)])
        f = jnp.min(f, axis=0, keepdims=True)
        vals = jnp.where(out_rows == r, m, vals)
        poss = jnp.where(out_rows == r, f, poss)
        gv = [jnp.where(p == f, _NEG_INF, v) for v, p in zip(gv, gp)]
    return vals, poss


def _select_rows(table, sel):
    out = jnp.zeros_like(table)
    for a in range(table.shape[0]):
        out = jnp.where(sel == a, table[a:a + 1, :], out)
    return out


def _peer_route_kernel(x_ref, mod_ref, wq_ref, keys_ref, idx_ref, w_ref, idx_t, w_t):
    k = PEER_TOPK
    s2 = mod_ref[0, 4:5, :]
    sh2 = mod_ref[0, 3:4, :]
    h = (x_ref[...] * (1.0 + s2) + sh2).astype(jnp.bfloat16)

    def head(hd, carry):
        q = jnp.dot(h, wq_ref[hd], preferred_element_type=jnp.float32).astype(jnp.bfloat16)
        nt = (((1,), (1,)), ((), ()))
        sc1 = lax.dot_general(keys_ref[0], q, nt, preferred_element_type=jnp.float32)
        sc2 = lax.dot_general(keys_ref[1], q, nt, preferred_element_type=jnp.float32)
        v1, i1 = _topk_rows(sc1, k)
        v2, i2 = _topk_rows(sc2, k)
        top_s, pos = _topk_groups(_pair_candidates(v1, v2), k)
        e = jnp.exp(top_s - top_s[0:1, :])
        wts = e / jnp.sum(e, axis=0, keepdims=True)
        a = lax.shift_right_logical(pos, 4)
        b = lax.bitwise_and(pos, k - 1)
        ids = _select_rows(i1, a) * PEER_NKEYS + _select_rows(i2, b)
        row0 = pl.multiple_of(hd * k, k)
        idx_t[pl.ds(row0, k), :] = ids
        w_t[pl.ds(row0, k), :] = wts
        return carry

    lax.fori_loop(0, PEER_HEADS, head, 0)
    idx_ref[...] = idx_t[...].T
    w_ref[...] = w_t[...].T


def _peer_route_pallas(x, mod, w_q, sub_keys, *, tokens_per_sample_block):
    t_total, d = x.shape
    tb = 128
    n_steps = t_total // tb
    n_mod = mod.shape[0]
    wq = w_q.reshape(d, PEER_HEADS, PEER_QDIM).transpose(1, 0, 2).astype(jnp.bfloat16)
    zeros = jnp.zeros((PEER_NKEYS, PEER_HALF), sub_keys.dtype)
    keys = jnp.stack([jnp.concatenate([sub_keys[0], zeros], axis=1),
                      jnp.concatenate([zeros, sub_keys[1]], axis=1)]).astype(jnp.bfloat16)
    return pl.pallas_call(
        _peer_route_kernel,
        grid=(n_steps,),
        in_specs=[
            pl.BlockSpec((tb, d), lambda i: (i, 0)),
            pl.BlockSpec((1, 6, d), lambda i: (jnp.minimum(i // tokens_per_sample_block, n_mod - 1), 0, 0)),
            pl.BlockSpec((PEER_HEADS, d, PEER_QDIM), lambda i: (0, 0, 0)),
            pl.BlockSpec((2, PEER_NKEYS, PEER_QDIM), lambda i: (0, 0, 0)),
        ],
        out_specs=[pl.BlockSpec((tb, PEER_PAIRS), lambda i: (i, 0)),
                   pl.BlockSpec((tb, PEER_PAIRS), lambda i: (i, 0))],
        out_shape=[jax.ShapeDtypeStruct((t_total, PEER_PAIRS), jnp.int32),
                   jax.ShapeDtypeStruct((t_total, PEER_PAIRS), jnp.float32)],
        scratch_shapes=[pltpu.VMEM((PEER_PAIRS, tb), jnp.int32), pltpu.VMEM((PEER_PAIRS, tb), jnp.float32)],
        compiler_params=pltpu.CompilerParams(dimension_semantics=("arbitrary",)),
    )(x, mod, wq, keys)


MOD_COLS_PER_STEP = 1536


def _modulation_kernel(sc_ref, w_ref, b_ref, out_ref):
    acc = jnp.dot(sc_ref[...].astype(jnp.bfloat16), w_ref[0].astype(jnp.bfloat16),
                  preferred_element_type=jnp.float32)
    out_ref[0] = acc + b_ref[0]


def _modulation(sc, w_mod, b_mod):
    n_layers, d, n_out = w_mod.shape
    s = sc.shape[0]
    tn = MOD_COLS_PER_STEP
    return pl.pallas_call(
        _modulation_kernel,
        grid=(n_layers, n_out // tn),
        in_specs=[pl.BlockSpec((s, d), lambda l, j: (0, 0)),
                  pl.BlockSpec((1, d, tn), lambda l, j: (l, 0, j)),
                  pl.BlockSpec((1, 1, tn), lambda l, j: (l, 0, j))],
        out_specs=pl.BlockSpec((1, s, tn), lambda l, j: (l, 0, j)),
        out_shape=jax.ShapeDtypeStruct((n_layers, s, n_out), jnp.float32),
        compiler_params=pltpu.CompilerParams(dimension_semantics=("arbitrary", "arbitrary"),
                                             vmem_limit_bytes=40 * 1024 * 1024),
    )(sc, w_mod, b_mod.reshape(n_layers, 1, n_out))


def _residual_layer_norm(x, g_mod, mix, ln_g, ln_b):
    y = DN_ALPHA * x + g_mod * mix
    return _layer_norm(y, ln_g, ln_b)


MIXER_ROWS_PER_STEP = 256


def _chunk_mlp_kernel(x_ref, mod_ref, win_ref, ng_ref, nb_ref, ws_ref, bs_ref, wout_ref, lng_ref, lnb_ref,
                      out_ref, gated):
    x = x_ref[...]
    sh1 = mod_ref[0, 0:1, :]
    s1 = mod_ref[0, 1:2, :]
    g1 = mod_ref[0, 2:3, :]
    h = (x * (1.0 + s1) + sh1).astype(jnp.bfloat16)
    z = jax.nn.gelu(jnp.dot(h, win_ref[...], preferred_element_type=jnp.float32))
    u = z[:, :A_WIDTH]
    v = _layer_norm(z[:, A_WIDTH:], ng_ref[...], nb_ref[...]).astype(jnp.bfloat16)
    for c in range(x.shape[0] // CHUNK):
        r0 = c * CHUNK
        for hd in range(A_HEADS):
            c0 = hd * A_GROUP
            s = jnp.dot(ws_ref[hd], v[r0:r0 + CHUNK, c0:c0 + A_GROUP], preferred_element_type=jnp.float32)
            s = s + bs_ref[:, hd:hd + 1]
            gated[r0:r0 + CHUNK, c0:c0 + A_GROUP] = (u[r0:r0 + CHUNK, c0:c0 + A_GROUP] * s).astype(jnp.bfloat16)
    mix = jnp.dot(gated[...], wout_ref[...], preferred_element_type=jnp.float32)
    out_ref[...] = _residual_layer_norm(x, g1, mix, lng_ref[...], lnb_ref[...])


def _chunk_mlp_mixer(x, mod, w_in, norm_g, norm_b, w_s, b_s, w_out, ln_g, ln_b, *, rows_per_sample):
    t_total, d = x.shape
    rows = MIXER_ROWS_PER_STEP
    n_mod = mod.shape[0]
    blocks_per_sample = rows_per_sample // rows
    const = lambda *shape: pl.BlockSpec(shape, lambda i: (0,) * len(shape))
    return pl.pallas_call(
        _chunk_mlp_kernel,
        grid=(t_total // rows,),
        in_specs=[pl.BlockSpec((rows, d), lambda i: (i, 0)),
                  pl.BlockSpec((1, 6, d), lambda i: (jnp.minimum(i // blocks_per_sample, n_mod - 1), 0, 0)),
                  const(d, 2 * A_WIDTH), const(1, A_WIDTH), const(1, A_WIDTH),
                  const(A_HEADS, CHUNK, CHUNK), const(CHUNK, A_HEADS), const(A_WIDTH, d),
                  const(1, d), const(1, d)],
        out_specs=pl.BlockSpec((rows, d), lambda i: (i, 0)),
        out_shape=jax.ShapeDtypeStruct((t_total, d), jnp.float32),
        scratch_shapes=[pltpu.VMEM((rows, A_WIDTH), jnp.bfloat16)],
        compiler_params=pltpu.CompilerParams(dimension_semantics=("arbitrary",),
                                             vmem_limit_bytes=48 * 1024 * 1024),
    )(x, mod, w_in.astype(jnp.bfloat16), norm_g.reshape(1, -1), norm_b.reshape(1, -1),
      w_s.astype(jnp.bfloat16), b_s.T, w_out.astype(jnp.bfloat16), ln_g.reshape(1, d), ln_b.reshape(1, d))


GLA_MAIN = 2 * GLA_QK + 2 * GLA_V
GLA_GATE_PAD = 128


def _gla_project_kernel(x_ref, mod_ref, win_ref, wgin_ref, wg_ref, gb_ref, q_ref, k_ref, v_ref, r_ref, g_ref):
    sh1 = mod_ref[0, 0:1, :]
    s1 = mod_ref[0, 1:2, :]
    h = (x_ref[...] * (1.0 + s1) + sh1).astype(jnp.bfloat16)
    z = jnp.dot(h, win_ref[...], preferred_element_type=jnp.float32)
    q_ref[...] = z[:, :GLA_QK] * (GLA_DK ** -0.5)
    k_ref[...] = z[:, GLA_QK:2 * GLA_QK]
    v_ref[...] = z[:, 2 * GLA_QK:2 * GLA_QK + GLA_V]
    r_ref[...] = z[:, 2 * GLA_QK + GLA_V:]
    gl = jnp.dot(h, wgin_ref[...], preferred_element_type=jnp.float32).astype(jnp.bfloat16)
    zg = jnp.dot(gl, wg_ref[...], preferred_element_type=jnp.float32) + gb_ref[...]
    g_ref[...] = jax.nn.log_sigmoid(zg) / GLA_TAU


def _gla_project(x, mod, w_in, w_gate, gate_bias, *, rows_per_sample):
    t_total, d = x.shape
    rows = MIXER_ROWS_PER_STEP
    n_mod = mod.shape[0]
    blocks_per_sample = rows_per_sample // rows
    rank = GLA_GATE_RANK
    w_main = w_in[:, :GLA_MAIN].astype(jnp.bfloat16)
    w_gin = jnp.pad(w_in[:, GLA_MAIN:], ((0, 0), (0, GLA_GATE_PAD - 2 * rank))).astype(jnp.bfloat16)
    w_g2 = jnp.zeros((GLA_GATE_PAD, 2 * GLA_QK), jnp.float32)
    w_g2 = w_g2.at[:rank, :GLA_QK].set(w_gate[0]).at[rank:2 * rank, GLA_QK:].set(w_gate[1]).astype(jnp.bfloat16)
    const = lambda *shape: pl.BlockSpec(shape, lambda i: (0,) * len(shape))
    row = lambda n: pl.BlockSpec((rows, n), lambda i: (i, 0))
    f32 = jnp.float32
    return pl.pallas_call(
        _gla_project_kernel,
        grid=(t_total // rows,),
        in_specs=[row(d),
                  pl.BlockSpec((1, 6, d), lambda i: (jnp.minimum(i // blocks_per_sample, n_mod - 1), 0, 0)),
                  const(d, GLA_MAIN), const(d, GLA_GATE_PAD), const(GLA_GATE_PAD, 2 * GLA_QK),
                  const(1, 2 * GLA_QK)],
        out_specs=[row(GLA_QK), row(GLA_QK), row(GLA_V), row(GLA_V), row(2 * GLA_QK)],
        out_shape=[jax.ShapeDtypeStruct((t_total, GLA_QK), f32), jax.ShapeDtypeStruct((t_total, GLA_QK), f32),
                   jax.ShapeDtypeStruct((t_total, GLA_V), f32), jax.ShapeDtypeStruct((t_total, GLA_V), f32),
                   jax.ShapeDtypeStruct((t_total, 2 * GLA_QK), f32)],
        compiler_params=pltpu.CompilerParams(dimension_semantics=("arbitrary",),
                                             vmem_limit_bytes=48 * 1024 * 1024),
    )(x, mod, w_main, w_gin, w_g2, gate_bias.reshape(1, 2 * GLA_QK))


def _gla_scan_kernel(q_ref, k_ref, v_ref, g_ref, o_ref, state, *, backward):
    @pl.when(pl.program_id(1) == 0)
    def _():
        state[...] = jnp.zeros_like(state)

    n = GLA_CHUNK
    row = lax.broadcasted_iota(jnp.int32, (n, n), 0)
    col = lax.broadcasted_iota(jnp.int32, (n, n), 1)
    keep = (col >= row) if backward else (col <= row)
    tri = keep.astype(jnp.float32)
    bf = jnp.bfloat16
    for hd in range(GLA_HEADS):
        ck = slice(hd * GLA_DK, (hd + 1) * GLA_DK)
        cv = slice(hd * GLA_DV, (hd + 1) * GLA_DV)
        g = g_ref[:, ck]
        b = jnp.dot(tri, g, preferred_element_type=jnp.float32, precision=lax.Precision.HIGHEST)
        b_end = b[0:1, :] if backward else b[n - 1:n, :]
        q_in = (q_ref[:, ck] * jnp.exp(b)).astype(bf)
        kk = k_ref[:, ck]
        k_in = (kk * jnp.exp(-b)).astype(bf)
        k_end = (kk * jnp.exp(b_end - b)).astype(bf)
        vv = v_ref[:, cv].astype(bf)
        att = lax.dot_general(q_in, k_in, (((1,), (1,)), ((), ())), preferred_element_type=jnp.float32)
        att = jnp.where(keep, att, 0.0).astype(bf)
        s_old = state[hd]
        o = jnp.dot(att, vv, preferred_element_type=jnp.float32)
        o = o + lax.dot_general(q_in, s_old.astype(bf), (((1,), (1,)), ((), ())),
                                preferred_element_type=jnp.float32)
        o_ref[:, cv] = o
        vk = lax.dot_general(vv, k_end, (((0,), (0,)), ((), ())), preferred_element_type=jnp.float32)
        state[hd] = jnp.exp(b_end) * s_old + vk


def _gla_scan(q, k, v, g, *, bsz, lat_chunks, ctx_chunks, backward):
    t_total = q.shape[0]
    n = GLA_CHUNK
    steps = ctx_chunks + lat_chunks
    ctx_base = bsz * lat_chunks

    def block(b, j):
        if backward:
            c_ctx, c_lat = ctx_chunks - 1 - j, lat_chunks - 1 - (j - ctx_chunks)
        else:
            c_ctx, c_lat = j, j - ctx_chunks
        return jnp.where(j < ctx_chunks, ctx_base + b * ctx_chunks + c_ctx, b * lat_chunks + c_lat)

    gcol = 1 if backward else 0
    return pl.pallas_call(
        functools.partial(_gla_scan_kernel, backward=backward),
        grid=(bsz, steps),
        in_specs=[pl.BlockSpec((n, GLA_QK), lambda b, j: (block(b, j), 0)),
                  pl.BlockSpec((n, GLA_QK), lambda b, j: (block(b, j), 0)),
                  pl.BlockSpec((n, GLA_V), lambda b, j: (block(b, j), 0)),
                  pl.BlockSpec((n, GLA_QK), lambda b, j: (block(b, j), gcol))],
        out_specs=pl.BlockSpec((n, GLA_V), lambda b, j: (block(b, j), 0)),
        out_shape=jax.ShapeDtypeStruct((t_total, GLA_V), jnp.float32),
        scratch_shapes=[pltpu.VMEM((GLA_HEADS, GLA_DV, GLA_DK), jnp.float32)],
        compiler_params=pltpu.CompilerParams(dimension_semantics=("arbitrary", "arbitrary")),
    )(q, k, v, g)


def _gla_readout_kernel(x_ref, mod_ref, of_ref, ob_ref, r_ref, gn_ref, wout_ref, lng_ref, lnb_ref, out_ref, gated):
    x = x_ref[...]
    g1 = mod_ref[0, 2:3, :]
    o = of_ref[...] + ob_ref[...]
    r = r_ref[...]
    for hd in range(GLA_HEADS):
        cv = slice(hd * GLA_DV, (hd + 1) * GLA_DV)
        oh = o[:, cv]
        mu = jnp.mean(oh, axis=-1, keepdims=True)
        var = jnp.mean(jnp.square(oh - mu), axis=-1, keepdims=True)
        y = (oh - mu) * lax.rsqrt(var + LN_EPS) * gn_ref[:, cv]
        gated[:, cv] = (y * jax.nn.silu(r[:, cv])).astype(jnp.bfloat16)
    mix = jnp.dot(gated[...], wout_ref[...], preferred_element_type=jnp.float32)
    out_ref[...] = _residual_layer_norm(x, g1, mix, lng_ref[...], lnb_ref[...])


def _gla_readout(x, mod, o_f, o_b, r, gn_g, w_out, ln_g, ln_b, *, rows_per_sample):
    t_total, d = x.shape
    rows = MIXER_ROWS_PER_STEP
    n_mod = mod.shape[0]
    blocks_per_sample = rows_per_sample // rows
    const = lambda *shape: pl.BlockSpec(shape, lambda i: (0,) * len(shape))
    row = lambda n: pl.BlockSpec((rows, n), lambda i: (i, 0))
    return pl.pallas_call(
        _gla_readout_kernel,
        grid=(t_total // rows,),
        in_specs=[row(d),
                  pl.BlockSpec((1, 6, d), lambda i: (jnp.minimum(i // blocks_per_sample, n_mod - 1), 0, 0)),
                  row(GLA_V), row(GLA_V), row(GLA_V), const(1, GLA_V), const(GLA_V, d), const(1, d), const(1, d)],
        out_specs=row(d),
        out_shape=jax.ShapeDtypeStruct((t_total, d), jnp.float32),
        scratch_shapes=[pltpu.VMEM((rows, GLA_V), jnp.bfloat16)],
        compiler_params=pltpu.CompilerParams(dimension_semantics=("arbitrary",),
                                             vmem_limit_bytes=48 * 1024 * 1024),
    )(x, mod, o_f, o_b, r, gn_g.reshape(1, GLA_V), w_out.astype(jnp.bfloat16), ln_g.reshape(1, d), ln_b.reshape(1, d))


def _gla_mixer(x, mod, w_in, w_gate, gate_bias, gn_g, w_out, ln_g, ln_b, *, bsz, seq, ctx_len, n_out):
    q, k, v, r, g = _gla_project(x, mod, w_in, w_gate, gate_bias, rows_per_sample=seq)
    kw = dict(bsz=bsz, lat_chunks=seq // GLA_CHUNK, ctx_chunks=ctx_len // GLA_CHUNK)
    o_f = _gla_scan(q, k, v, g, backward=False, **kw)
    o_b = _gla_scan(q, k, v, g, backward=True, **kw)
    return _gla_readout(x[:n_out], mod, o_f[:n_out], o_b[:n_out], r[:n_out], gn_g, w_out, ln_g, ln_b,
                        rows_per_sample=seq)


def kernel(x, c, ctx, c_ctx, w_mod, b_mod, ln_g, ln_b, a_w_in, a_norm_g, a_norm_b, a_w_s, a_b_s, a_w_out,
           b_w_in, b_w_gate, b_gate_bias, b_gn_g, b_w_out, p_w_q, p_keys, p_u, p_v):
    bsz, seq, d = x.shape
    ctx_len = ctx.shape[1]
    n_lat = bsz * seq
    xc = jnp.concatenate([x.reshape(n_lat, d), ctx.reshape(bsz * ctx_len, d)], axis=0)
    sc = jax.nn.silu(jnp.concatenate([c, c_ctx[None, :]], axis=0))
    mods = _modulation(sc, w_mod, b_mod).reshape(DEPTH, bsz + 1, 6, d)
    for i in range(DEPTH):
        last = i == DEPTH - 1
        j = i // N_MIXERS
        m = mods[i]
        n_tok = n_lat if last else xc.shape[0]
        if i % N_MIXERS == 0:
            x1 = _chunk_mlp_mixer(xc[:n_tok], m, a_w_in[j], a_norm_g[j], a_norm_b[j], a_w_s[j], a_b_s[j],
                                  a_w_out[j], ln_g[i, 0], ln_b[i, 0], rows_per_sample=seq)
        else:
            x1 = _gla_mixer(xc, m, b_w_in[j], b_w_gate[j], b_gate_bias[j], b_gn_g[j], b_w_out[j],
                            ln_g[i, 0], ln_b[i, 0], bsz=bsz, seq=seq, ctx_len=ctx_len, n_out=n_tok)
        idx, wts = _peer_route_pallas(x1, m, p_w_q[i], p_keys[i], tokens_per_sample_block=seq // 128)
        tab = jnp.stack([p_u[i], p_v[i]], axis=1)
        xc = _peer_experts(x1, m, idx, wts, tab, ln_g[i, 1], ln_b[i, 1],
                           tokens_per_sample_block=seq // PEER_TOKENS_PER_STEP)
    return xc[:n_lat].reshape(bsz, seq, d)
```

```python
import functools

import jax
import jax.numpy as jnp
from jax import lax
from jax.experimental import pallas as pl
from jax.experimental.pallas import tpu as pltpu

D_MODEL = 1024
DEPTH = 4
N_MIXERS = 2
DN_ALPHA = (2.0 * DEPTH) ** 0.25
LN_EPS = 1e-5
CHUNK = 128
A_WIDTH = D_MODEL
A_HEADS = 8
A_GROUP = A_WIDTH // A_HEADS
GLA_HEADS = 4
GLA_DK = D_MODEL // 2 // GLA_HEADS
GLA_DV = D_MODEL // GLA_HEADS
GLA_QK = GLA_HEADS * GLA_DK
GLA_V = GLA_HEADS * GLA_DV
GLA_GATE_RANK = 16
GLA_TAU = 16.0
GLA_CHUNK = 64
PEER_HEADS = 8
PEER_NKEYS = 128
PEER_QDIM = 128
PEER_HALF = PEER_QDIM // 2
PEER_TOPK = 16
PEER_PAIRS = PEER_HEADS * PEER_TOPK

PEER_TOKENS_PER_STEP = 128
PEER_ROW_BUFFERS = 8


def _layer_norm(x, g, b):
    mu = jnp.mean(x, axis=-1, keepdims=True)
    var = jnp.mean(jnp.square(x - mu), axis=-1, keepdims=True)
    return (x - mu) * lax.rsqrt(var + LN_EPS) * g + b


SUBLANES = 8
LANES = 128


def _sublane_sums(p):
    sub = lax.broadcasted_iota(jnp.int32, (SUBLANES, LANES), 0)

    def merge(a, b, sh, members):
        mask = functools.reduce(jnp.logical_or, [sub == j for j in members])
        return jnp.where(mask, a, b) + pltpu.roll(jnp.where(mask, b, a), sh, axis=0)

    c_e = merge(p[0], p[4], 4, (0, 5, 6, 7))
    d_e = merge(p[2], p[6], 4, (0, 1, 2, 7))
    c_f = merge(p[1], p[5], 4, (0, 1, 6, 7))
    d_f = merge(p[3], p[7], 4, (0, 1, 2, 3))
    e = merge(c_e, d_e, 2, (0, 3, 4, 7))
    f = merge(c_f, d_f, 2, (0, 1, 4, 5))
    return merge(e, f, 1, (0, 2, 4, 6))


def _peer_expert_kernel(zero_ref, idx_ref, idx_next_ref, w_ref, x_ref, mod_ref, lng_ref, lnb_ref, tab_ref,
                        out_ref, *scratch, tb, nbuf):
    rows = scratch[:nbuf]
    h_scr, ffn_scr, act_scr, sems = scratch[nbuf:]
    step = pl.program_id(0)
    last = pl.num_programs(0) - 1
    npairs = rows[0].shape[1]
    ahead = nbuf - 2
    sl = SUBLANES

    def issue(src_idx_ref, t, slot):
        z = zero_ref[0]
        for k in range(npairs):
            e = src_idx_ref[t, k]
            pltpu.make_async_copy(tab_ref.at[e], rows[slot].at[z, k], sems.at[slot]).start(priority=k % 2)

    def wait(slot):
        pltpu.make_async_copy(tab_ref.at[pl.ds(0, npairs)], rows[slot].at[0], sems.at[slot]).wait()

    @pl.when(step == 0)
    def _():
        for t in range(ahead):
            issue(idx_ref, t, t)

    s2 = mod_ref[0, 4]
    sh2 = mod_ref[0, 3]
    g2 = mod_ref[0, 5]
    h_scr[...] = x_ref[...] * (1.0 + s2) + sh2

    def token(t, slot, w_col):
        r = rows[slot].at[0]
        h = h_scr[t]
        sums = []
        for g in range(npairs // sl):
            sums.append(_sublane_sums([r[g * sl + j, 0:sl, :] * h for j in range(sl)]))
        s = jnp.sum(jnp.concatenate(sums, axis=0), axis=-1, keepdims=True)
        act_scr[...] = jnp.broadcast_to(jax.nn.gelu(s) * w_col, (npairs, LANES))
        accs = [jnp.zeros((sl, LANES), jnp.float32) for _ in range(4)]
        for k in range(npairs):
            a_k = jnp.broadcast_to(act_scr[pl.ds(k, 1), :], (sl, LANES))
            accs[k % 4] = accs[k % 4] + a_k * r[k, sl:2 * sl, :]
        ffn_scr[t] = (accs[0] + accs[1]) + (accs[2] + accs[3])

    n_groups = tb // nbuf

    def group(j8, is_last):
        w_t = w_ref[pl.ds(pl.multiple_of(j8 * nbuf, nbuf), nbuf), :].T
        for u in range(nbuf):
            t = j8 * nbuf + u
            wait(u)
            if not is_last or u + ahead < nbuf:
                issue(idx_ref, t + ahead, (u + ahead) % nbuf)
            else:
                issue(idx_next_ref, u + ahead - nbuf, (u + ahead) % nbuf)
            token(t, u, w_t[:, u:u + 1])

    def main_body(j8, carry):
        group(j8, False)
        return carry
    lax.fori_loop(0, n_groups - 1, main_body, 0)
    group(n_groups - 1, True)

    @pl.when(step == last)
    def _():
        for t in range(ahead):
            wait(t)

    y = DN_ALPHA * x_ref[...] + g2 * ffn_scr[...]
    n = y.shape[1] * y.shape[2]
    mu = jnp.sum(jnp.sum(y, axis=2, keepdims=True), axis=1, keepdims=True) / n
    yc = y - mu
    var = jnp.sum(jnp.sum(yc * yc, axis=2, keepdims=True), axis=1, keepdims=True) / n
    out_ref[...] = yc * lax.rsqrt(var + LN_EPS) * lng_ref[...] + lnb_ref[...]


def _peer_experts(x, mod, idx, w, tab, ln_g, ln_b, *, tokens_per_sample_block):
    t_total, d = x.shape
    npairs = idx.shape[1]
    tb = PEER_TOKENS_PER_STEP
    nbuf = PEER_ROW_BUFFERS
    sl = SUBLANES
    assert d == sl * LANES and nbuf == sl and t_total % tb == 0 and tb % nbuf == 0 and npairs % sl == 0
    n_steps = t_total // tb
    n_mod = mod.shape[0]
    kern = functools.partial(_peer_expert_kernel, tb=tb, nbuf=nbuf)
    out = pl.pallas_call(
        kern,
        grid=(n_steps,),
        in_specs=[
            pl.BlockSpec(memory_space=pltpu.SMEM),
            pl.BlockSpec((tb, npairs), lambda i: (i, 0), memory_space=pltpu.SMEM),
            pl.BlockSpec((tb, npairs), lambda i: (jnp.minimum(i + 1, n_steps - 1), 0), memory_space=pltpu.SMEM),
            pl.BlockSpec((tb, npairs), lambda i: (i, 0)),
            pl.BlockSpec((tb, sl, LANES), lambda i: (i, 0, 0)),
            pl.BlockSpec((1, 6, sl, LANES),
                         lambda i: (jnp.minimum(i // tokens_per_sample_block, n_mod - 1), 0, 0, 0)),
            pl.BlockSpec((sl, LANES), lambda i: (0, 0)),
            pl.BlockSpec((sl, LANES), lambda i: (0, 0)),
            pl.BlockSpec(memory_space=pl.ANY),
        ],
        out_specs=pl.BlockSpec((tb, sl, LANES), lambda i: (i, 0, 0)),
        out_shape=jax.ShapeDtypeStruct((t_total, sl, LANES), jnp.float32),
        scratch_shapes=[pltpu.VMEM((1, npairs, 2 * sl, LANES), jnp.float32) for _ in range(nbuf)] + [
            pltpu.VMEM((tb, sl, LANES), jnp.float32),
            pltpu.VMEM((tb, sl, LANES), jnp.float32),
            pltpu.VMEM((npairs, LANES), jnp.float32),
            pltpu.SemaphoreType.DMA((nbuf,)),
        ],
        compiler_params=pltpu.CompilerParams(dimension_semantics=("arbitrary",),
                                             vmem_limit_bytes=40 * 1024 * 1024),
    )(jnp.zeros((1,), jnp.int32), idx, idx, w, x.reshape(t_total, sl, LANES), mod.reshape(n_mod, 6, sl, LANES),
      ln_g.reshape(sl, LANES), ln_b.reshape(sl, LANES), tab.reshape(tab.shape[0], 2 * sl, LANES))
    return out.reshape(t_total, d)


_NEG_INF = float("-inf")


def _topk_rows(s, k):
    n, width = s.shape
    rows = lax.broadcasted_iota(jnp.int32, s.shape, 0)
    out_rows = lax.broadcasted_iota(jnp.int32, (k, width), 0)
    vals = jnp.zeros((k, width), jnp.float32)
    idxs = jnp.zeros((k, width), jnp.int32)
    for r in range(k):
        m = jnp.max(s, axis=0, keepdims=True)
        first = jnp.min(jnp.where(s == m, rows, n), axis=0, keepdims=True)
        vals = jnp.where(out_rows == r, m, vals)
        idxs = jnp.where(out_rows == r, first, idxs)
        s = jnp.where(rows == first, _NEG_INF, s)
    return vals, idxs


def _pair_candidates(s1, s2):
    k = PEER_TOPK
    sub = lax.broadcasted_iota(jnp.int32, (8, 1), 0)
    groups = []

    def add(vals, a, b, keep):
        ok = keep & ((a + 1) * (b + 1) <= k)
        groups.append((jnp.where(ok, vals, _NEG_INF), a * k + b))

    for b0 in (0, 8):
        add(s1[0:1, :] + s2[b0:b0 + 8, :], 0 * sub, b0 + sub, sub >= 0)
    add(s1[8:16, :] + s2[0:1, :], 8 + sub, 0 * sub, sub >= 0)
    for b in range(5):
        add(s1[0:8, :] + s2[b:b + 1, :], sub, b + 0 * sub, sub >= 1)
    add(s1[1:2, :] + s2[0:8, :], 1 + 0 * sub, sub, sub >= 5)
    return groups


def _topk_groups(groups, k):
    width = groups[0][0].shape[1]
    big = PEER_TOPK * PEER_TOPK
    out_rows = lax.broadcasted_iota(jnp.int32, (k, width), 0)
    vals = jnp.zeros((k, width), jnp.float32)
    poss = jnp.zeros((k, width), jnp.int32)
    gv = [g[0] for g in groups]
    gp = [g[1] for g in groups]
    for r in range(k):
        m = functools.reduce(jnp.maximum, gv)
        m = jnp.max(m, axis=0, keepdims=True)
        f = functools.reduce(jnp.minimum, [jnp.where(v == m, p, big) for v, p in zip(gv, gp)])
        f = jnp.min(f, axis=0, keepdims=True)
        vals = jnp.where(out_rows == r, m, vals)
        poss = jnp.where(out_rows == r, f, poss)
        gv = [jnp.where(p == f, _NEG_INF, v) for v, p in zip(gv, gp)]
    return vals, poss


def _select_rows(table, sel):
    out = jnp.zeros_like(table)
    for a in range(table.shape[0]):
        out = jnp.where(sel == a, table[a:a + 1, :], out)
    return out


def _peer_route_kernel(x_ref, mod_ref, wq_ref, keys_ref, idx_ref, w_ref, idx_t, w_t):
    k = PEER_TOPK
    s2 = mod_ref[0, 4:5, :]
    sh2 = mod_ref[0, 3:4, :]
    h = (x_ref[...] * (1.0 + s2) + sh2).astype(jnp.bfloat16)

    def head(hd, carry):
        q = jnp.dot(h, wq_ref[hd], preferred_element_type=jnp.float32).astype(jnp.bfloat16)
        nt = (((1,), (1,)), ((), ()))
        sc1 = lax.dot_general(keys_ref[0], q, nt, preferred_element_type=jnp.float32)
        sc2 = lax.dot_general(keys_ref[1], q, nt, preferred_element_type=jnp.float32)
        v1, i1 = _topk_rows(sc1, k)
        v2, i2 = _topk_rows(sc2, k)
        top_s, pos = _topk_groups(_pair_candidates(v1, v2), k)
        e = jnp.exp(top_s - top_s[0:1, :])
        wts = e / jnp.sum(e, axis=0, keepdims=True)
        a = lax.shift_right_logical(pos, 4)
        b = lax.bitwise_and(pos, k - 1)
        ids = _select_rows(i1, a) * PEER_NKEYS + _select_rows(i2, b)
        row0 = pl.multiple_of(hd * k, k)
        idx_t[pl.ds(row0, k), :] = ids
        w_t[pl.ds(row0, k), :] = wts
        return carry

    lax.fori_loop(0, PEER_HEADS, head, 0, unroll=4)
    idx_ref[...] = idx_t[...].T
    w_ref[...] = w_t[...].T


def _peer_route_pallas(x, mod, w_q, sub_keys, *, tokens_per_sample_block):
    t_total, d = x.shape
    tb = 128
    n_steps = t_total // tb
    n_mod = mod.shape[0]
    wq = w_q.reshape(d, PEER_HEADS, PEER_QDIM).transpose(1, 0, 2).astype(jnp.bfloat16)
    zeros = jnp.zeros((PEER_NKEYS, PEER_HALF), sub_keys.dtype)
    keys = jnp.stack([jnp.concatenate([sub_keys[0], zeros], axis=1),
                      jnp.concatenate([zeros, sub_keys[1]], axis=1)]).astype(jnp.bfloat16)
    return pl.pallas_call(
        _peer_route_kernel,
        grid=(n_steps,),
        in_specs=[
            pl.BlockSpec((tb, d), lambda i: (i, 0)),
            pl.BlockSpec((1, 6, d), lambda i: (jnp.minimum(i // tokens_per_sample_block, n_mod - 1), 0, 0)),
            pl.BlockSpec((PEER_HEADS, d, PEER_QDIM), lambda i: (0, 0, 0)),
            pl.BlockSpec((2, PEER_NKEYS, PEER_QDIM), lambda i: (0, 0, 0)),
        ],
        out_specs=[pl.BlockSpec((tb, PEER_PAIRS), lambda i: (i, 0)),
                   pl.BlockSpec((tb, PEER_PAIRS), lambda i: (i, 0))],
        out_shape=[jax.ShapeDtypeStruct((t_total, PEER_PAIRS), jnp.int32),
                   jax.ShapeDtypeStruct((t_total, PEER_PAIRS), jnp.float32)],
        scratch_shapes=[pltpu.VMEM((PEER_PAIRS, tb), jnp.int32), pltpu.VMEM((PEER_PAIRS, tb), jnp.float32)],
        compiler_params=pltpu.CompilerParams(dimension_semantics=("arbitrary",)),
    )(x, mod, wq, keys)


MOD_COLS_PER_STEP = 1536


def _modulation_kernel(sc_ref, w_ref, b_ref, out_ref):
    acc = jnp.dot(sc_ref[...].astype(jnp.bfloat16), w_ref[0].astype(jnp.bfloat16),
                  preferred_element_type=jnp.float32)
    out_ref[0] = acc + b_ref[0]


def _modulation(sc, w_mod, b_mod):
    n_layers, d, n_out = w_mod.shape
    s = sc.shape[0]
    tn = MOD_COLS_PER_STEP
    return pl.pallas_call(
        _modulation_kernel,
        grid=(n_layers, n_out // tn),
        in_specs=[pl.BlockSpec((s, d), lambda l, j: (0, 0)),
                  pl.BlockSpec((1, d, tn), lambda l, j: (l, 0, j)),
                  pl.BlockSpec((1, 1, tn), lambda l, j: (l, 0, j))],
        out_specs=pl.BlockSpec((1, s, tn), lambda l, j: (l, 0, j)),
        out_shape=jax.ShapeDtypeStruct((n_layers, s, n_out), jnp.float32),
        compiler_params=pltpu.CompilerParams(dimension_semantics=("arbitrary", "arbitrary"),
                                             vmem_limit_bytes=40 * 1024 * 1024),
    )(sc, w_mod, b_mod.reshape(n_layers, 1, n_out))


def _residual_layer_norm(x, g_mod, mix, ln_g, ln_b):
    y = DN_ALPHA * x + g_mod * mix
    return _layer_norm(y, ln_g, ln_b)


MIXER_ROWS_PER_STEP = 256


def _chunk_mlp_kernel(x_ref, mod_ref, win_ref, ng_ref, nb_ref, ws_ref, bs_ref, wout_ref, lng_ref, lnb_ref,
                      out_ref, gated):
    x = x_ref[...]
    sh1 = mod_ref[0, 0:1, :]
    s1 = mod_ref[0, 1:2, :]
    g1 = mod_ref[0, 2:3, :]
    h = (x * (1.0 + s1) + sh1).astype(jnp.bfloat16)
    z = jax.nn.gelu(jnp.dot(h, win_ref[...], preferred_element_type=jnp.float32))
    u = z[:, :A_WIDTH]
    v = _layer_norm(z[:, A_WIDTH:], ng_ref[...], nb_ref[...]).astype(jnp.bfloat16)
    for c in range(x.shape[0] // CHUNK):
        r0 = c * CHUNK
        for hd in range(A_HEADS):
            c0 = hd * A_GROUP
            s = jnp.dot(ws_ref[hd], v[r0:r0 + CHUNK, c0:c0 + A_GROUP], preferred_element_type=jnp.float32)
            s = s + bs_ref[:, hd:hd + 1]
            gated[r0:r0 + CHUNK, c0:c0 + A_GROUP] = (u[r0:r0 + CHUNK, c0:c0 + A_GROUP] * s).astype(jnp.bfloat16)
    mix = jnp.dot(gated[...], wout_ref[...], preferred_element_type=jnp.float32)
    out_ref[...] = _residual_layer_norm(x, g1, mix, lng_ref[...], lnb_ref[...])


def _chunk_mlp_mixer(x, mod, w_in, norm_g, norm_b, w_s, b_s, w_out, ln_g, ln_b, *, rows_per_sample):
    t_total, d = x.shape
    rows = MIXER_ROWS_PER_STEP
    n_mod = mod.shape[0]
    blocks_per_sample = rows_per_sample // rows
    const = lambda *shape: pl.BlockSpec(shape, lambda i: (0,) * len(shape))
    return pl.pallas_call(
        _chunk_mlp_kernel,
        grid=(t_total // rows,),
        in_specs=[pl.BlockSpec((rows, d), lambda i: (i, 0)),
                  pl.BlockSpec((1, 6, d), lambda i: (jnp.minimum(i // blocks_per_sample, n_mod - 1), 0, 0)),
                  const(d, 2 * A_WIDTH), const(1, A_WIDTH), const(1, A_WIDTH),
                  const(A_HEADS, CHUNK, CHUNK), const(CHUNK, A_HEADS), const(A_WIDTH, d),
                  const(1, d), const(1, d)],
        out_specs=pl.BlockSpec((rows, d), lambda i: (i, 0)),
        out_shape=jax.ShapeDtypeStruct((t_total, d), jnp.float32),
        scratch_shapes=[pltpu.VMEM((rows, A_WIDTH), jnp.bfloat16)],
        compiler_params=pltpu.CompilerParams(dimension_semantics=("arbitrary",),
                                             vmem_limit_bytes=48 * 1024 * 1024),
    )(x, mod, w_in.astype(jnp.bfloat16), norm_g.reshape(1, -1), norm_b.reshape(1, -1),
      w_s.astype(jnp.bfloat16), b_s.T, w_out.astype(jnp.bfloat16), ln_g.reshape(1, d), ln_b.reshape(1, d))


GLA_MAIN = 2 * GLA_QK + 2 * GLA_V
GLA_GATE_PAD = 128


def _gla_project_kernel(x_ref, mod_ref, win_ref, wgin_ref, wg_ref, gb_ref, q_ref, k_ref, v_ref, r_ref, g_ref):
    sh1 = mod_ref[0, 0:1, :]
    s1 = mod_ref[0, 1:2, :]
    h = (x_ref[...] * (1.0 + s1) + sh1).astype(jnp.bfloat16)
    z = jnp.dot(h, win_ref[...], preferred_element_type=jnp.float32)
    q_ref[...] = z[:, :GLA_QK] * (GLA_DK ** -0.5)
    k_ref[...] = z[:, GLA_QK:2 * GLA_QK]
    v_ref[...] = z[:, 2 * GLA_QK:2 * GLA_QK + GLA_V]
    r_ref[...] = z[:, 2 * GLA_QK + GLA_V:]
    gl = jnp.dot(h, wgin_ref[...], preferred_element_type=jnp.float32).astype(jnp.bfloat16)
    zg = jnp.dot(gl, wg_ref[...], preferred_element_type=jnp.float32) + gb_ref[...]
    g_ref[...] = jax.nn.log_sigmoid(zg) / GLA_TAU


def _gla_project(x, mod, w_in, w_gate, gate_bias, *, rows_per_sample):
    t_total, d = x.shape
    rows = MIXER_ROWS_PER_STEP
    n_mod = mod.shape[0]
    blocks_per_sample = rows_per_sample // rows
    rank = GLA_GATE_RANK
    w_main = w_in[:, :GLA_MAIN].astype(jnp.bfloat16)
    w_gin = jnp.pad(w_in[:, GLA_MAIN:], ((0, 0), (0, GLA_GATE_PAD - 2 * rank))).astype(jnp.bfloat16)
    w_g2 = jnp.zeros((GLA_GATE_PAD, 2 * GLA_QK), jnp.float32)
    w_g2 = w_g2.at[:rank, :GLA_QK].set(w_gate[0]).at[rank:2 * rank, GLA_QK:].set(w_gate[1]).astype(jnp.bfloat16)
    const = lambda *shape: pl.BlockSpec(shape, lambda i: (0,) * len(shape))
    row = lambda n: pl.BlockSpec((rows, n), lambda i: (i, 0))
    f32 = jnp.float32
    return pl.pallas_call(
        _gla_project_kernel,
        grid=(t_total // rows,),
        in_specs=[row(d),
                  pl.BlockSpec((1, 6, d), lambda i: (jnp.minimum(i // blocks_per_sample, n_mod - 1), 0, 0)),
                  const(d, GLA_MAIN), const(d, GLA_GATE_PAD), const(GLA_GATE_PAD, 2 * GLA_QK),
                  const(1, 2 * GLA_QK)],
        out_specs=[row(GLA_QK), row(GLA_QK), row(GLA_V), row(GLA_V), row(2 * GLA_QK)],
        out_shape=[jax.ShapeDtypeStruct((t_total, GLA_QK), f32), jax.ShapeDtypeStruct((t_total, GLA_QK), f32),
                   jax.ShapeDtypeStruct((t_total, GLA_V), f32), jax.ShapeDtypeStruct((t_total, GLA_V), f32),
                   jax.ShapeDtypeStruct((t_total, 2 * GLA_QK), f32)],
        compiler_params=pltpu.CompilerParams(dimension_semantics=("arbitrary",),
                                             vmem_limit_bytes=48 * 1024 * 1024),
    )(x, mod, w_main, w_gin, w_g2, gate_bias.reshape(1, 2 * GLA_QK))


def _gla_scan_kernel(q_ref, k_ref, v_ref, g_ref, o_ref, state, *, backward):
    @pl.when(pl.program_id(1) == 0)
    def _():
        state[...] = jnp.zeros_like(state)

    n = GLA_CHUNK
    row = lax.broadcasted_iota(jnp.int32, (n, n), 0)
    col = lax.broadcasted_iota(jnp.int32, (n, n), 1)
    keep = (col >= row) if backward else (col <= row)
    tri = keep.astype(jnp.float32)
    bf = jnp.bfloat16
    for hd in range(GLA_HEADS):
        ck = slice(hd * GLA_DK, (hd + 1) * GLA_DK)
        cv = slice(hd * GLA_DV, (hd + 1) * GLA_DV)
        g = g_ref[:, ck]
        b = jnp.dot(tri, g, preferred_element_type=jnp.float32, precision=lax.Precision.HIGHEST)
        b_end = b[0:1, :] if backward else b[n - 1:n, :]
        q_in = (q_ref[:, ck] * jnp.exp(b)).astype(bf)
        kk = k_ref[:, ck]
        k_in = (kk * jnp.exp(-b)).astype(bf)
        k_end = (kk * jnp.exp(b_end - b)).astype(bf)
        vv = v_ref[:, cv].astype(bf)
        att = lax.dot_general(q_in, k_in, (((1,), (1,)), ((), ())), preferred_element_type=jnp.float32)
        att = jnp.where(keep, att, 0.0).astype(bf)
        s_old = state[hd]
        o = jnp.dot(att, vv, preferred_element_type=jnp.float32)
        o = o + lax.dot_general(q_in, s_old.astype(bf), (((1,), (1,)), ((), ())),
                                preferred_element_type=jnp.float32)
        o_ref[:, cv] = o
        vk = lax.dot_general(vv, k_end, (((0,), (0,)), ((), ())), preferred_element_type=jnp.float32)
        state[hd] = jnp.exp(b_end) * s_old + vk


def _gla_scan(q, k, v, g, *, bsz, lat_chunks, ctx_chunks, backward):
    t_total = q.shape[0]
    n = GLA_CHUNK
    steps = ctx_chunks + lat_chunks
    ctx_base = bsz * lat_chunks

    def block(b, j):
        if backward:
            c_ctx, c_lat = ctx_chunks - 1 - j, lat_chunks - 1 - (j - ctx_chunks)
        else:
            c_ctx, c_lat = j, j - ctx_chunks
        return jnp.where(j < ctx_chunks, ctx_base + b * ctx_chunks + c_ctx, b * lat_chunks + c_lat)

    gcol = 1 if backward else 0
    return pl.pallas_call(
        functools.partial(_gla_scan_kernel, backward=backward),
        grid=(bsz, steps),
        in_specs=[pl.BlockSpec((n, GLA_QK), lambda b, j: (block(b, j), 0)),
                  pl.BlockSpec((n, GLA_QK), lambda b, j: (block(b, j), 0)),
                  pl.BlockSpec((n, GLA_V), lambda b, j: (block(b, j), 0)),
                  pl.BlockSpec((n, GLA_QK), lambda b, j: (block(b, j), gcol))],
        out_specs=pl.BlockSpec((n, GLA_V), lambda b, j: (block(b, j), 0)),
        out_shape=jax.ShapeDtypeStruct((t_total, GLA_V), jnp.float32),
        scratch_shapes=[pltpu.VMEM((GLA_HEADS, GLA_DV, GLA_DK), jnp.float32)],
        compiler_params=pltpu.CompilerParams(dimension_semantics=("arbitrary", "arbitrary")),
    )(q, k, v, g)


def _gla_readout_kernel(x_ref, mod_ref, of_ref, ob_ref, r_ref, gn_ref, wout_ref, lng_ref, lnb_ref, out_ref, gated):
    x = x_ref[...]
    g1 = mod_ref[0, 2:3, :]
    o = of_ref[...] + ob_ref[...]
    r = r_ref[...]
    for hd in range(GLA_HEADS):
        cv = slice(hd * GLA_DV, (hd + 1) * GLA_DV)
        oh = o[:, cv]
        mu = jnp.mean(oh, axis=-1, keepdims=True)
        var = jnp.mean(jnp.square(oh - mu), axis=-1, keepdims=True)
        y = (oh - mu) * lax.rsqrt(var + LN_EPS) * gn_ref[:, cv]
        gated[:, cv] = (y * jax.nn.silu(r[:, cv])).astype(jnp.bfloat16)
    mix = jnp.dot(gated[...], wout_ref[...], preferred_element_type=jnp.float32)
    out_ref[...] = _residual_layer_norm(x, g1, mix, lng_ref[...], lnb_ref[...])


def _gla_readout(x, mod, o_f, o_b, r, gn_g, w_out, ln_g, ln_b, *, rows_per_sample):
    t_total, d = x.shape
    rows = MIXER_ROWS_PER_STEP
    n_mod = mod.shape[0]
    blocks_per_sample = rows_per_sample // rows
    const = lambda *shape: pl.BlockSpec(shape, lambda i: (0,) * len(shape))
    row = lambda n: pl.BlockSpec((rows, n), lambda i: (i, 0))
    return pl.pallas_call(
        _gla_readout_kernel,
        grid=(t_total // rows,),
        in_specs=[row(d),
                  pl.BlockSpec((1, 6, d), lambda i: (jnp.minimum(i // blocks_per_sample, n_mod - 1), 0, 0)),
                  row(GLA_V), row(GLA_V), row(GLA_V), const(1, GLA_V), const(GLA_V, d), const(1, d), const(1, d)],
        out_specs=row(d),
        out_shape=jax.ShapeDtypeStruct((t_total, d), jnp.float32),
        scratch_shapes=[pltpu.VMEM((rows, GLA_V), jnp.bfloat16)],
        compiler_params=pltpu.CompilerParams(dimension_semantics=("arbitrary",),
                                             vmem_limit_bytes=48 * 1024 * 1024),
    )(x, mod, o_f, o_b, r, gn_g.reshape(1, GLA_V), w_out.astype(jnp.bfloat16), ln_g.reshape(1, d), ln_b.reshape(1, d))


def _gla_mixer(x, mod, w_in, w_gate, gate_bias, gn_g, w_out, ln_g, ln_b, *, bsz, seq, ctx_len, n_out):
    q, k, v, r, g = _gla_project(x, mod, w_in, w_gate, gate_bias, rows_per_sample=seq)
    kw = dict(bsz=bsz, lat_chunks=seq // GLA_CHUNK, ctx_chunks=ctx_len // GLA_CHUNK)
    o_f = _gla_scan(q, k, v, g, backward=False, **kw)
    o_b = _gla_scan(q, k, v, g, backward=True, **kw)
    return _gla_readout(x[:n_out], mod, o_f[:n_out], o_b[:n_out], r[:n_out], gn_g, w_out, ln_g, ln_b,
                        rows_per_sample=seq)


def kernel(x, c, ctx, c_ctx, w_mod, b_mod, ln_g, ln_b, a_w_in, a_norm_g, a_norm_b, a_w_s, a_b_s, a_w_out,
           b_w_in, b_w_gate, b_gate_bias, b_gn_g, b_w_out, p_w_q, p_keys, p_u, p_v):
    bsz, seq, d = x.shape
    ctx_len = ctx.shape[1]
    n_lat = bsz * seq
    xc = jnp.concatenate([x.reshape(n_lat, d), ctx.reshape(bsz * ctx_len, d)], axis=0)
    sc = jax.nn.silu(jnp.concatenate([c, c_ctx[None, :]], axis=0))
    mods = _modulation(sc, w_mod, b_mod).reshape(DEPTH, bsz + 1, 6, d)
    for i in range(DEPTH):
        last = i == DEPTH - 1
        j = i // N_MIXERS
        m = mods[i]
        n_tok = n_lat if last else xc.shape[0]
        if i % N_MIXERS == 0:
            x1 = _chunk_mlp_mixer(xc[:n_tok], m, a_w_in[j], a_norm_g[j], a_norm_b[j], a_w_s[j], a_b_s[j],
                                  a_w_out[j], ln_g[i, 0], ln_b[i, 0], rows_per_sample=seq)
        else:
            x1 = _gla_mixer(xc, m, b_w_in[j], b_w_gate[j], b_gate_bias[j], b_gn_g[j], b_w_out[j],
                            ln_g[i, 0], ln_b[i, 0], bsz=bsz, seq=seq, ctx_len=ctx_len, n_out=n_tok)
        idx, wts = _peer_route_pallas(x1, m, p_w_q[i], p_keys[i], tokens_per_sample_block=seq // 128)
        tab = jnp.stack([p_u[i], p_v[i]], axis=1)
        xc = _peer_experts(x1, m, idx, wts, tab, ln_g[i, 1], ln_b[i, 1],
                           tokens_per_sample_block=seq // PEER_TOKENS_PER_STEP)
    return xc[:n_lat].reshape(bsz, seq, d)
```

```python
import functools

import jax
import jax.numpy as jnp
from jax import lax
from jax.experimental import pallas as pl
from jax.experimental.pallas import tpu as pltpu

D_MODEL = 1024
DEPTH = 4
N_MIXERS = 2
DN_ALPHA = (2.0 * DEPTH) ** 0.25
LN_EPS = 1e-5
CHUNK = 128
A_WIDTH = D_MODEL
A_HEADS = 8
A_GROUP = A_WIDTH // A_HEADS
GLA_HEADS = 4
GLA_DK = D_MODEL // 2 // GLA_HEADS
GLA_DV = D_MODEL // GLA_HEADS
GLA_QK = GLA_HEADS * GLA_DK
GLA_V = GLA_HEADS * GLA_DV
GLA_GATE_RANK = 16
GLA_TAU = 16.0
GLA_CHUNK = 64
PEER_HEADS = 8
PEER_NKEYS = 128
PEER_QDIM = 128
PEER_HALF = PEER_QDIM // 2
PEER_TOPK = 16
PEER_PAIRS = PEER_HEADS * PEER_TOPK

PEER_TOKENS_PER_STEP = 128
PEER_ROW_BUFFERS = 8


def _layer_norm(x, g, b):
    mu = jnp.mean(x, axis=-1, keepdims=True)
    var = jnp.mean(jnp.square(x - mu), axis=-1, keepdims=True)
    return (x - mu) * lax.rsqrt(var + LN_EPS) * g + b


SUBLANES = 8
LANES = 128


def _sublane_sums(p):
    sub = lax.broadcasted_iota(jnp.int32, (SUBLANES, LANES), 0)

    def merge(a, b, sh, members):
        mask = functools.reduce(jnp.logical_or, [sub == j for j in members])
        return jnp.where(mask, a, b) + pltpu.roll(jnp.where(mask, b, a), sh, axis=0)

    c_e = merge(p[0], p[4], 4, (0, 5, 6, 7))
    d_e = merge(p[2], p[6], 4, (0, 1, 2, 7))
    c_f = merge(p[1], p[5], 4, (0, 1, 6, 7))
    d_f = merge(p[3], p[7], 4, (0, 1, 2, 3))
    e = merge(c_e, d_e, 2, (0, 3, 4, 7))
    f = merge(c_f, d_f, 2, (0, 1, 4, 5))
    return merge(e, f, 1, (0, 2, 4, 6))


def _peer_expert_kernel(zero_ref, idx_ref, idx_next_ref, w_ref, x_ref, mod_ref, lng_ref, lnb_ref, tab_ref,
                        out_ref, *scratch, tb, nbuf):
    rows = scratch[:nbuf]
    h_scr, ffn_scr, act_scr, sems = scratch[nbuf:]
    step = pl.program_id(0)
    last = pl.num_programs(0) - 1
    npairs = rows[0].shape[1]
    ahead = nbuf - 2
    sl = SUBLANES

    def issue(src_idx_ref, t, slot):
        z = zero_ref[0]
        for k in range(npairs):
            e = src_idx_ref[t, k]
            pltpu.make_async_copy(tab_ref.at[e], rows[slot].at[z, k], sems.at[slot]).start(priority=k % 2)

    def wait(slot):
        pltpu.make_async_copy(tab_ref.at[pl.ds(0, npairs)], rows[slot].at[0], sems.at[slot]).wait()

    @pl.when(step == 0)
    def _():
        for t in range(ahead):
            issue(idx_ref, t, t)

    s2 = mod_ref[0, 4]
    sh2 = mod_ref[0, 3]
    g2 = mod_ref[0, 5]
    h_scr[...] = x_ref[...].reshape(tb, sl, LANES) * (1.0 + s2) + sh2

    def token(t, slot, w_col):
        r = rows[slot].at[0]
        h = h_scr[t]
        sums = []
        for g in range(npairs // sl):
            sums.append(_sublane_sums([r[g * sl + j, 0:sl, :] * h for j in range(sl)]))
        s = jnp.sum(jnp.concatenate(sums, axis=0), axis=-1, keepdims=True)
        act_scr[...] = jnp.broadcast_to(jax.nn.gelu(s) * w_col, (npairs, LANES))
        accs = [jnp.zeros((sl, LANES), jnp.float32) for _ in range(4)]
        for k in range(npairs):
            a_k = jnp.broadcast_to(act_scr[pl.ds(k, 1), :], (sl, LANES))
            accs[k % 4] = accs[k % 4] + a_k * r[k, sl:2 * sl, :]
        ffn_scr[t] = (accs[0] + accs[1]) + (accs[2] + accs[3])

    n_groups = tb // nbuf

    def group(j8, is_last):
        w_t = w_ref[pl.ds(pl.multiple_of(j8 * nbuf, nbuf), nbuf), :].T
        for u in range(nbuf):
            t = j8 * nbuf + u
            wait(u)
            if not is_last or u + ahead < nbuf:
                issue(idx_ref, t + ahead, (u + ahead) % nbuf)
            else:
                issue(idx_next_ref, u + ahead - nbuf, (u + ahead) % nbuf)
            token(t, u, w_t[:, u:u + 1])

    def main_body(j8, carry):
        group(j8, False)
        return carry
    lax.fori_loop(0, n_groups - 1, main_body, 0)
    group(n_groups - 1, True)

    @pl.when(step == last)
    def _():
        for t in range(ahead):
            wait(t)

    ffn = (g2 * ffn_scr[...]).reshape(tb, sl * LANES)
    out_ref[...] = _layer_norm(DN_ALPHA * x_ref[...] + ffn, lng_ref[...], lnb_ref[...])


def _peer_experts(x, mod, idx, w, tab, ln_g, ln_b, *, tokens_per_sample_block):
    t_total, d = x.shape
    npairs = idx.shape[1]
    tb = PEER_TOKENS_PER_STEP
    nbuf = PEER_ROW_BUFFERS
    sl = SUBLANES
    assert d == sl * LANES and nbuf == sl and t_total % tb == 0 and tb % nbuf == 0 and npairs % sl == 0
    n_steps = t_total // tb
    n_mod = mod.shape[0]
    kern = functools.partial(_peer_expert_kernel, tb=tb, nbuf=nbuf)
    return pl.pallas_call(
        kern,
        grid=(n_steps,),
        in_specs=[
            pl.BlockSpec(memory_space=pltpu.SMEM),
            pl.BlockSpec((tb, npairs), lambda i: (i, 0), memory_space=pltpu.SMEM),
            pl.BlockSpec((tb, npairs), lambda i: (jnp.minimum(i + 1, n_steps - 1), 0), memory_space=pltpu.SMEM),
            pl.BlockSpec((tb, npairs), lambda i: (i, 0)),
            pl.BlockSpec((tb, d), lambda i: (i, 0)),
            pl.BlockSpec((1, 6, sl, LANES),
                         lambda i: (jnp.minimum(i // tokens_per_sample_block, n_mod - 1), 0, 0, 0)),
            pl.BlockSpec((1, d), lambda i: (0, 0)),
            pl.BlockSpec((1, d), lambda i: (0, 0)),
            pl.BlockSpec(memory_space=pl.ANY),
        ],
        out_specs=pl.BlockSpec((tb, d), lambda i: (i, 0)),
        out_shape=jax.ShapeDtypeStruct((t_total, d), jnp.float32),
        scratch_shapes=[pltpu.VMEM((1, npairs, 2 * sl, LANES), jnp.float32) for _ in range(nbuf)] + [
            pltpu.VMEM((tb, sl, LANES), jnp.float32),
            pltpu.VMEM((tb, sl, LANES), jnp.float32),
            pltpu.VMEM((npairs, LANES), jnp.float32),
            pltpu.SemaphoreType.DMA((nbuf,)),
        ],
        compiler_params=pltpu.CompilerParams(dimension_semantics=("arbitrary",),
                                             vmem_limit_bytes=40 * 1024 * 1024),
    )(jnp.zeros((1,), jnp.int32), idx, idx, w, x, mod.reshape(n_mod, 6, sl, LANES),
      ln_g.reshape(1, d), ln_b.reshape(1, d), tab.reshape(tab.shape[0], 2 * sl, LANES))


_NEG_INF = float("-inf")


def _topk_rows(s, k):
    n, width = s.shape
    rows = lax.broadcasted_iota(jnp.int32, s.shape, 0)
    out_rows = lax.broadcasted_iota(jnp.int32, (k, width), 0)
    vals = jnp.zeros((k, width), jnp.float32)
    idxs = jnp.zeros((k, width), jnp.int32)
    for r in range(k):
        m = jnp.max(s, axis=0, keepdims=True)
        first = jnp.min(jnp.where(s == m, rows, n), axis=0, keepdims=True)
        vals = jnp.where(out_rows == r, m, vals)
        idxs = jnp.where(out_rows == r, first, idxs)
        s = jnp.where(rows == first, _NEG_INF, s)
    return vals, idxs


def _pair_candidates(s1, s2):
    k = PEER_TOPK
    sub = lax.broadcasted_iota(jnp.int32, (8, 1), 0)
    groups = []

    def add(vals, a, b, keep):
        ok = keep & ((a + 1) * (b + 1) <= k)
        groups.append((jnp.where(ok, vals, _NEG_INF), a * k + b))

    for b0 in (0, 8):
        add(s1[0:1, :] + s2[b0:b0 + 8, :], 0 * sub, b0 + sub, sub >= 0)
    add(s1[8:16, :] + s2[0:1, :], 8 + sub, 0 * sub, sub >= 0)
    for b in range(5):
        add(s1[0:8, :] + s2[b:b + 1, :], sub, b + 0 * sub, sub >= 1)
    add(s1[1:2, :] + s2[0:8, :], 1 + 0 * sub, sub, sub >= 5)
    return groups


def _topk_groups(groups, k):
    width = groups[0][0].shape[1]
    big = PEER_TOPK * PEER_TOPK
    out_rows = lax.broadcasted_iota(jnp.int32, (k, width), 0)
    vals = jnp.zeros((k, width), jnp.float32)
    poss = jnp.zeros((k, width), jnp.int32)
    gv = [g[0] for g in groups]
    gp = [g[1] for g in groups]
    for r in range(k):
        m = functools.reduce(jnp.maximum, gv)
        m = jnp.max(m, axis=0, keepdims=True)
        f = functools.reduce(jnp.minimum, [jnp.where(v == m, p, big) for v, p in zip(gv, gp)])
        f = jnp.min(f, axis=0, keepdims=True)
        vals = jnp.where(out_rows == r, m, vals)
        poss = jnp.where(out_rows == r, f, poss)
        gv = [jnp.where(p == f, _NEG_INF, v) for v, p in zip(gv, gp)]
    return vals, poss


def _select_rows(table, sel):
    out = jnp.zeros_like(table)
    for a in range(table.shape[0]):
        out = jnp.where(sel == a, table[a:a + 1, :], out)
    return out


def _peer_route_kernel(x_ref, mod_ref, wq_ref, keys_ref, idx_ref, w_ref, idx_t, w_t):
    k = PEER_TOPK
    s2 = mod_ref[0, 4:5, :]
    sh2 = mod_ref[0, 3:4, :]
    h = (x_ref[...] * (1.0 + s2) + sh2).astype(jnp.bfloat16)

    def head(hd, carry):
        q = jnp.dot(h, wq_ref[hd], preferred_element_type=jnp.float32).astype(jnp.bfloat16)
        nt = (((1,), (1,)), ((), ()))
        sc1 = lax.dot_general(keys_ref[0], q, nt, preferred_element_type=jnp.float32)
        sc2 = lax.dot_general(keys_ref[1], q, nt, preferred_element_type=jnp.float32)
        v1, i1 = _topk_rows(sc1, k)
        v2, i2 = _topk_rows(sc2, k)
        top_s, pos = _topk_groups(_pair_candidates(v1, v2), k)
        e = jnp.exp(top_s - top_s[0:1, :])
        wts = e / jnp.sum(e, axis=0, keepdims=True)
        a = lax.shift_right_logical(pos, 4)
        b = lax.bitwise_and(pos, k - 1)
        ids = _select_rows(i1, a) * PEER_NKEYS + _select_rows(i2, b)
        row0 = pl.multiple_of(hd * k, k)
        idx_t[pl.ds(row0, k), :] = ids
        w_t[pl.ds(row0, k), :] = wts
        return carry

    lax.fori_loop(0, PEER_HEADS, head, 0, unroll=4)
    idx_ref[...] = idx_t[...].T
    w_ref[...] = w_t[...].T


def _peer_route_pallas(x, mod, w_q, sub_keys, *, tokens_per_sample_block):
    t_total, d = x.shape
    tb = 128
    n_steps = t_total // tb
    n_mod = mod.shape[0]
    wq = w_q.reshape(d, PEER_HEADS, PEER_QDIM).transpose(1, 0, 2).astype(jnp.bfloat16)
    zeros = jnp.zeros((PEER_NKEYS, PEER_HALF), sub_keys.dtype)
    keys = jnp.stack([jnp.concatenate([sub_keys[0], zeros], axis=1),
                      jnp.concatenate([zeros, sub_keys[1]], axis=1)]).astype(jnp.bfloat16)
    return pl.pallas_call(
        _peer_route_kernel,
        grid=(n_steps,),
        in_specs=[
            pl.BlockSpec((tb, d), lambda i: (i, 0)),
            pl.BlockSpec((1, 6, d), lambda i: (jnp.minimum(i // tokens_per_sample_block, n_mod - 1), 0, 0)),
            pl.BlockSpec((PEER_HEADS, d, PEER_QDIM), lambda i: (0, 0, 0)),
            pl.BlockSpec((2, PEER_NKEYS, PEER_QDIM), lambda i: (0, 0, 0)),
        ],
        out_specs=[pl.BlockSpec((tb, PEER_PAIRS), lambda i: (i, 0)),
                   pl.BlockSpec((tb, PEER_PAIRS), lambda i: (i, 0))],
        out_shape=[jax.ShapeDtypeStruct((t_total, PEER_PAIRS), jnp.int32),
                   jax.ShapeDtypeStruct((t_total, PEER_PAIRS), jnp.float32)],
        scratch_shapes=[pltpu.VMEM((PEER_PAIRS, tb), jnp.int32), pltpu.VMEM((PEER_PAIRS, tb), jnp.float32)],
        compiler_params=pltpu.CompilerParams(dimension_semantics=("arbitrary",)),
    )(x, mod, wq, keys)


MOD_COLS_PER_STEP = 1536


def _modulation_kernel(sc_ref, w_ref, b_ref, out_ref):
    acc = jnp.dot(sc_ref[...].astype(jnp.bfloat16), w_ref[0].astype(jnp.bfloat16),
                  preferred_element_type=jnp.float32)
    out_ref[0] = acc + b_ref[0]


def _modulation(sc, w_mod, b_mod):
    n_layers, d, n_out = w_mod.shape
    s = sc.shape[0]
    tn = MOD_COLS_PER_STEP
    return pl.pallas_call(
        _modulation_kernel,
        grid=(n_layers, n_out // tn),
        in_specs=[pl.BlockSpec((s, d), lambda l, j: (0, 0)),
                  pl.BlockSpec((1, d, tn), lambda l, j: (l, 0, j)),
                  pl.BlockSpec((1, 1, tn), lambda l, j: (l, 0, j))],
        out_specs=pl.BlockSpec((1, s, tn), lambda l, j: (l, 0, j)),
        out_shape=jax.ShapeDtypeStruct((n_layers, s, n_out), jnp.float32),
        compiler_params=pltpu.CompilerParams(dimension_semantics=("arbitrary", "arbitrary"),
                                             vmem_limit_bytes=40 * 1024 * 1024),
    )(sc, w_mod, b_mod.reshape(n_layers, 1, n_out))


def _residual_layer_norm(x, g_mod, mix, ln_g, ln_b):
    y = DN_ALPHA * x + g_mod * mix
    return _layer_norm(y, ln_g, ln_b)


MIXER_ROWS_PER_STEP = 256


def _chunk_mlp_kernel(x_ref, mod_ref, win_ref, ng_ref, nb_ref, ws_ref, bs_ref, wout_ref, lng_ref, lnb_ref,
                      out_ref, gated):
    x = x_ref[...]
    sh1 = mod_ref[0, 0:1, :]
    s1 = mod_ref[0, 1:2, :]
    g1 = mod_ref[0, 2:3, :]
    h = (x * (1.0 + s1) + sh1).astype(jnp.bfloat16)
    z = jax.nn.gelu(jnp.dot(h, win_ref[...], preferred_element_type=jnp.float32))
    u = z[:, :A_WIDTH]
    v = _layer_norm(z[:, A_WIDTH:], ng_ref[...], nb_ref[...]).astype(jnp.bfloat16)
    for c in range(x.shape[0] // CHUNK):
        r0 = c * CHUNK
        for hd in range(A_HEADS):
            c0 = hd * A_GROUP
            s = jnp.dot(ws_ref[hd], v[r0:r0 + CHUNK, c0:c0 + A_GROUP], preferred_element_type=jnp.float32)
            s = s + bs_ref[:, hd:hd + 1]
            gated[r0:r0 + CHUNK, c0:c0 + A_GROUP] = (u[r0:r0 + CHUNK, c0:c0 + A_GROUP] * s).astype(jnp.bfloat16)
    mix = jnp.dot(gated[...], wout_ref[...], preferred_element_type=jnp.float32)
    out_ref[...] = _residual_layer_norm(x, g1, mix, lng_ref[...], lnb_ref[...])


def _chunk_mlp_mixer(x, mod, w_in, norm_g, norm_b, w_s, b_s, w_out, ln_g, ln_b, *, rows_per_sample):
    t_total, d = x.shape
    rows = MIXER_ROWS_PER_STEP
    n_mod = mod.shape[0]
    blocks_per_sample = rows_per_sample // rows
    const = lambda *shape: pl.BlockSpec(shape, lambda i: (0,) * len(shape))
    return pl.pallas_call(
        _chunk_mlp_kernel,
        grid=(t_total // rows,),
        in_specs=[pl.BlockSpec((rows, d), lambda i: (i, 0)),
                  pl.BlockSpec((1, 6, d), lambda i: (jnp.minimum(i // blocks_per_sample, n_mod - 1), 0, 0)),
                  const(d, 2 * A_WIDTH), const(1, A_WIDTH), const(1, A_WIDTH),
                  const(A_HEADS, CHUNK, CHUNK), const(CHUNK, A_HEADS), const(A_WIDTH, d),
                  const(1, d), const(1, d)],
        out_specs=pl.BlockSpec((rows, d), lambda i: (i, 0)),
        out_shape=jax.ShapeDtypeStruct((t_total, d), jnp.float32),
        scratch_shapes=[pltpu.VMEM((rows, A_WIDTH), jnp.bfloat16)],
        compiler_params=pltpu.CompilerParams(dimension_semantics=("arbitrary",),
                                             vmem_limit_bytes=48 * 1024 * 1024),
    )(x, mod, w_in.astype(jnp.bfloat16), norm_g.reshape(1, -1), norm_b.reshape(1, -1),
      w_s.astype(jnp.bfloat16), b_s.T, w_out.astype(jnp.bfloat16), ln_g.reshape(1, d), ln_b.reshape(1, d))


GLA_MAIN = 2 * GLA_QK + 2 * GLA_V
GLA_GATE_PAD = 128


def _gla_project_kernel(x_ref, mod_ref, win_ref, wgin_ref, wg_ref, gb_ref, q_ref, k_ref, v_ref, r_ref, g_ref):
    sh1 = mod_ref[0, 0:1, :]
    s1 = mod_ref[0, 1:2, :]
    h = (x_ref[...] * (1.0 + s1) + sh1).astype(jnp.bfloat16)
    z = jnp.dot(h, win_ref[...], preferred_element_type=jnp.float32)
    q_ref[...] = z[:, :GLA_QK] * (GLA_DK ** -0.5)
    k_ref[...] = z[:, GLA_QK:2 * GLA_QK]
    v_ref[...] = z[:, 2 * GLA_QK:2 * GLA_QK + GLA_V]
    r_ref[...] = z[:, 2 * GLA_QK + GLA_V:]
    gl = jnp.dot(h, wgin_ref[...], preferred_element_type=jnp.float32).astype(jnp.bfloat16)
    zg = jnp.dot(gl, wg_ref[...], preferred_element_type=jnp.float32) + gb_ref[...]
    g_ref[...] = jax.nn.log_sigmoid(zg) / GLA_TAU


def _gla_project(x, mod, w_in, w_gate, gate_bias, *, rows_per_sample):
    t_total, d = x.shape
    rows = MIXER_ROWS_PER_STEP
    n_mod = mod.shape[0]
    blocks_per_sample = rows_per_sample // rows
    rank = GLA_GATE_RANK
    w_main = w_in[:, :GLA_MAIN].astype(jnp.bfloat16)
    w_gin = jnp.pad(w_in[:, GLA_MAIN:], ((0, 0), (0, GLA_GATE_PAD - 2 * rank))).astype(jnp.bfloat16)
    w_g2 = jnp.zeros((GLA_GATE_PAD, 2 * GLA_QK), jnp.float32)
    w_g2 = w_g2.at[:rank, :GLA_QK].set(w_gate[0]).at[rank:2 * rank, GLA_QK:].set(w_gate[1]).astype(jnp.bfloat16)
    const = lambda *shape: pl.BlockSpec(shape, lambda i: (0,) * len(shape))
    row = lambda n: pl.BlockSpec((rows, n), lambda i: (i, 0))
    f32 = jnp.float32
    return pl.pallas_call(
        _gla_project_kernel,
        grid=(t_total // rows,),
        in_specs=[row(d),
                  pl.BlockSpec((1, 6, d), lambda i: (jnp.minimum(i // blocks_per_sample, n_mod - 1), 0, 0)),
                  const(d, GLA_MAIN), const(d, GLA_GATE_PAD), const(GLA_GATE_PAD, 2 * GLA_QK),
                  const(1, 2 * GLA_QK)],
        out_specs=[row(GLA_QK), row(GLA_QK), row(GLA_V), row(GLA_V), row(2 * GLA_QK)],
        out_shape=[jax.ShapeDtypeStruct((t_total, GLA_QK), f32), jax.ShapeDtypeStruct((t_total, GLA_QK), f32),
                   jax.ShapeDtypeStruct((t_total, GLA_V), f32), jax.ShapeDtypeStruct((t_total, GLA_V), f32),
                   jax.ShapeDtypeStruct((t_total, 2 * GLA_QK), f32)],
        compiler_params=pltpu.CompilerParams(dimension_semantics=("arbitrary",),
                                             vmem_limit_bytes=48 * 1024 * 1024),
    )(x, mod, w_main, w_gin, w_g2, gate_bias.reshape(1, 2 * GLA_QK))


def _gla_chunk(q_ref, k_ref, v_ref, g_ref, o_ref, state, backward):
    n = GLA_CHUNK
    row = lax.broadcasted_iota(jnp.int32, (n, n), 0)
    col = lax.broadcasted_iota(jnp.int32, (n, n), 1)
    keep = (col >= row) if backward else (col <= row)
    tri = keep.astype(jnp.float32)
    bf = jnp.bfloat16
    for hd in range(GLA_HEADS):
        ck = slice(hd * GLA_DK, (hd + 1) * GLA_DK)
        cv = slice(hd * GLA_DV, (hd + 1) * GLA_DV)
        g = g_ref[:, ck]
        b = jnp.dot(tri, g, preferred_element_type=jnp.float32, precision=lax.Precision.HIGHEST)
        b_end = b[0:1, :] if backward else b[n - 1:n, :]
        q_in = (q_ref[:, ck] * jnp.exp(b)).astype(bf)
        kk = k_ref[:, ck]
        k_in = (kk * jnp.exp(-b)).astype(bf)
        k_end = (kk * jnp.exp(b_end - b)).astype(bf)
        vv = v_ref[:, cv].astype(bf)
        att = lax.dot_general(q_in, k_in, (((1,), (1,)), ((), ())), preferred_element_type=jnp.float32)
        att = jnp.where(keep, att, 0.0).astype(bf)
        s_old = state[hd]
        o = jnp.dot(att, vv, preferred_element_type=jnp.float32)
        o = o + lax.dot_general(q_in, s_old.astype(bf), (((1,), (1,)), ((), ())),
                                preferred_element_type=jnp.float32)
        o_ref[:, cv] = o
        vk = lax.dot_general(vv, k_end, (((0,), (0,)), ((), ())), preferred_element_type=jnp.float32)
        state[hd] = jnp.exp(b_end) * s_old + vk


def _gla_scan_kernel(qf_ref, kf_ref, vf_ref, gf_ref, qb_ref, kb_ref, vb_ref, gb_ref, of_ref, ob_ref,
                     state_f, state_b):
    @pl.when(pl.program_id(1) == 0)
    def _():
        state_f[...] = jnp.zeros_like(state_f)
        state_b[...] = jnp.zeros_like(state_b)

    _gla_chunk(qf_ref, kf_ref, vf_ref, gf_ref, of_ref, state_f, False)
    _gla_chunk(qb_ref, kb_ref, vb_ref, gb_ref, ob_ref, state_b, True)


def _gla_scan(q, k, v, g, *, bsz, lat_chunks, ctx_chunks):
    t_total = q.shape[0]
    n = GLA_CHUNK
    steps = ctx_chunks + lat_chunks
    ctx_base = bsz * lat_chunks

    def block(b, j, backward):
        if backward:
            c_ctx, c_lat = ctx_chunks - 1 - j, lat_chunks - 1 - (j - ctx_chunks)
        else:
            c_ctx, c_lat = j, j - ctx_chunks
        return jnp.where(j < ctx_chunks, ctx_base + b * ctx_chunks + c_ctx, b * lat_chunks + c_lat)

    def specs(backward):
        blk = functools.partial(block, backward=backward)
        return [pl.BlockSpec((n, GLA_QK), lambda b, j: (blk(b, j), 0)),
                pl.BlockSpec((n, GLA_QK), lambda b, j: (blk(b, j), 0)),
                pl.BlockSpec((n, GLA_V), lambda b, j: (blk(b, j), 0)),
                pl.BlockSpec((n, GLA_QK), lambda b, j: (blk(b, j), 1 if backward else 0))]

    out_spec = lambda backward: pl.BlockSpec((n, GLA_V), lambda b, j: (block(b, j, backward), 0))
    o_shape = jax.ShapeDtypeStruct((t_total, GLA_V), jnp.float32)
    return pl.pallas_call(
        _gla_scan_kernel,
        grid=(bsz, steps),
        in_specs=specs(False) + specs(True),
        out_specs=[out_spec(False), out_spec(True)],
        out_shape=[o_shape, o_shape],
        scratch_shapes=[pltpu.VMEM((GLA_HEADS, GLA_DV, GLA_DK), jnp.float32),
                        pltpu.VMEM((GLA_HEADS, GLA_DV, GLA_DK), jnp.float32)],
        compiler_params=pltpu.CompilerParams(dimension_semantics=("arbitrary", "arbitrary")),
    )(q, k, v, g, q, k, v, g)


def _gla_readout_kernel(x_ref, mod_ref, of_ref, ob_ref, r_ref, gn_ref, wout_ref, lng_ref, lnb_ref, out_ref, gated):
    x = x_ref[...]
    g1 = mod_ref[0, 2:3, :]
    o = of_ref[...] + ob_ref[...]
    r = r_ref[...]
    for hd in range(GLA_HEADS):
        cv = slice(hd * GLA_DV, (hd + 1) * GLA_DV)
        oh = o[:, cv]
        mu = jnp.mean(oh, axis=-1, keepdims=True)
        var = jnp.mean(jnp.square(oh - mu), axis=-1, keepdims=True)
        y = (oh - mu) * lax.rsqrt(var + LN_EPS) * gn_ref[:, cv]
        gated[:, cv] = (y * jax.nn.silu(r[:, cv])).astype(jnp.bfloat16)
    mix = jnp.dot(gated[...], wout_ref[...], preferred_element_type=jnp.float32)
    out_ref[...] = _residual_layer_norm(x, g1, mix, lng_ref[...], lnb_ref[...])


def _gla_readout(x, mod, o_f, o_b, r, gn_g, w_out, ln_g, ln_b, *, rows_per_sample):
    t_total, d = x.shape
    rows = MIXER_ROWS_PER_STEP
    n_mod = mod.shape[0]
    blocks_per_sample = rows_per_sample // rows
    const = lambda *shape: pl.BlockSpec(shape, lambda i: (0,) * len(shape))
    row = lambda n: pl.BlockSpec((rows, n), lambda i: (i, 0))
    return pl.pallas_call(
        _gla_readout_kernel,
        grid=(t_total // rows,),
        in_specs=[row(d),
                  pl.BlockSpec((1, 6, d), lambda i: (jnp.minimum(i // blocks_per_sample, n_mod - 1), 0, 0)),
                  row(GLA_V), row(GLA_V), row(GLA_V), const(1, GLA_V), const(GLA_V, d), const(1, d), const(1, d)],
        out_specs=row(d),
        out_shape=jax.ShapeDtypeStruct((t_total, d), jnp.float32),
        scratch_shapes=[pltpu.VMEM((rows, GLA_V), jnp.bfloat16)],
        compiler_params=pltpu.CompilerParams(dimension_semantics=("arbitrary",),
                                             vmem_limit_bytes=48 * 1024 * 1024),
    )(x, mod, o_f, o_b, r, gn_g.reshape(1, GLA_V), w_out.astype(jnp.bfloat16), ln_g.reshape(1, d), ln_b.reshape(1, d))


def _gla_mixer(x, mod, w_in, w_gate, gate_bias, gn_g, w_out, ln_g, ln_b, *, bsz, seq, ctx_len, n_out):
    q, k, v, r, g = _gla_project(x, mod, w_in, w_gate, gate_bias, rows_per_sample=seq)
    o_f, o_b = _gla_scan(q, k, v, g, bsz=bsz, lat_chunks=seq // GLA_CHUNK, ctx_chunks=ctx_len // GLA_CHUNK)
    return _gla_readout(x[:n_out], mod, o_f[:n_out], o_b[:n_out], r[:n_out], gn_g, w_out, ln_g, ln_b,
                        rows_per_sample=seq)


def kernel(x, c, ctx, c_ctx, w_mod, b_mod, ln_g, ln_b, a_w_in, a_norm_g, a_norm_b, a_w_s, a_b_s, a_w_out,
           b_w_in, b_w_gate, b_gate_bias, b_gn_g, b_w_out, p_w_q, p_keys, p_u, p_v):
    bsz, seq, d = x.shape
    ctx_len = ctx.shape[1]
    n_lat = bsz * seq
    xc = jnp.concatenate([x.reshape(n_lat, d), ctx.reshape(bsz * ctx_len, d)], axis=0)
    sc = jax.nn.silu(jnp.concatenate([c, c_ctx[None, :]], axis=0))
    mods = _modulation(sc, w_mod, b_mod).reshape(DEPTH, bsz + 1, 6, d)
    for i in range(DEPTH):
        last = i == DEPTH - 1
        j = i // N_MIXERS
        m = mods[i]
        n_tok = n_lat if last else xc.shape[0]
        if i % N_MIXERS == 0:
            x1 = _chunk_mlp_mixer(xc[:n_tok], m, a_w_in[j], a_norm_g[j], a_norm_b[j], a_w_s[j], a_b_s[j],
                                  a_w_out[j], ln_g[i, 0], ln_b[i, 0], rows_per_sample=seq)
        else:
            x1 = _gla_mixer(xc, m, b_w_in[j], b_w_gate[j], b_gate_bias[j], b_gn_g[j], b_w_out[j],
                            ln_g[i, 0], ln_b[i, 0], bsz=bsz, seq=seq, ctx_len=ctx_len, n_out=n_tok)
        idx, wts = _peer_route_pallas(x1, m, p_w_q[i], p_keys[i], tokens_per_sample_block=seq // 128)
        tab = jnp.stack([p_u[i], p_v[i]], axis=1)
        xc = _peer_experts(x1, m, idx, wts, tab, ln_g[i, 1], ln_b[i, 1],
                           tokens_per_sample_block=seq // PEER_TOKENS_PER_STEP)
    return xc[:n_lat].reshape(bsz, seq, d)
```

```python
import functools

import jax
import jax.numpy as jnp
from jax import lax
from jax.experimental import pallas as pl
from jax.experimental.pallas import tpu as pltpu

D_MODEL = 1024
DEPTH = 4
N_MIXERS = 2
DN_ALPHA = (2.0 * DEPTH) ** 0.25
LN_EPS = 1e-5
CHUNK = 128
A_WIDTH = D_MODEL
A_HEADS = 8
A_GROUP = A_WIDTH // A_HEADS
GLA_HEADS = 4
GLA_DK = D_MODEL // 2 // GLA_HEADS
GLA_DV = D_MODEL // GLA_HEADS
GLA_QK = GLA_HEADS * GLA_DK
GLA_V = GLA_HEADS * GLA_DV
GLA_GATE_RANK = 16
GLA_TAU = 16.0
GLA_CHUNK = 64
PEER_HEADS = 8
PEER_NKEYS = 128
PEER_QDIM = 128
PEER_HALF = PEER_QDIM // 2
PEER_TOPK = 16
PEER_PAIRS = PEER_HEADS * PEER_TOPK

PEER_TOKENS_PER_STEP = 128
PEER_ROW_BUFFERS = 8


def _layer_norm(x, g, b):
    mu = jnp.mean(x, axis=-1, keepdims=True)
    var = jnp.mean(jnp.square(x - mu), axis=-1, keepdims=True)
    return (x - mu) * lax.rsqrt(var + LN_EPS) * g + b


SUBLANES = 8
LANES = 128


def _sublane_sums(p):
    sub = lax.broadcasted_iota(jnp.int32, (SUBLANES, LANES), 0)

    def merge(a, b, sh, members):
        mask = functools.reduce(jnp.logical_or, [sub == j for j in members])
        return jnp.where(mask, a, b) + pltpu.roll(jnp.where(mask, b, a), sh, axis=0)

    c_e = merge(p[0], p[4], 4, (0, 5, 6, 7))
    d_e = merge(p[2], p[6], 4, (0, 1, 2, 7))
    c_f = merge(p[1], p[5], 4, (0, 1, 6, 7))
    d_f = merge(p[3], p[7], 4, (0, 1, 2, 3))
    e = merge(c_e, d_e, 2, (0, 3, 4, 7))
    f = merge(c_f, d_f, 2, (0, 1, 4, 5))
    return merge(e, f, 1, (0, 2, 4, 6))


def _peer_expert_kernel(zero_ref, idx_ref, idx_next_ref, w_ref, x_ref, mod_ref, lng_ref, lnb_ref, tab_ref,
                        out_ref, *scratch, tb, nbuf):
    rows = scratch[:nbuf]
    h_scr, ffn_scr, act_scr, sems = scratch[nbuf:]
    step = pl.program_id(0)
    last = pl.num_programs(0) - 1
    npairs = rows[0].shape[1]
    ahead = nbuf - 2
    sl = SUBLANES

    def issue(src_idx_ref, t, slot):
        z = zero_ref[0]
        for k in range(npairs):
            e = src_idx_ref[t, k]
            pltpu.make_async_copy(tab_ref.at[e], rows[slot].at[z, k], sems.at[slot]).start(priority=k % 2)

    def wait(slot):
        pltpu.make_async_copy(tab_ref.at[pl.ds(0, npairs)], rows[slot].at[0], sems.at[slot]).wait()

    @pl.when(step == 0)
    def _():
        for t in range(ahead):
            issue(idx_ref, t, t)

    s2 = mod_ref[0, 4]
    sh2 = mod_ref[0, 3]
    g2 = mod_ref[0, 5]
    h_scr[...] = x_ref[...].reshape(tb, sl, LANES) * (1.0 + s2) + sh2

    def token(t, slot, w_col):
        r = rows[slot].at[0]
        h = h_scr[t]
        sums = []
        for g in range(npairs // sl):
            sums.append(_sublane_sums([r[g * sl + j, 0:sl, :] * h for j in range(sl)]))
        s = jnp.sum(jnp.concatenate(sums, axis=0), axis=-1, keepdims=True)
        act_scr[...] = jnp.broadcast_to(jax.nn.gelu(s) * w_col, (npairs, LANES))
        accs = [jnp.zeros((sl, LANES), jnp.float32) for _ in range(4)]
        for k in range(npairs):
            a_k = jnp.broadcast_to(act_scr[pl.ds(k, 1), :], (sl, LANES))
            accs[k % 4] = accs[k % 4] + a_k * r[k, sl:2 * sl, :]
        ffn_scr[t] = (accs[0] + accs[1]) + (accs[2] + accs[3])

    n_groups = tb // nbuf

    def group(j8, is_last):
        w_t = w_ref[pl.ds(pl.multiple_of(j8 * nbuf, nbuf), nbuf), :].T
        for u in range(nbuf):
            t = j8 * nbuf + u
            wait(u)
            if not is_last or u + ahead < nbuf:
                issue(idx_ref, t + ahead, (u + ahead) % nbuf)
            else:
                issue(idx_next_ref, u + ahead - nbuf, (u + ahead) % nbuf)
            token(t, u, w_t[:, u:u + 1])

    def main_body(j8, carry):
        group(j8, False)
        return carry
    lax.fori_loop(0, n_groups - 1, main_body, 0)
    group(n_groups - 1, True)

    @pl.when(step == last)
    def _():
        for t in range(ahead):
            wait(t)

    ffn = (g2 * ffn_scr[...]).reshape(tb, sl * LANES)
    out_ref[...] = _layer_norm(DN_ALPHA * x_ref[...] + ffn, lng_ref[...], lnb_ref[...])


def _peer_experts(x, mod, idx, w, tab, ln_g, ln_b, *, tokens_per_sample_block):
    t_total, d = x.shape
    npairs = idx.shape[1]
    tb = PEER_TOKENS_PER_STEP
    nbuf = PEER_ROW_BUFFERS
    sl = SUBLANES
    assert d == sl * LANES and nbuf == sl and t_total % tb == 0 and tb % nbuf == 0 and npairs % sl == 0
    n_steps = t_total // tb
    n_mod = mod.shape[0]
    kern = functools.partial(_peer_expert_kernel, tb=tb, nbuf=nbuf)
    return pl.pallas_call(
        kern,
        grid=(n_steps,),
        in_specs=[
            pl.BlockSpec(memory_space=pltpu.SMEM),
            pl.BlockSpec((tb, npairs), lambda i: (i, 0), memory_space=pltpu.SMEM),
            pl.BlockSpec((tb, npairs), lambda i: (jnp.minimum(i + 1, n_steps - 1), 0), memory_space=pltpu.SMEM),
            pl.BlockSpec((tb, npairs), lambda i: (i, 0)),
            pl.BlockSpec((tb, d), lambda i: (i, 0)),
            pl.BlockSpec((1, 6, sl, LANES),
                         lambda i: (jnp.minimum(i // tokens_per_sample_block, n_mod - 1), 0, 0, 0)),
            pl.BlockSpec((1, d), lambda i: (0, 0)),
            pl.BlockSpec((1, d), lambda i: (0, 0)),
            pl.BlockSpec(memory_space=pl.ANY),
        ],
        out_specs=pl.BlockSpec((tb, d), lambda i: (i, 0)),
        out_shape=jax.ShapeDtypeStruct((t_total, d), jnp.float32),
        scratch_shapes=[pltpu.VMEM((1, npairs, 2 * sl, LANES), jnp.float32) for _ in range(nbuf)] + [
            pltpu.VMEM((tb, sl, LANES), jnp.float32),
            pltpu.VMEM((tb, sl, LANES), jnp.float32),
            pltpu.VMEM((npairs, LANES), jnp.float32),
            pltpu.SemaphoreType.DMA((nbuf,)),
        ],
        compiler_params=pltpu.CompilerParams(dimension_semantics=("arbitrary",),
                                             vmem_limit_bytes=40 * 1024 * 1024),
    )(jnp.zeros((1,), jnp.int32), idx, idx, w, x, mod.reshape(n_mod, 6, sl, LANES),
      ln_g.reshape(1, d), ln_b.reshape(1, d), tab)


_NEG_INF = float("-inf")


def _topk_rows(s, k):
    n, width = s.shape
    rows = lax.broadcasted_iota(jnp.int32, s.shape, 0)
    out_rows = lax.broadcasted_iota(jnp.int32, (k, width), 0)
    vals = jnp.zeros((k, width), jnp.float32)
    idxs = jnp.zeros((k, width), jnp.int32)
    for r in range(k):
        m = jnp.max(s, axis=0, keepdims=True)
        first = jnp.min(jnp.where(s == m, rows, n), axis=0, keepdims=True)
        vals = jnp.where(out_rows == r, m, vals)
        idxs = jnp.where(out_rows == r, first, idxs)
        s = jnp.where(rows == first, _NEG_INF, s)
    return vals, idxs


def _pair_candidates(s1, s2):
    k = PEER_TOPK
    sub = lax.broadcasted_iota(jnp.int32, (8, 1), 0)
    groups = []

    def add(vals, a, b, keep):
        ok = keep & ((a + 1) * (b + 1) <= k)
        groups.append((jnp.where(ok, vals, _NEG_INF), a * k + b))

    for b0 in (0, 8):
        add(s1[0:1, :] + s2[b0:b0 + 8, :], 0 * sub, b0 + sub, sub >= 0)
    add(s1[8:16, :] + s2[0:1, :], 8 + sub, 0 * sub, sub >= 0)
    for b in range(5):
        add(s1[0:8, :] + s2[b:b + 1, :], sub, b + 0 * sub, sub >= 1)
    add(s1[1:2, :] + s2[0:8, :], 1 + 0 * sub, sub, sub >= 5)
    return groups


def _topk_groups(groups, k):
    width = groups[0][0].shape[1]
    big = PEER_TOPK * PEER_TOPK
    out_rows = lax.broadcasted_iota(jnp.int32, (k, width), 0)
    vals = jnp.zeros((k, width), jnp.float32)
    poss = jnp.zeros((k, width), jnp.int32)
    gv = [g[0] for g in groups]
    gp = [g[1] for g in groups]
    for r in range(k):
        m = functools.reduce(jnp.maximum, gv)
        m = jnp.max(m, axis=0, keepdims=True)
        f = functools.reduce(jnp.minimum, [jnp.where(v == m, p, big) for v, p in zip(gv, gp)])
        f = jnp.min(f, axis=0, keepdims=True)
        vals = jnp.where(out_rows == r, m, vals)
        poss = jnp.where(out_rows == r, f, poss)
        gv = [jnp.where(p == f, _NEG_INF, v) for v, p in zip(gv, gp)]
    return vals, poss


def _select_rows(table, sel):
    out = jnp.zeros_like(table)
    for a in range(table.shape[0]):
        out = jnp.where(sel == a, table[a:a + 1, :], out)
    return out


def _peer_route_kernel(x_ref, mod_ref, wq_ref, keys_ref, idx_ref, w_ref, idx_t, w_t):
    k = PEER_TOPK
    s2 = mod_ref[0, 4:5, :]
    sh2 = mod_ref[0, 3:4, :]
    h = (x_ref[...] * (1.0 + s2) + sh2).astype(jnp.bfloat16)

    def head(hd, carry):
        q = jnp.dot(h, wq_ref[hd], preferred_element_type=jnp.float32).astype(jnp.bfloat16)
        nt = (((1,), (1,)), ((), ()))
        sc1 = lax.dot_general(keys_ref[0], q, nt, preferred_element_type=jnp.float32)
        sc2 = lax.dot_general(keys_ref[1], q, nt, preferred_element_type=jnp.float32)
        v1, i1 = _topk_rows(sc1, k)
        v2, i2 = _topk_rows(sc2, k)
        top_s, pos = _topk_groups(_pair_candidates(v1, v2), k)
        e = jnp.exp(top_s - top_s[0:1, :])
        wts = e / jnp.sum(e, axis=0, keepdims=True)
        a = lax.shift_right_logical(pos, 4)
        b = lax.bitwise_and(pos, k - 1)
        ids = _select_rows(i1, a) * PEER_NKEYS + _select_rows(i2, b)
        row0 = pl.multiple_of(hd * k, k)
        idx_t[pl.ds(row0, k), :] = ids
        w_t[pl.ds(row0, k), :] = wts
        return carry

    lax.fori_loop(0, PEER_HEADS, head, 0, unroll=4)
    idx_ref[...] = idx_t[...].T
    w_ref[...] = w_t[...].T


def _peer_route_pallas(x, mod, w_q, sub_keys, *, tokens_per_sample_block):
    t_total, d = x.shape
    tb = 128
    n_steps = t_total // tb
    n_mod = mod.shape[0]
    wq = w_q.reshape(d, PEER_HEADS, PEER_QDIM).transpose(1, 0, 2).astype(jnp.bfloat16)
    zeros = jnp.zeros((PEER_NKEYS, PEER_HALF), sub_keys.dtype)
    keys = jnp.stack([jnp.concatenate([sub_keys[0], zeros], axis=1),
                      jnp.concatenate([zeros, sub_keys[1]], axis=1)]).astype(jnp.bfloat16)
    return pl.pallas_call(
        _peer_route_kernel,
        grid=(n_steps,),
        in_specs=[
            pl.BlockSpec((tb, d), lambda i: (i, 0)),
            pl.BlockSpec((1, 6, d), lambda i: (jnp.minimum(i // tokens_per_sample_block, n_mod - 1), 0, 0)),
            pl.BlockSpec((PEER_HEADS, d, PEER_QDIM), lambda i: (0, 0, 0)),
            pl.BlockSpec((2, PEER_NKEYS, PEER_QDIM), lambda i: (0, 0, 0)),
        ],
        out_specs=[pl.BlockSpec((tb, PEER_PAIRS), lambda i: (i, 0)),
                   pl.BlockSpec((tb, PEER_PAIRS), lambda i: (i, 0))],
        out_shape=[jax.ShapeDtypeStruct((t_total, PEER_PAIRS), jnp.int32),
                   jax.ShapeDtypeStruct((t_total, PEER_PAIRS), jnp.float32)],
        scratch_shapes=[pltpu.VMEM((PEER_PAIRS, tb), jnp.int32), pltpu.VMEM((PEER_PAIRS, tb), jnp.float32)],
        compiler_params=pltpu.CompilerParams(dimension_semantics=("arbitrary",)),
    )(x, mod, wq, keys)


MOD_COLS_PER_STEP = 1536


def _modulation_kernel(sc_ref, w_ref, b_ref, out_ref):
    acc = jnp.dot(sc_ref[...].astype(jnp.bfloat16), w_ref[0].astype(jnp.bfloat16),
                  preferred_element_type=jnp.float32)
    out_ref[0] = acc + b_ref[0]


def _modulation(sc, w_mod, b_mod):
    n_layers, d, n_out = w_mod.shape
    s = sc.shape[0]
    tn = MOD_COLS_PER_STEP
    return pl.pallas_call(
        _modulation_kernel,
        grid=(n_layers, n_out // tn),
        in_specs=[pl.BlockSpec((s, d), lambda l, j: (0, 0)),
                  pl.BlockSpec((1, d, tn), lambda l, j: (l, 0, j)),
                  pl.BlockSpec((1, 1, tn), lambda l, j: (l, 0, j))],
        out_specs=pl.BlockSpec((1, s, tn), lambda l, j: (l, 0, j)),
        out_shape=jax.ShapeDtypeStruct((n_layers, s, n_out), jnp.float32),
        compiler_params=pltpu.CompilerParams(dimension_semantics=("arbitrary", "arbitrary"),
                                             vmem_limit_bytes=40 * 1024 * 1024),
    )(sc, w_mod, b_mod.reshape(n_layers, 1, n_out))


def _residual_layer_norm(x, g_mod, mix, ln_g, ln_b):
    y = DN_ALPHA * x + g_mod * mix
    return _layer_norm(y, ln_g, ln_b)


MIXER_ROWS_PER_STEP = 256


def _chunk_mlp_kernel(x_ref, mod_ref, win_ref, ng_ref, nb_ref, ws_ref, bs_ref, wout_ref, lng_ref, lnb_ref,
                      out_ref, gated):
    x = x_ref[...]
    sh1 = mod_ref[0, 0:1, :]
    s1 = mod_ref[0, 1:2, :]
    g1 = mod_ref[0, 2:3, :]
    h = (x * (1.0 + s1) + sh1).astype(jnp.bfloat16)
    z = jax.nn.gelu(jnp.dot(h, win_ref[...], preferred_element_type=jnp.float32))
    u = z[:, :A_WIDTH]
    v = _layer_norm(z[:, A_WIDTH:], ng_ref[...], nb_ref[...]).astype(jnp.bfloat16)
    for c in range(x.shape[0] // CHUNK):
        r0 = c * CHUNK
        for hd in range(A_HEADS):
            c0 = hd * A_GROUP
            s = jnp.dot(ws_ref[hd], v[r0:r0 + CHUNK, c0:c0 + A_GROUP], preferred_element_type=jnp.float32)
            s = s + bs_ref[:, hd:hd + 1]
            gated[r0:r0 + CHUNK, c0:c0 + A_GROUP] = (u[r0:r0 + CHUNK, c0:c0 + A_GROUP] * s).astype(jnp.bfloat16)
    mix = jnp.dot(gated[...], wout_ref[...], preferred_element_type=jnp.float32)
    out_ref[...] = _residual_layer_norm(x, g1, mix, lng_ref[...], lnb_ref[...])


def _chunk_mlp_mixer(x, mod, w_in, norm_g, norm_b, w_s, b_s, w_out, ln_g, ln_b, *, rows_per_sample):
    t_total, d = x.shape
    rows = MIXER_ROWS_PER_STEP
    n_mod = mod.shape[0]
    blocks_per_sample = rows_per_sample // rows
    const = lambda *shape: pl.BlockSpec(shape, lambda i: (0,) * len(shape))
    return pl.pallas_call(
        _chunk_mlp_kernel,
        grid=(t_total // rows,),
        in_specs=[pl.BlockSpec((rows, d), lambda i: (i, 0)),
                  pl.BlockSpec((1, 6, d), lambda i: (jnp.minimum(i // blocks_per_sample, n_mod - 1), 0, 0)),
                  const(d, 2 * A_WIDTH), const(1, A_WIDTH), const(1, A_WIDTH),
                  const(A_HEADS, CHUNK, CHUNK), const(CHUNK, A_HEADS), const(A_WIDTH, d),
                  const(1, d), const(1, d)],
        out_specs=pl.BlockSpec((rows, d), lambda i: (i, 0)),
        out_shape=jax.ShapeDtypeStruct((t_total, d), jnp.float32),
        scratch_shapes=[pltpu.VMEM((rows, A_WIDTH), jnp.bfloat16)],
        compiler_params=pltpu.CompilerParams(dimension_semantics=("arbitrary",),
                                             vmem_limit_bytes=48 * 1024 * 1024),
    )(x, mod, w_in.astype(jnp.bfloat16), norm_g.reshape(1, -1), norm_b.reshape(1, -1),
      w_s.astype(jnp.bfloat16), b_s.T, w_out.astype(jnp.bfloat16), ln_g.reshape(1, d), ln_b.reshape(1, d))


GLA_MAIN = 2 * GLA_QK + 2 * GLA_V
GLA_GATE_PAD = 128


def _gla_project_kernel(x_ref, mod_ref, win_ref, wgin_ref, wg_ref, gb_ref, q_ref, k_ref, v_ref, r_ref, g_ref):
    sh1 = mod_ref[0, 0:1, :]
    s1 = mod_ref[0, 1:2, :]
    h = (x_ref[...] * (1.0 + s1) + sh1).astype(jnp.bfloat16)
    z = jnp.dot(h, win_ref[...], preferred_element_type=jnp.float32)
    q_ref[...] = z[:, :GLA_QK] * (GLA_DK ** -0.5)
    k_ref[...] = z[:, GLA_QK:2 * GLA_QK]
    v_ref[...] = z[:, 2 * GLA_QK:2 * GLA_QK + GLA_V]
    r_ref[...] = z[:, 2 * GLA_QK + GLA_V:]
    gl = jnp.dot(h, wgin_ref[...], preferred_element_type=jnp.float32).astype(jnp.bfloat16)
    zg = jnp.dot(gl, wg_ref[...], preferred_element_type=jnp.float32) + gb_ref[...]
    g_ref[...] = jax.nn.log_sigmoid(zg) / GLA_TAU


def _gla_project(x, mod, w_in, w_gate, gate_bias, *, rows_per_sample):
    t_total, d = x.shape
    rows = MIXER_ROWS_PER_STEP
    n_mod = mod.shape[0]
    blocks_per_sample = rows_per_sample // rows
    rank = GLA_GATE_RANK
    w_main = w_in[:, :GLA_MAIN].astype(jnp.bfloat16)
    w_gin = jnp.pad(w_in[:, GLA_MAIN:], ((0, 0), (0, GLA_GATE_PAD - 2 * rank))).astype(jnp.bfloat16)
    w_g2 = jnp.zeros((GLA_GATE_PAD, 2 * GLA_QK), jnp.float32)
    w_g2 = w_g2.at[:rank, :GLA_QK].set(w_gate[0]).at[rank:2 * rank, GLA_QK:].set(w_gate[1]).astype(jnp.bfloat16)
    const = lambda *shape: pl.BlockSpec(shape, lambda i: (0,) * len(shape))
    row = lambda n: pl.BlockSpec((rows, n), lambda i: (i, 0))
    f32 = jnp.float32
    return pl.pallas_call(
        _gla_project_kernel,
        grid=(t_total // rows,),
        in_specs=[row(d),
                  pl.BlockSpec((1, 6, d), lambda i: (jnp.minimum(i // blocks_per_sample, n_mod - 1), 0, 0)),
                  const(d, GLA_MAIN), const(d, GLA_GATE_PAD), const(GLA_GATE_PAD, 2 * GLA_QK),
                  const(1, 2 * GLA_QK)],
        out_specs=[row(GLA_QK), row(GLA_QK), row(GLA_V), row(GLA_V), row(2 * GLA_QK)],
        out_shape=[jax.ShapeDtypeStruct((t_total, GLA_QK), f32), jax.ShapeDtypeStruct((t_total, GLA_QK), f32),
                   jax.ShapeDtypeStruct((t_total, GLA_V), f32), jax.ShapeDtypeStruct((t_total, GLA_V), f32),
                   jax.ShapeDtypeStruct((t_total, 2 * GLA_QK), f32)],
        compiler_params=pltpu.CompilerParams(dimension_semantics=("arbitrary",),
                                             vmem_limit_bytes=48 * 1024 * 1024),
    )(x, mod, w_main, w_gin, w_g2, gate_bias.reshape(1, 2 * GLA_QK))


def _gla_scan_kernel(qf_ref, kf_ref, vf_ref, gf_ref, qb_ref, kb_ref, vb_ref, gb_ref, of_ref, ob_ref,
                     state_f, state_b):
    @pl.when(pl.program_id(1) == 0)
    def _():
        state_f[...] = jnp.zeros_like(state_f)
        state_b[...] = jnp.zeros_like(state_b)

    n = GLA_CHUNK
    row = lax.broadcasted_iota(jnp.int32, (n, n), 0)
    col = lax.broadcasted_iota(jnp.int32, (n, n), 1)
    bf = jnp.bfloat16
    nt = (((1,), (1,)), ((), ()))
    tn = (((0,), (0,)), ((), ()))
    dirs = []
    for refs, backward in (((qf_ref, kf_ref, vf_ref, gf_ref, of_ref, state_f), False),
                           ((qb_ref, kb_ref, vb_ref, gb_ref, ob_ref, state_b), True)):
        q_ref, k_ref, v_ref, g_ref, o_ref, state = refs
        keep = (col >= row) if backward else (col <= row)
        b = jnp.dot(keep.astype(jnp.float32), g_ref[...], preferred_element_type=jnp.float32,
                    precision=lax.Precision.HIGHEST)
        b_end = b[0:1, :] if backward else b[n - 1:n, :]
        kk = k_ref[...]
        dirs.append(dict(keep=keep, o_ref=o_ref, state=state, decay=jnp.exp(b_end),
                         q_in=(q_ref[...] * jnp.exp(b)).astype(bf), k_in=(kk * jnp.exp(-b)).astype(bf),
                         k_end=(kk * jnp.exp(b_end - b)).astype(bf), v=v_ref[...].astype(bf)))
    heads = [(d, hd, slice(hd * GLA_DK, (hd + 1) * GLA_DK), slice(hd * GLA_DV, (hd + 1) * GLA_DV))
             for d in dirs for hd in range(GLA_HEADS)]
    att = [lax.dot_general(d["q_in"][:, ck], d["k_in"][:, ck], nt, preferred_element_type=jnp.float32)
           for d, hd, ck, cv in heads]
    att = [jnp.where(d["keep"], a, 0.0).astype(bf) for (d, hd, ck, cv), a in zip(heads, att)]
    old = [d["state"][hd] for d, hd, ck, cv in heads]
    inter = [lax.dot_general(d["q_in"][:, ck], s.astype(bf), nt, preferred_element_type=jnp.float32)
             for (d, hd, ck, cv), s in zip(heads, old)]
    intra = [jnp.dot(a, d["v"][:, cv], preferred_element_type=jnp.float32) for (d, hd, ck, cv), a in zip(heads, att)]
    vk = [lax.dot_general(d["v"][:, cv], d["k_end"][:, ck], tn, preferred_element_type=jnp.float32)
          for d, hd, ck, cv in heads]
    for (d, hd, ck, cv), o1, o2, s, u in zip(heads, intra, inter, old, vk):
        d["o_ref"][:, cv] = o1 + o2
        d["state"][hd] = d["decay"][:, ck] * s + u


def _gla_scan(q, k, v, g, *, bsz, lat_chunks, ctx_chunks):
    t_total = q.shape[0]
    n = GLA_CHUNK
    steps = ctx_chunks + lat_chunks
    ctx_base = bsz * lat_chunks

    def block(b, j, backward):
        if backward:
            c_ctx, c_lat = ctx_chunks - 1 - j, lat_chunks - 1 - (j - ctx_chunks)
        else:
            c_ctx, c_lat = j, j - ctx_chunks
        return jnp.where(j < ctx_chunks, ctx_base + b * ctx_chunks + c_ctx, b * lat_chunks + c_lat)

    def specs(backward):
        blk = functools.partial(block, backward=backward)
        return [pl.BlockSpec((n, GLA_QK), lambda b, j: (blk(b, j), 0)),
                pl.BlockSpec((n, GLA_QK), lambda b, j: (blk(b, j), 0)),
                pl.BlockSpec((n, GLA_V), lambda b, j: (blk(b, j), 0)),
                pl.BlockSpec((n, GLA_QK), lambda b, j: (blk(b, j), 1 if backward else 0))]

    out_spec = lambda backward: pl.BlockSpec((n, GLA_V), lambda b, j: (block(b, j, backward), 0))
    o_shape = jax.ShapeDtypeStruct((t_total, GLA_V), jnp.float32)
    return pl.pallas_call(
        _gla_scan_kernel,
        grid=(bsz, steps),
        in_specs=specs(False) + specs(True),
        out_specs=[out_spec(False), out_spec(True)],
        out_shape=[o_shape, o_shape],
        scratch_shapes=[pltpu.VMEM((GLA_HEADS, GLA_DV, GLA_DK), jnp.float32),
                        pltpu.VMEM((GLA_HEADS, GLA_DV, GLA_DK), jnp.float32)],
        compiler_params=pltpu.CompilerParams(dimension_semantics=("arbitrary", "arbitrary")),
    )(q, k, v, g, q, k, v, g)


def _gla_readout_kernel(x_ref, mod_ref, of_ref, ob_ref, r_ref, gn_ref, wout_ref, lng_ref, lnb_ref, out_ref, gated):
    x = x_ref[...]
    g1 = mod_ref[0, 2:3, :]
    o = of_ref[...] + ob_ref[...]
    r = r_ref[...]
    for hd in range(GLA_HEADS):
        cv = slice(hd * GLA_DV, (hd + 1) * GLA_DV)
        oh = o[:, cv]
        mu = jnp.mean(oh, axis=-1, keepdims=True)
        var = jnp.mean(jnp.square(oh - mu), axis=-1, keepdims=True)
        y = (oh - mu) * lax.rsqrt(var + LN_EPS) * gn_ref[:, cv]
        gated[:, cv] = (y * jax.nn.silu(r[:, cv])).astype(jnp.bfloat16)
    mix = jnp.dot(gated[...], wout_ref[...], preferred_element_type=jnp.float32)
    out_ref[...] = _residual_layer_norm(x, g1, mix, lng_ref[...], lnb_ref[...])


def _gla_readout(x, mod, o_f, o_b, r, gn_g, w_out, ln_g, ln_b, *, rows_per_sample):
    t_total, d = x.shape
    rows = MIXER_ROWS_PER_STEP
    n_mod = mod.shape[0]
    blocks_per_sample = rows_per_sample // rows
    const = lambda *shape: pl.BlockSpec(shape, lambda i: (0,) * len(shape))
    row = lambda n: pl.BlockSpec((rows, n), lambda i: (i, 0))
    return pl.pallas_call(
        _gla_readout_kernel,
        grid=(t_total // rows,),
        in_specs=[row(d),
                  pl.BlockSpec((1, 6, d), lambda i: (jnp.minimum(i // blocks_per_sample, n_mod - 1), 0, 0)),
                  row(GLA_V), row(GLA_V), row(GLA_V), const(1, GLA_V), const(GLA_V, d), const(1, d), const(1, d)],
        out_specs=row(d),
        out_shape=jax.ShapeDtypeStruct((t_total, d), jnp.float32),
        scratch_shapes=[pltpu.VMEM((rows, GLA_V), jnp.bfloat16)],
        compiler_params=pltpu.CompilerParams(dimension_semantics=("arbitrary",),
                                             vmem_limit_bytes=48 * 1024 * 1024),
    )(x, mod, o_f, o_b, r, gn_g.reshape(1, GLA_V), w_out.astype(jnp.bfloat16), ln_g.reshape(1, d), ln_b.reshape(1, d))


def _gla_mixer(x, mod, w_in, w_gate, gate_bias, gn_g, w_out, ln_g, ln_b, *, bsz, seq, ctx_len, n_out):
    q, k, v, r, g = _gla_project(x, mod, w_in, w_gate, gate_bias, rows_per_sample=seq)
    o_f, o_b = _gla_scan(q, k, v, g, bsz=bsz, lat_chunks=seq // GLA_CHUNK, ctx_chunks=ctx_len // GLA_CHUNK)
    return _gla_readout(x[:n_out], mod, o_f[:n_out], o_b[:n_out], r[:n_out], gn_g, w_out, ln_g, ln_b,
                        rows_per_sample=seq)


def kernel(x, c, ctx, c_ctx, w_mod, b_mod, ln_g, ln_b, a_w_in, a_norm_g, a_norm_b, a_w_s, a_b_s, a_w_out,
           b_w_in, b_w_gate, b_gate_bias, b_gn_g, b_w_out, p_w_q, p_keys, p_u, p_v):
    bsz, seq, d = x.shape
    ctx_len = ctx.shape[1]
    n_lat = bsz * seq
    xc = jnp.concatenate([x.reshape(n_lat, d), ctx.reshape(bsz * ctx_len, d)], axis=0)
    sc = jax.nn.silu(jnp.concatenate([c, c_ctx[None, :]], axis=0))
    mods = _modulation(sc, w_mod, b_mod).reshape(DEPTH, bsz + 1, 6, d)
    for i in range(DEPTH):
        last = i == DEPTH - 1
        j = i // N_MIXERS
        m = mods[i]
        n_tok = n_lat if last else xc.shape[0]
        if i % N_MIXERS == 0:
            x1 = _chunk_mlp_mixer(xc[:n_tok], m, a_w_in[j], a_norm_g[j], a_norm_b[j], a_w_s[j], a_b_s[j],
                                  a_w_out[j], ln_g[i, 0], ln_b[i, 0], rows_per_sample=seq)
        else:
            x1 = _gla_mixer(xc, m, b_w_in[j], b_w_gate[j], b_gate_bias[j], b_gn_g[j], b_w_out[j],
                            ln_g[i, 0], ln_b[i, 0], bsz=bsz, seq=seq, ctx_len=ctx_len, n_out=n_tok)
        idx, wts = _peer_route_pallas(x1, m, p_w_q[i], p_keys[i], tokens_per_sample_block=seq // 128)
        n_exp = p_u.shape[1]
        tab = jnp.concatenate([p_u[i].reshape(n_exp, SUBLANES, LANES), p_v[i].reshape(n_exp, SUBLANES, LANES)],
                              axis=1)
        xc = _peer_experts(x1, m, idx, wts, tab, ln_g[i, 1], ln_b[i, 1],
                           tokens_per_sample_block=seq // PEER_TOKENS_PER_STEP)
    return xc[:n_lat].reshape(bsz, seq, d)
```

```python
import functools

import jax
import jax.numpy as jnp
from jax import lax
from jax.experimental import pallas as pl
from jax.experimental.pallas import tpu as pltpu

D_MODEL = 1024
DEPTH = 4
N_MIXERS = 2
DN_ALPHA = (2.0 * DEPTH) ** 0.25
LN_EPS = 1e-5
CHUNK = 128
A_WIDTH = D_MODEL
A_HEADS = 8
A_GROUP = A_WIDTH // A_HEADS
GLA_HEADS = 4
GLA_DK = D_MODEL // 2 // GLA_HEADS
GLA_DV = D_MODEL // GLA_HEADS
GLA_QK = GLA_HEADS * GLA_DK
GLA_V = GLA_HEADS * GLA_DV
GLA_GATE_RANK = 16
GLA_TAU = 16.0
GLA_CHUNK = 64
PEER_HEADS = 8
PEER_NKEYS = 128
PEER_QDIM = 128
PEER_HALF = PEER_QDIM // 2
PEER_TOPK = 16
PEER_PAIRS = PEER_HEADS * PEER_TOPK

PEER_TOKENS_PER_STEP = 128
PEER_ROW_BUFFERS = 8
PEER_ID_WINDOWS = 4


def _layer_norm(x, g, b):
    mu = jnp.mean(x, axis=-1, keepdims=True)
    var = jnp.mean(jnp.square(x - mu), axis=-1, keepdims=True)
    return (x - mu) * lax.rsqrt(var + LN_EPS) * g + b


SUBLANES = 8
LANES = 128


def _sublane_sums(p):
    sub = lax.broadcasted_iota(jnp.int32, (SUBLANES, LANES), 0)

    def merge(a, b, sh, members):
        mask = functools.reduce(jnp.logical_or, [sub == j for j in members])
        return jnp.where(mask, a, b) + pltpu.roll(jnp.where(mask, b, a), sh, axis=0)

    c_e = merge(p[0], p[4], 4, (0, 5, 6, 7))
    d_e = merge(p[2], p[6], 4, (0, 1, 2, 7))
    c_f = merge(p[1], p[5], 4, (0, 1, 6, 7))
    d_f = merge(p[3], p[7], 4, (0, 1, 2, 3))
    e = merge(c_e, d_e, 2, (0, 3, 4, 7))
    f = merge(c_f, d_f, 2, (0, 1, 4, 5))
    return merge(e, f, 1, (0, 2, 4, 6))


def _peer_expert_kernel(zero_ref, idx_ref, idx_next_ref, w_ref, x_ref, mod_ref, lng_ref, lnb_ref, tab_ref,
                        out_ref, *scratch, tb, nbuf):
    rows = scratch[:nbuf]
    h_scr, ffn_scr, act_scr, win, sems, wsems = scratch[nbuf:]
    step = pl.program_id(0)
    last = pl.num_programs(0) - 1
    npairs = rows[0].shape[1]
    ahead = nbuf - 2
    lead = nbuf
    nwin = win.shape[0]
    sl = SUBLANES

    def issue(ids, slot):
        z = zero_ref[0]
        for k in range(npairs):
            pltpu.make_async_copy(tab_ref.at[ids(k)], rows[slot].at[z, k], sems.at[slot]).start(priority=k % 2)

    def wait(slot):
        pltpu.make_async_copy(tab_ref.at[pl.ds(0, npairs)], rows[slot].at[0], sems.at[slot]).wait()

    def window_copy(src_idx_ref, row, ws):
        return pltpu.make_async_copy(src_idx_ref.at[pl.ds(row, 1)], win.at[pl.ds(ws, 1)], wsems.at[ws])

    @pl.when(step == 0)
    def _():
        for t in range(ahead):
            issue(lambda k, t=t: idx_ref[t, k], t)

    for t in range(ahead, lead):
        window_copy(idx_ref, t, t % nwin).start()

    s2 = mod_ref[0, 4]
    sh2 = mod_ref[0, 3]
    g2 = mod_ref[0, 5]
    h_scr[...] = x_ref[...].reshape(tb, sl, LANES) * (1.0 + s2) + sh2

    def token(t, slot, w_col):
        r = rows[slot].at[0]
        h = h_scr[t]
        sums = []
        for g in range(npairs // sl):
            sums.append(_sublane_sums([r[g * sl + j, 0:sl, :] * h for j in range(sl)]))
        s = jnp.sum(jnp.concatenate(sums, axis=0), axis=-1, keepdims=True)
        act_scr[...] = jnp.broadcast_to(jax.nn.gelu(s) * w_col, (npairs, LANES))
        accs = [jnp.zeros((sl, LANES), jnp.float32) for _ in range(4)]
        for k in range(npairs):
            a_k = jnp.broadcast_to(act_scr[pl.ds(k, 1), :], (sl, LANES))
            accs[k % 4] = accs[k % 4] + a_k * r[k, sl:2 * sl, :]
        ffn_scr[t] = (accs[0] + accs[1]) + (accs[2] + accs[3])

    n_groups = tb // nbuf

    def group(j8, is_last):
        w_t = w_ref[pl.ds(pl.multiple_of(j8 * nbuf, nbuf), nbuf), :].T
        for u in range(nbuf):
            t = j8 * nbuf + u
            wait(u)
            if not is_last:
                window_copy(idx_ref, t + lead, (u + lead) % nwin).start()
            elif u < ahead:
                window_copy(idx_next_ref, u, (u + lead) % nwin).start()
            ws = (u + ahead) % nwin
            window_copy(idx_ref, 0, ws).wait()
            issue(lambda k, ws=ws: win[ws, k], (u + ahead) % nbuf)
            token(t, u, w_t[:, u:u + 1])

    def main_body(j8, carry):
        group(j8, False)
        return carry
    lax.fori_loop(0, n_groups - 1, main_body, 0)
    group(n_groups - 1, True)

    @pl.when(step == last)
    def _():
        for t in range(ahead):
            wait(t)

    ffn = (g2 * ffn_scr[...]).reshape(tb, sl * LANES)
    out_ref[...] = _layer_norm(DN_ALPHA * x_ref[...] + ffn, lng_ref[...], lnb_ref[...])


def _peer_experts(x, mod, idx, w, tab, ln_g, ln_b, *, tokens_per_sample_block):
    t_total, d = x.shape
    npairs = idx.shape[1]
    tb = PEER_TOKENS_PER_STEP
    nbuf = PEER_ROW_BUFFERS
    sl = SUBLANES
    assert d == sl * LANES and nbuf == sl and t_total % tb == 0 and tb % nbuf == 0 and npairs % sl == 0
    assert nbuf % PEER_ID_WINDOWS == 0 and PEER_ID_WINDOWS > 2
    n_steps = t_total // tb
    n_mod = mod.shape[0]
    kern = functools.partial(_peer_expert_kernel, tb=tb, nbuf=nbuf)
    return pl.pallas_call(
        kern,
        grid=(n_steps,),
        in_specs=[
            pl.BlockSpec(memory_space=pltpu.SMEM),
            pl.BlockSpec((tb, npairs), lambda i: (i, 0), memory_space=pltpu.SMEM),
            pl.BlockSpec((tb, npairs), lambda i: (jnp.minimum(i + 1, n_steps - 1), 0), memory_space=pltpu.SMEM),
            pl.BlockSpec((tb, npairs), lambda i: (i, 0)),
            pl.BlockSpec((tb, d), lambda i: (i, 0)),
            pl.BlockSpec((1, 6, sl, LANES),
                         lambda i: (jnp.minimum(i // tokens_per_sample_block, n_mod - 1), 0, 0, 0)),
            pl.BlockSpec((1, d), lambda i: (0, 0)),
            pl.BlockSpec((1, d), lambda i: (0, 0)),
            pl.BlockSpec(memory_space=pl.ANY),
        ],
        out_specs=pl.BlockSpec((tb, d), lambda i: (i, 0)),
        out_shape=jax.ShapeDtypeStruct((t_total, d), jnp.float32),
        scratch_shapes=[pltpu.VMEM((1, npairs, 2 * sl, LANES), jnp.float32) for _ in range(nbuf)] + [
            pltpu.VMEM((tb, sl, LANES), jnp.float32),
            pltpu.VMEM((tb, sl, LANES), jnp.float32),
            pltpu.VMEM((npairs, LANES), jnp.float32),
            pltpu.SMEM((PEER_ID_WINDOWS, npairs), jnp.int32),
            pltpu.SemaphoreType.DMA((nbuf,)),
            pltpu.SemaphoreType.DMA((PEER_ID_WINDOWS,)),
        ],
        compiler_params=pltpu.CompilerParams(dimension_semantics=("arbitrary",),
                                             vmem_limit_bytes=40 * 1024 * 1024),
    )(jnp.zeros((1,), jnp.int32), idx, idx, w, x, mod.reshape(n_mod, 6, sl, LANES),
      ln_g.reshape(1, d), ln_b.reshape(1, d), tab)


_NEG_INF = float("-inf")


def _topk_rows(s, k):
    n, width = s.shape
    rows = lax.broadcasted_iota(jnp.int32, s.shape, 0)
    out_rows = lax.broadcasted_iota(jnp.int32, (k, width), 0)
    vals = jnp.zeros((k, width), jnp.float32)
    idxs = jnp.zeros((k, width), jnp.int32)
    for r in range(k):
        m = jnp.max(s, axis=0, keepdims=True)
        first = jnp.min(jnp.where(s == m, rows, n), axis=0, keepdims=True)
        vals = jnp.where(out_rows == r, m, vals)
        idxs = jnp.where(out_rows == r, first, idxs)
        s = jnp.where(rows == first, _NEG_INF, s)
    return vals, idxs


def _pair_candidates(s1, s2):
    k = PEER_TOPK
    sub = lax.broadcasted_iota(jnp.int32, (8, 1), 0)
    groups = []

    def add(vals, a, b, keep):
        ok = keep & ((a + 1) * (b + 1) <= k)
        groups.append((jnp.where(ok, vals, _NEG_INF), a * k + b))

    for b0 in (0, 8):
        add(s1[0:1, :] + s2[b0:b0 + 8, :], 0 * sub, b0 + sub, sub >= 0)
    add(s1[8:16, :] + s2[0:1, :], 8 + sub, 0 * sub, sub >= 0)
    for b in range(5):
        add(s1[0:8, :] + s2[b:b + 1, :], sub, b + 0 * sub, sub >= 1)
    add(s1[1:2, :] + s2[0:8, :], 1 + 0 * sub, sub, sub >= 5)
    return groups


def _topk_groups(groups, k):
    width = groups[0][0].shape[1]
    big = PEER_TOPK * PEER_TOPK
    out_rows = lax.broadcasted_iota(jnp.int32, (k, width), 0)
    vals = jnp.zeros((k, width), jnp.float32)
    poss = jnp.zeros((k, width), jnp.int32)
    gv = [g[0] for g in groups]
    gp = [g[1] for g in groups]
    for r in range(k):
        m = functools.reduce(jnp.maximum, gv)
        m = jnp.max(m, axis=0, keepdims=True)
        f = functools.reduce(jnp.minimum, [jnp.where(v == m, p, big) for v, p in zip(gv, gp)])
        f = jnp.min(f, axis=0, keepdims=True)
        vals = jnp.where(out_rows == r, m, vals)
        poss = jnp.where(out_rows == r, f, poss)
        gv = [jnp.where(p == f, _NEG_INF, v) for v, p in zip(gv, gp)]
    return vals, poss


def _select_rows(table, sel):
    out = jnp.zeros_like(table)
    for a in range(table.shape[0]):
        out = jnp.where(sel == a, table[a:a + 1, :], out)
    return out


def _peer_route_kernel(x_ref, mod_ref, wq_ref, keys_ref, idx_ref, w_ref, idx_t, w_t):
    k = PEER_TOPK
    s2 = mod_ref[0, 4:5, :]
    sh2 = mod_ref[0, 3:4, :]
    h = (x_ref[...] * (1.0 + s2) + sh2).astype(jnp.bfloat16)

    def head(hd, carry):
        q = jnp.dot(h, wq_ref[hd], preferred_element_type=jnp.float32).astype(jnp.bfloat16)
        nt = (((1,), (1,)), ((), ()))
        sc1 = lax.dot_general(keys_ref[0], q, nt, preferred_element_type=jnp.float32)
        sc2 = lax.dot_general(keys_ref[1], q, nt, preferred_element_type=jnp.float32)
        v1, i1 = _topk_rows(sc1, k)
        v2, i2 = _topk_rows(sc2, k)
        top_s, pos = _topk_groups(_pair_candidates(v1, v2), k)
        e = jnp.exp(top_s - top_s[0:1, :])
        wts = e / jnp.sum(e, axis=0, keepdims=True)
        a = lax.shift_right_logical(pos, 4)
        b = lax.bitwise_and(pos, k - 1)
        ids = _select_rows(i1, a) * PEER_NKEYS + _select_rows(i2, b)
        row0 = pl.multiple_of(hd * k, k)
        idx_t[pl.ds(row0, k), :] = ids
        w_t[pl.ds(row0, k), :] = wts
        return carry

    lax.fori_loop(0, PEER_HEADS, head, 0, unroll=4)
    idx_ref[...] = idx_t[...].T
    w_ref[...] = w_t[...].T


def _peer_route_pallas(x, mod, w_q, sub_keys, *, tokens_per_sample_block):
    t_total, d = x.shape
    tb = 128
    n_steps = t_total // tb
    n_mod = mod.shape[0]
    wq = w_q.reshape(d, PEER_HEADS, PEER_QDIM).transpose(1, 0, 2).astype(jnp.bfloat16)
    zeros = jnp.zeros((PEER_NKEYS, PEER_HALF), sub_keys.dtype)
    keys = jnp.stack([jnp.concatenate([sub_keys[0], zeros], axis=1),
                      jnp.concatenate([zeros, sub_keys[1]], axis=1)]).astype(jnp.bfloat16)
    return pl.pallas_call(
        _peer_route_kernel,
        grid=(n_steps,),
        in_specs=[
            pl.BlockSpec((tb, d), lambda i: (i, 0)),
            pl.BlockSpec((1, 6, d), lambda i: (jnp.minimum(i // tokens_per_sample_block, n_mod - 1), 0, 0)),
            pl.BlockSpec((PEER_HEADS, d, PEER_QDIM), lambda i: (0, 0, 0)),
            pl.BlockSpec((2, PEER_NKEYS, PEER_QDIM), lambda i: (0, 0, 0)),
        ],
        out_specs=[pl.BlockSpec((tb, PEER_PAIRS), lambda i: (i, 0)),
                   pl.BlockSpec((tb, PEER_PAIRS), lambda i: (i, 0))],
        out_shape=[jax.ShapeDtypeStruct((t_total, PEER_PAIRS), jnp.int32),
                   jax.ShapeDtypeStruct((t_total, PEER_PAIRS), jnp.float32)],
        scratch_shapes=[pltpu.VMEM((PEER_PAIRS, tb), jnp.int32), pltpu.VMEM((PEER_PAIRS, tb), jnp.float32)],
        compiler_params=pltpu.CompilerParams(dimension_semantics=("arbitrary",)),
    )(x, mod, wq, keys)


MOD_COLS_PER_STEP = 1536


def _modulation_kernel(sc_ref, w_ref, b_ref, out_ref):
    acc = jnp.dot(sc_ref[...].astype(jnp.bfloat16), w_ref[0].astype(jnp.bfloat16),
                  preferred_element_type=jnp.float32)
    out_ref[0] = acc + b_ref[0]


def _modulation(sc, w_mod, b_mod):
    n_layers, d, n_out = w_mod.shape
    s = sc.shape[0]
    tn = MOD_COLS_PER_STEP
    return pl.pallas_call(
        _modulation_kernel,
        grid=(n_layers, n_out // tn),
        in_specs=[pl.BlockSpec((s, d), lambda l, j: (0, 0)),
                  pl.BlockSpec((1, d, tn), lambda l, j: (l, 0, j)),
                  pl.BlockSpec((1, 1, tn), lambda l, j: (l, 0, j))],
        out_specs=pl.BlockSpec((1, s, tn), lambda l, j: (l, 0, j)),
        out_shape=jax.ShapeDtypeStruct((n_layers, s, n_out), jnp.float32),
        compiler_params=pltpu.CompilerParams(dimension_semantics=("arbitrary", "arbitrary"),
                                             vmem_limit_bytes=40 * 1024 * 1024),
    )(sc, w_mod, b_mod.reshape(n_layers, 1, n_out))


def _residual_layer_norm(x, g_mod, mix, ln_g, ln_b):
    y = DN_ALPHA * x + g_mod * mix
    return _layer_norm(y, ln_g, ln_b)


MIXER_ROWS_PER_STEP = 256


def _chunk_mlp_kernel(x_ref, mod_ref, win_ref, ng_ref, nb_ref, ws_ref, bs_ref, wout_ref, lng_ref, lnb_ref,
                      out_ref, gated):
    x = x_ref[...]
    sh1 = mod_ref[0, 0:1, :]
    s1 = mod_ref[0, 1:2, :]
    g1 = mod_ref[0, 2:3, :]
    h = (x * (1.0 + s1) + sh1).astype(jnp.bfloat16)
    z = jax.nn.gelu(jnp.dot(h, win_ref[...], preferred_element_type=jnp.float32))
    u = z[:, :A_WIDTH]
    v = _layer_norm(z[:, A_WIDTH:], ng_ref[...], nb_ref[...]).astype(jnp.bfloat16)
    for c in range(x.shape[0] // CHUNK):
        r0 = c * CHUNK
        for hd in range(A_HEADS):
            c0 = hd * A_GROUP
            s = jnp.dot(ws_ref[hd], v[r0:r0 + CHUNK, c0:c0 + A_GROUP], preferred_element_type=jnp.float32)
            s = s + bs_ref[:, hd:hd + 1]
            gated[r0:r0 + CHUNK, c0:c0 + A_GROUP] = (u[r0:r0 + CHUNK, c0:c0 + A_GROUP] * s).astype(jnp.bfloat16)
    mix = jnp.dot(gated[...], wout_ref[...], preferred_element_type=jnp.float32)
    out_ref[...] = _residual_layer_norm(x, g1, mix, lng_ref[...], lnb_ref[...])


def _chunk_mlp_mixer(x, mod, w_in, norm_g, norm_b, w_s, b_s, w_out, ln_g, ln_b, *, rows_per_sample):
    t_total, d = x.shape
    rows = MIXER_ROWS_PER_STEP
    n_mod = mod.shape[0]
    blocks_per_sample = rows_per_sample // rows
    const = lambda *shape: pl.BlockSpec(shape, lambda i: (0,) * len(shape))
    return pl.pallas_call(
        _chunk_mlp_kernel,
        grid=(t_total // rows,),
        in_specs=[pl.BlockSpec((rows, d), lambda i: (i, 0)),
                  pl.BlockSpec((1, 6, d), lambda i: (jnp.minimum(i // blocks_per_sample, n_mod - 1), 0, 0)),
                  const(d, 2 * A_WIDTH), const(1, A_WIDTH), const(1, A_WIDTH),
                  const(A_HEADS, CHUNK, CHUNK), const(CHUNK, A_HEADS), const(A_WIDTH, d),
                  const(1, d), const(1, d)],
        out_specs=pl.BlockSpec((rows, d), lambda i: (i, 0)),
        out_shape=jax.ShapeDtypeStruct((t_total, d), jnp.float32),
        scratch_shapes=[pltpu.VMEM((rows, A_WIDTH), jnp.bfloat16)],
        compiler_params=pltpu.CompilerParams(dimension_semantics=("arbitrary",),
                                             vmem_limit_bytes=48 * 1024 * 1024),
    )(x, mod, w_in.astype(jnp.bfloat16), norm_g.reshape(1, -1), norm_b.reshape(1, -1),
      w_s.astype(jnp.bfloat16), b_s.T, w_out.astype(jnp.bfloat16), ln_g.reshape(1, d), ln_b.reshape(1, d))


GLA_MAIN = 2 * GLA_QK + 2 * GLA_V
GLA_GATE_PAD = 128


def _gla_project_kernel(x_ref, mod_ref, win_ref, wgin_ref, wg_ref, gb_ref, q_ref, k_ref, v_ref, r_ref, g_ref):
    sh1 = mod_ref[0, 0:1, :]
    s1 = mod_ref[0, 1:2, :]
    h = (x_ref[...] * (1.0 + s1) + sh1).astype(jnp.bfloat16)
    z = jnp.dot(h, win_ref[...], preferred_element_type=jnp.float32)
    q_ref[...] = z[:, :GLA_QK] * (GLA_DK ** -0.5)
    k_ref[...] = z[:, GLA_QK:2 * GLA_QK]
    v_ref[...] = z[:, 2 * GLA_QK:2 * GLA_QK + GLA_V]
    r_ref[...] = z[:, 2 * GLA_QK + GLA_V:]
    gl = jnp.dot(h, wgin_ref[...], preferred_element_type=jnp.float32).astype(jnp.bfloat16)
    zg = jnp.dot(gl, wg_ref[...], preferred_element_type=jnp.float32) + gb_ref[...]
    g_ref[...] = jax.nn.log_sigmoid(zg) / GLA_TAU


def _gla_project(x, mod, w_in, w_gate, gate_bias, *, rows_per_sample):
    t_total, d = x.shape
    rows = MIXER_ROWS_PER_STEP
    n_mod = mod.shape[0]
    blocks_per_sample = rows_per_sample // rows
    rank = GLA_GATE_RANK
    w_main = w_in[:, :GLA_MAIN].astype(jnp.bfloat16)
    w_gin = jnp.pad(w_in[:, GLA_MAIN:], ((0, 0), (0, GLA_GATE_PAD - 2 * rank))).astype(jnp.bfloat16)
    w_g2 = jnp.zeros((GLA_GATE_PAD, 2 * GLA_QK), jnp.float32)
    w_g2 = w_g2.at[:rank, :GLA_QK].set(w_gate[0]).at[rank:2 * rank, GLA_QK:].set(w_gate[1]).astype(jnp.bfloat16)
    const = lambda *shape: pl.BlockSpec(shape, lambda i: (0,) * len(shape))
    row = lambda n: pl.BlockSpec((rows, n), lambda i: (i, 0))
    f32 = jnp.float32
    return pl.pallas_call(
        _gla_project_kernel,
        grid=(t_total // rows,),
        in_specs=[row(d),
                  pl.BlockSpec((1, 6, d), lambda i: (jnp.minimum(i // blocks_per_sample, n_mod - 1), 0, 0)),
                  const(d, GLA_MAIN), const(d, GLA_GATE_PAD), const(GLA_GATE_PAD, 2 * GLA_QK),
                  const(1, 2 * GLA_QK)],
        out_specs=[row(GLA_QK), row(GLA_QK), row(GLA_V), row(GLA_V), row(2 * GLA_QK)],
        out_shape=[jax.ShapeDtypeStruct((t_total, GLA_QK), f32), jax.ShapeDtypeStruct((t_total, GLA_QK), f32),
                   jax.ShapeDtypeStruct((t_total, GLA_V), f32), jax.ShapeDtypeStruct((t_total, GLA_V), f32),
                   jax.ShapeDtypeStruct((t_total, 2 * GLA_QK), f32)],
        compiler_params=pltpu.CompilerParams(dimension_semantics=("arbitrary",),
                                             vmem_limit_bytes=48 * 1024 * 1024),
    )(x, mod, w_main, w_gin, w_g2, gate_bias.reshape(1, 2 * GLA_QK))


def _gla_scan_kernel(qf_ref, kf_ref, vf_ref, gf_ref, qb_ref, kb_ref, vb_ref, gb_ref, of_ref, ob_ref,
                     state_f, state_b):
    @pl.when(pl.program_id(1) == 0)
    def _():
        state_f[...] = jnp.zeros_like(state_f)
        state_b[...] = jnp.zeros_like(state_b)

    n = GLA_CHUNK
    row = lax.broadcasted_iota(jnp.int32, (n, n), 0)
    col = lax.broadcasted_iota(jnp.int32, (n, n), 1)
    bf = jnp.bfloat16
    nt = (((1,), (1,)), ((), ()))
    tn = (((0,), (0,)), ((), ()))
    dirs = []
    for refs, backward in (((qf_ref, kf_ref, vf_ref, gf_ref, of_ref, state_f), False),
                           ((qb_ref, kb_ref, vb_ref, gb_ref, ob_ref, state_b), True)):
        q_ref, k_ref, v_ref, g_ref, o_ref, state = refs
        keep = (col >= row) if backward else (col <= row)
        b = jnp.dot(keep.astype(jnp.float32), g_ref[...], preferred_element_type=jnp.float32,
                    precision=lax.Precision.HIGHEST)
        b_end = b[0:1, :] if backward else b[n - 1:n, :]
        kk = k_ref[...]
        dirs.append(dict(keep=keep, o_ref=o_ref, state=state, decay=jnp.exp(b_end),
                         q_in=(q_ref[...] * jnp.exp(b)).astype(bf), k_in=(kk * jnp.exp(-b)).astype(bf),
                         k_end=(kk * jnp.exp(b_end - b)).astype(bf), v=v_ref[...].astype(bf)))
    heads = [(d, hd, slice(hd * GLA_DK, (hd + 1) * GLA_DK), slice(hd * GLA_DV, (hd + 1) * GLA_DV))
             for d in dirs for hd in range(GLA_HEADS)]
    att = [lax.dot_general(d["q_in"][:, ck], d["k_in"][:, ck], nt, preferred_element_type=jnp.float32)
           for d, hd, ck, cv in heads]
    att = [jnp.where(d["keep"], a, 0.0).astype(bf) for (d, hd, ck, cv), a in zip(heads, att)]
    old = [d["state"][hd] for d, hd, ck, cv in heads]
    inter = [lax.dot_general(d["q_in"][:, ck], s.astype(bf), nt, preferred_element_type=jnp.float32)
             for (d, hd, ck, cv), s in zip(heads, old)]
    intra = [jnp.dot(a, d["v"][:, cv], preferred_element_type=jnp.float32) for (d, hd, ck, cv), a in zip(heads, att)]
    vk = [lax.dot_general(d["v"][:, cv], d["k_end"][:, ck], tn, preferred_element_type=jnp.float32)
          for d, hd, ck, cv in heads]
    for (d, hd, ck, cv), o1, o2, s, u in zip(heads, intra, inter, old, vk):
        d["o_ref"][:, cv] = o1 + o2
        d["state"][hd] = d["decay"][:, ck] * s + u


def _gla_scan(q, k, v, g, *, bsz, lat_chunks, ctx_chunks):
    t_total = q.shape[0]
    n = GLA_CHUNK
    steps = ctx_chunks + lat_chunks
    ctx_base = bsz * lat_chunks

    def block(b, j, backward):
        if backward:
            c_ctx, c_lat = ctx_chunks - 1 - j, lat_chunks - 1 - (j - ctx_chunks)
        else:
            c_ctx, c_lat = j, j - ctx_chunks
        return jnp.where(j < ctx_chunks, ctx_base + b * ctx_chunks + c_ctx, b * lat_chunks + c_lat)

    def specs(backward):
        blk = functools.partial(block, backward=backward)
        return [pl.BlockSpec((n, GLA_QK), lambda b, j: (blk(b, j), 0)),
                pl.BlockSpec((n, GLA_QK), lambda b, j: (blk(b, j), 0)),
                pl.BlockSpec((n, GLA_V), lambda b, j: (blk(b, j), 0)),
                pl.BlockSpec((n, GLA_QK), lambda b, j: (blk(b, j), 1 if backward else 0))]

    out_spec = lambda backward: pl.BlockSpec((n, GLA_V), lambda b, j: (block(b, j, backward), 0))
    o_shape = jax.ShapeDtypeStruct((t_total, GLA_V), jnp.float32)
    return pl.pallas_call(
        _gla_scan_kernel,
        grid=(bsz, steps),
        in_specs=specs(False) + specs(True),
        out_specs=[out_spec(False), out_spec(True)],
        out_shape=[o_shape, o_shape],
        scratch_shapes=[pltpu.VMEM((GLA_HEADS, GLA_DV, GLA_DK), jnp.float32),
                        pltpu.VMEM((GLA_HEADS, GLA_DV, GLA_DK), jnp.float32)],
        compiler_params=pltpu.CompilerParams(dimension_semantics=("arbitrary", "arbitrary")),
    )(q, k, v, g, q, k, v, g)


def _gla_readout_kernel(x_ref, mod_ref, of_ref, ob_ref, r_ref, gn_ref, wout_ref, lng_ref, lnb_ref, out_ref, gated):
    x = x_ref[...]
    g1 = mod_ref[0, 2:3, :]
    o = of_ref[...] + ob_ref[...]
    r = r_ref[...]
    for hd in range(GLA_HEADS):
        cv = slice(hd * GLA_DV, (hd + 1) * GLA_DV)
        oh = o[:, cv]
        mu = jnp.mean(oh, axis=-1, keepdims=True)
        var = jnp.mean(jnp.square(oh - mu), axis=-1, keepdims=True)
        y = (oh - mu) * lax.rsqrt(var + LN_EPS) * gn_ref[:, cv]
        gated[:, cv] = (y * jax.nn.silu(r[:, cv])).astype(jnp.bfloat16)
    mix = jnp.dot(gated[...], wout_ref[...], preferred_element_type=jnp.float32)
    out_ref[...] = _residual_layer_norm(x, g1, mix, lng_ref[...], lnb_ref[...])


def _gla_readout(x, mod, o_f, o_b, r, gn_g, w_out, ln_g, ln_b, *, rows_per_sample):
    t_total, d = x.shape
    rows = MIXER_ROWS_PER_STEP
    n_mod = mod.shape[0]
    blocks_per_sample = rows_per_sample // rows
    const = lambda *shape: pl.BlockSpec(shape, lambda i: (0,) * len(shape))
    row = lambda n: pl.BlockSpec((rows, n), lambda i: (i, 0))
    return pl.pallas_call(
        _gla_readout_kernel,
        grid=(t_total // rows,),
        in_specs=[row(d),
                  pl.BlockSpec((1, 6, d), lambda i: (jnp.minimum(i // blocks_per_sample, n_mod - 1), 0, 0)),
                  row(GLA_V), row(GLA_V), row(GLA_V), const(1, GLA_V), const(GLA_V, d), const(1, d), const(1, d)],
        out_specs=row(d),
        out_shape=jax.ShapeDtypeStruct((t_total, d), jnp.float32),
        scratch_shapes=[pltpu.VMEM((rows, GLA_V), jnp.bfloat16)],
        compiler_params=pltpu.CompilerParams(dimension_semantics=("arbitrary",),
                                             vmem_limit_bytes=48 * 1024 * 1024),
    )(x, mod, o_f, o_b, r, gn_g.reshape(1, GLA_V), w_out.astype(jnp.bfloat16), ln_g.reshape(1, d), ln_b.reshape(1, d))


def _gla_mixer(x, mod, w_in, w_gate, gate_bias, gn_g, w_out, ln_g, ln_b, *, bsz, seq, ctx_len, n_out):
    q, k, v, r, g = _gla_project(x, mod, w_in, w_gate, gate_bias, rows_per_sample=seq)
    o_f, o_b = _gla_scan(q, k, v, g, bsz=bsz, lat_chunks=seq // GLA_CHUNK, ctx_chunks=ctx_len // GLA_CHUNK)
    return _gla_readout(x[:n_out], mod, o_f[:n_out], o_b[:n_out], r[:n_out], gn_g, w_out, ln_g, ln_b,
                        rows_per_sample=seq)


def kernel(x, c, ctx, c_ctx, w_mod, b_mod, ln_g, ln_b, a_w_in, a_norm_g, a_norm_b, a_w_s, a_b_s, a_w_out,
           b_w_in, b_w_gate, b_gate_bias, b_gn_g, b_w_out, p_w_q, p_keys, p_u, p_v):
    bsz, seq, d = x.shape
    ctx_len = ctx.shape[1]
    n_lat = bsz * seq
    xc = jnp.concatenate([x.reshape(n_lat, d), ctx.reshape(bsz * ctx_len, d)], axis=0)
    sc = jax.nn.silu(jnp.concatenate([c, c_ctx[None, :]], axis=0))
    mods = _modulation(sc, w_mod, b_mod).reshape(DEPTH, bsz + 1, 6, d)
    for i in range(DEPTH):
        last = i == DEPTH - 1
        j = i // N_MIXERS
        m = mods[i]
        n_tok = n_lat if last else xc.shape[0]
        if i % N_MIXERS == 0:
            x1 = _chunk_mlp_mixer(xc[:n_tok], m, a_w_in[j], a_norm_g[j], a_norm_b[j], a_w_s[j], a_b_s[j],
                                  a_w_out[j], ln_g[i, 0], ln_b[i, 0], rows_per_sample=seq)
        else:
            x1 = _gla_mixer(xc, m, b_w_in[j], b_w_gate[j], b_gate_bias[j], b_gn_g[j], b_w_out[j],
                            ln_g[i, 0], ln_b[i, 0], bsz=bsz, seq=seq, ctx_len=ctx_len, n_out=n_tok)
        idx, wts = _peer_route_pallas(x1, m, p_w_q[i], p_keys[i], tokens_per_sample_block=seq // 128)
        n_exp = p_u.shape[1]
        tab = jnp.concatenate([p_u[i].reshape(n_exp, SUBLANES, LANES), p_v[i].reshape(n_exp, SUBLANES, LANES)],
                              axis=1)
        xc = _peer_experts(x1, m, idx, wts, tab, ln_g[i, 1], ln_b[i, 1],
                           tokens_per_sample_block=seq // PEER_TOKENS_PER_STEP)
    return xc[:n_lat].reshape(bsz, seq, d)
```

```python
import functools

import jax
import jax.numpy as jnp
from jax import lax
from jax.experimental import pallas as pl
from jax.experimental.pallas import tpu as pltpu

D_MODEL = 1024
DEPTH = 4
N_MIXERS = 2
DN_ALPHA = (2.0 * DEPTH) ** 0.25
LN_EPS = 1e-5
CHUNK = 128
A_WIDTH = D_MODEL
A_HEADS = 8
A_GROUP = A_WIDTH // A_HEADS
GLA_HEADS = 4
GLA_DK = D_MODEL // 2 // GLA_HEADS
GLA_DV = D_MODEL // GLA_HEADS
GLA_QK = GLA_HEADS * GLA_DK
GLA_V = GLA_HEADS * GLA_DV
GLA_GATE_RANK = 16
GLA_TAU = 16.0
GLA_CHUNK = 64
PEER_HEADS = 8
PEER_NKEYS = 128
PEER_QDIM = 128
PEER_HALF = PEER_QDIM // 2
PEER_TOPK = 16
PEER_PAIRS = PEER_HEADS * PEER_TOPK

PEER_TOKENS_PER_STEP = 128
PEER_ROW_BUFFERS = 8


def _layer_norm(x, g, b):
    mu = jnp.mean(x, axis=-1, keepdims=True)
    var = jnp.mean(jnp.square(x - mu), axis=-1, keepdims=True)
    return (x - mu) * lax.rsqrt(var + LN_EPS) * g + b


SUBLANES = 8
LANES = 128


def _sublane_sums(p):
    sub = lax.broadcasted_iota(jnp.int32, (SUBLANES, LANES), 0)

    def merge(a, b, sh, members):
        mask = functools.reduce(jnp.logical_or, [sub == j for j in members])
        return jnp.where(mask, a, b) + pltpu.roll(jnp.where(mask, b, a), sh, axis=0)

    c_e = merge(p[0], p[4], 4, (0, 5, 6, 7))
    d_e = merge(p[2], p[6], 4, (0, 1, 2, 7))
    c_f = merge(p[1], p[5], 4, (0, 1, 6, 7))
    d_f = merge(p[3], p[7], 4, (0, 1, 2, 3))
    e = merge(c_e, d_e, 2, (0, 3, 4, 7))
    f = merge(c_f, d_f, 2, (0, 1, 4, 5))
    return merge(e, f, 1, (0, 2, 4, 6))


def _low_bf16(words):
    return lax.bitcast_convert_type(lax.shift_left(words, jnp.uint32(16)), jnp.float32)


def _high_bf16(words):
    return lax.bitcast_convert_type(lax.bitwise_and(words, jnp.uint32(0xFFFF0000)), jnp.float32)


def _pack_expert_table(u, v):
    n_exp = u.shape[0]
    bits = lambda t: lax.bitcast_convert_type(t.astype(jnp.bfloat16), jnp.uint16).astype(jnp.uint32)
    words = lax.shift_left(bits(v), jnp.uint32(16)) | bits(u)
    return words.reshape(n_exp, SUBLANES, LANES)


def _peer_expert_kernel(zero_ref, idx_ref, idx_next_ref, w_ref, x_ref, mod_ref, lng_ref, lnb_ref, tab_ref,
                        out_ref, *scratch, tb, nbuf):
    rows = scratch[:nbuf]
    h_scr, ffn_scr, act_scr, sems = scratch[nbuf:]
    step = pl.program_id(0)
    last = pl.num_programs(0) - 1
    npairs = rows[0].shape[1]
    ahead = nbuf - 2
    sl = SUBLANES

    def issue(src_idx_ref, t, slot):
        z = zero_ref[0]
        for k in range(npairs):
            e = src_idx_ref[t, k]
            pltpu.make_async_copy(tab_ref.at[e], rows[slot].at[z, k], sems.at[slot]).start(priority=k % 2)

    def wait(slot):
        pltpu.make_async_copy(tab_ref.at[pl.ds(0, npairs)], rows[slot].at[0], sems.at[slot]).wait()

    @pl.when(step == 0)
    def _():
        for t in range(ahead):
            issue(idx_ref, t, t)

    s2 = mod_ref[0, 4]
    sh2 = mod_ref[0, 3]
    g2 = mod_ref[0, 5]
    h_scr[...] = x_ref[...].reshape(tb, sl, LANES) * (1.0 + s2) + sh2

    def token(t, slot, w_col):
        r = rows[slot].at[0]
        h = h_scr[t]
        sums = []
        for g in range(npairs // sl):
            sums.append(_sublane_sums([_low_bf16(r[g * sl + j]) * h for j in range(sl)]))
        s = jnp.sum(jnp.concatenate(sums, axis=0), axis=-1, keepdims=True)
        act_scr[...] = jnp.broadcast_to(jax.nn.gelu(s) * w_col, (npairs, LANES))
        accs = [jnp.zeros((sl, LANES), jnp.float32) for _ in range(4)]
        for k in range(npairs):
            a_k = jnp.broadcast_to(act_scr[pl.ds(k, 1), :], (sl, LANES))
            accs[k % 4] = accs[k % 4] + a_k * _high_bf16(r[k])
        ffn_scr[t] = (accs[0] + accs[1]) + (accs[2] + accs[3])

    n_groups = tb // nbuf

    def group(j8, is_last):
        w_t = w_ref[pl.ds(pl.multiple_of(j8 * nbuf, nbuf), nbuf), :].T
        for u in range(nbuf):
            t = j8 * nbuf + u
            wait(u)
            if not is_last or u + ahead < nbuf:
                issue(idx_ref, t + ahead, (u + ahead) % nbuf)
            else:
                issue(idx_next_ref, u + ahead - nbuf, (u + ahead) % nbuf)
            token(t, u, w_t[:, u:u + 1])

    def main_body(j8, carry):
        group(j8, False)
        return carry
    lax.fori_loop(0, n_groups - 1, main_body, 0)
    group(n_groups - 1, True)

    @pl.when(step == last)
    def _():
        for t in range(ahead):
            wait(t)

    ffn = (g2 * ffn_scr[...]).reshape(tb, sl * LANES)
    out_ref[...] = _layer_norm(DN_ALPHA * x_ref[...] + ffn, lng_ref[...], lnb_ref[...])


def _peer_experts(x, mod, idx, w, tab, ln_g, ln_b, *, tokens_per_sample_block):
    t_total, d = x.shape
    npairs = idx.shape[1]
    tb = PEER_TOKENS_PER_STEP
    nbuf = PEER_ROW_BUFFERS
    sl = SUBLANES
    assert d == sl * LANES and nbuf == sl and t_total % tb == 0 and tb % nbuf == 0 and npairs % sl == 0
    n_steps = t_total // tb
    n_mod = mod.shape[0]
    kern = functools.partial(_peer_expert_kernel, tb=tb, nbuf=nbuf)
    return pl.pallas_call(
        kern,
        grid=(n_steps,),
        in_specs=[
            pl.BlockSpec(memory_space=pltpu.SMEM),
            pl.BlockSpec((tb, npairs), lambda i: (i, 0), memory_space=pltpu.SMEM),
            pl.BlockSpec((tb, npairs), lambda i: (jnp.minimum(i + 1, n_steps - 1), 0), memory_space=pltpu.SMEM),
            pl.BlockSpec((tb, npairs), lambda i: (i, 0)),
            pl.BlockSpec((tb, d), lambda i: (i, 0)),
            pl.BlockSpec((1, 6, sl, LANES),
                         lambda i: (jnp.minimum(i // tokens_per_sample_block, n_mod - 1), 0, 0, 0)),
            pl.BlockSpec((1, d), lambda i: (0, 0)),
            pl.BlockSpec((1, d), lambda i: (0, 0)),
            pl.BlockSpec(memory_space=pl.ANY),
        ],
        out_specs=pl.BlockSpec((tb, d), lambda i: (i, 0)),
        out_shape=jax.ShapeDtypeStruct((t_total, d), jnp.float32),
        scratch_shapes=[pltpu.VMEM((1, npairs, sl, LANES), jnp.uint32) for _ in range(nbuf)] + [
            pltpu.VMEM((tb, sl, LANES), jnp.float32),
            pltpu.VMEM((tb, sl, LANES), jnp.float32),
            pltpu.VMEM((npairs, LANES), jnp.float32),
            pltpu.SemaphoreType.DMA((nbuf,)),
        ],
        compiler_params=pltpu.CompilerParams(dimension_semantics=("arbitrary",),
                                             vmem_limit_bytes=40 * 1024 * 1024),
    )(jnp.zeros((1,), jnp.int32), idx, idx, w, x, mod.reshape(n_mod, 6, sl, LANES),
      ln_g.reshape(1, d), ln_b.reshape(1, d), tab)


_NEG_INF = float("-inf")


def _topk_rows(s, k):
    n, width = s.shape
    rows = lax.broadcasted_iota(jnp.int32, s.shape, 0)
    out_rows = lax.broadcasted_iota(jnp.int32, (k, width), 0)
    vals = jnp.zeros((k, width), jnp.float32)
    idxs = jnp.zeros((k, width), jnp.int32)
    for r in range(k):
        m = jnp.max(s, axis=0, keepdims=True)
        first = jnp.min(jnp.where(s == m, rows, n), axis=0, keepdims=True)
        vals = jnp.where(out_rows == r, m, vals)
        idxs = jnp.where(out_rows == r, first, idxs)
        s = jnp.where(rows == first, _NEG_INF, s)
    return vals, idxs


def _pair_candidates(s1, s2):
    k = PEER_TOPK
    sub = lax.broadcasted_iota(jnp.int32, (8, 1), 0)
    groups = []

    def add(vals, a, b, keep):
        ok = keep & ((a + 1) * (b + 1) <= k)
        groups.append((jnp.where(ok, vals, _NEG_INF), a * k + b))

    for b0 in (0, 8):
        add(s1[0:1, :] + s2[b0:b0 + 8, :], 0 * sub, b0 + sub, sub >= 0)
    add(s1[8:16, :] + s2[0:1, :], 8 + sub, 0 * sub, sub >= 0)
    for b in range(5):
        add(s1[0:8, :] + s2[b:b + 1, :], sub, b + 0 * sub, sub >= 1)
    add(s1[1:2, :] + s2[0:8, :], 1 + 0 * sub, sub, sub >= 5)
    return groups


def _topk_groups(groups, k):
    width = groups[0][0].shape[1]
    big = PEER_TOPK * PEER_TOPK
    out_rows = lax.broadcasted_iota(jnp.int32, (k, width), 0)
    vals = jnp.zeros((k, width), jnp.float32)
    poss = jnp.zeros((k, width), jnp.int32)
    gv = [g[0] for g in groups]
    gp = [g[1] for g in groups]
    for r in range(k):
        m = functools.reduce(jnp.maximum, gv)
        m = jnp.max(m, axis=0, keepdims=True)
        f = functools.reduce(jnp.minimum, [jnp.where(v == m, p, big) for v, p in zip(gv, gp)])
        f = jnp.min(f, axis=0, keepdims=True)
        vals = jnp.where(out_rows == r, m, vals)
        poss = jnp.where(out_rows == r, f, poss)
        gv = [jnp.where(p == f, _NEG_INF, v) for v, p in zip(gv, gp)]
    return vals, poss


def _select_rows(table, sel):
    out = jnp.zeros_like(table)
    for a in range(table.shape[0]):
        out = jnp.where(sel == a, table[a:a + 1, :], out)
    return out


def _peer_route_kernel(x_ref, mod_ref, wq_ref, keys_ref, idx_ref, w_ref, idx_t, w_t):
    k = PEER_TOPK
    s2 = mod_ref[0, 4:5, :]
    sh2 = mod_ref[0, 3:4, :]
    h = (x_ref[...] * (1.0 + s2) + sh2).astype(jnp.bfloat16)

    def head(hd, carry):
        q = jnp.dot(h, wq_ref[hd], preferred_element_type=jnp.float32).astype(jnp.bfloat16)
        nt = (((1,), (1,)), ((), ()))
        sc1 = lax.dot_general(keys_ref[0], q, nt, preferred_element_type=jnp.float32)
        sc2 = lax.dot_general(keys_ref[1], q, nt, preferred_element_type=jnp.float32)
        v1, i1 = _topk_rows(sc1, k)
        v2, i2 = _topk_rows(sc2, k)
        top_s, pos = _topk_groups(_pair_candidates(v1, v2), k)
        e = jnp.exp(top_s - top_s[0:1, :])
        wts = e / jnp.sum(e, axis=0, keepdims=True)
        a = lax.shift_right_logical(pos, 4)
        b = lax.bitwise_and(pos, k - 1)
        ids = _select_rows(i1, a) * PEER_NKEYS + _select_rows(i2, b)
        row0 = pl.multiple_of(hd * k, k)
        idx_t[pl.ds(row0, k), :] = ids
        w_t[pl.ds(row0, k), :] = wts
        return carry

    lax.fori_loop(0, PEER_HEADS, head, 0, unroll=4)
    idx_ref[...] = idx_t[...].T
    w_ref[...] = w_t[...].T


def _peer_route_pallas(x, mod, w_q, sub_keys, *, tokens_per_sample_block):
    t_total, d = x.shape
    tb = 128
    n_steps = t_total // tb
    n_mod = mod.shape[0]
    wq = w_q.reshape(d, PEER_HEADS, PEER_QDIM).transpose(1, 0, 2).astype(jnp.bfloat16)
    zeros = jnp.zeros((PEER_NKEYS, PEER_HALF), sub_keys.dtype)
    keys = jnp.stack([jnp.concatenate([sub_keys[0], zeros], axis=1),
                      jnp.concatenate([zeros, sub_keys[1]], axis=1)]).astype(jnp.bfloat16)
    return pl.pallas_call(
        _peer_route_kernel,
        grid=(n_steps,),
        in_specs=[
            pl.BlockSpec((tb, d), lambda i: (i, 0)),
            pl.BlockSpec((1, 6, d), lambda i: (jnp.minimum(i // tokens_per_sample_block, n_mod - 1), 0, 0)),
            pl.BlockSpec((PEER_HEADS, d, PEER_QDIM), lambda i: (0, 0, 0)),
            pl.BlockSpec((2, PEER_NKEYS, PEER_QDIM), lambda i: (0, 0, 0)),
        ],
        out_specs=[pl.BlockSpec((tb, PEER_PAIRS), lambda i: (i, 0)),
                   pl.BlockSpec((tb, PEER_PAIRS), lambda i: (i, 0))],
        out_shape=[jax.ShapeDtypeStruct((t_total, PEER_PAIRS), jnp.int32),
                   jax.ShapeDtypeStruct((t_total, PEER_PAIRS), jnp.float32)],
        scratch_shapes=[pltpu.VMEM((PEER_PAIRS, tb), jnp.int32), pltpu.VMEM((PEER_PAIRS, tb), jnp.float32)],
        compiler_params=pltpu.CompilerParams(dimension_semantics=("arbitrary",)),
    )(x, mod, wq, keys)


MOD_COLS_PER_STEP = 1536


def _modulation_kernel(sc_ref, w_ref, b_ref, out_ref):
    acc = jnp.dot(sc_ref[...].astype(jnp.bfloat16), w_ref[0].astype(jnp.bfloat16),
                  preferred_element_type=jnp.float32)
    out_ref[0] = acc + b_ref[0]


def _modulation(sc, w_mod, b_mod):
    n_layers, d, n_out = w_mod.shape
    s = sc.shape[0]
    tn = MOD_COLS_PER_STEP
    return pl.pallas_call(
        _modulation_kernel,
        grid=(n_layers, n_out // tn),
        in_specs=[pl.BlockSpec((s, d), lambda l, j: (0, 0)),
                  pl.BlockSpec((1, d, tn), lambda l, j: (l, 0, j)),
                  pl.BlockSpec((1, 1, tn), lambda l, j: (l, 0, j))],
        out_specs=pl.BlockSpec((1, s, tn), lambda l, j: (l, 0, j)),
        out_shape=jax.ShapeDtypeStruct((n_layers, s, n_out), jnp.float32),
        compiler_params=pltpu.CompilerParams(dimension_semantics=("arbitrary", "arbitrary"),
                                             vmem_limit_bytes=40 * 1024 * 1024),
    )(sc, w_mod, b_mod.reshape(n_layers, 1, n_out))


def _residual_layer_norm(x, g_mod, mix, ln_g, ln_b):
    y = DN_ALPHA * x + g_mod * mix
    return _layer_norm(y, ln_g, ln_b)


MIXER_ROWS_PER_STEP = 256


def _chunk_mlp_kernel(x_ref, mod_ref, win_ref, ng_ref, nb_ref, ws_ref, bs_ref, wout_ref, lng_ref, lnb_ref,
                      out_ref, gated):
    x = x_ref[...]
    sh1 = mod_ref[0, 0:1, :]
    s1 = mod_ref[0, 1:2, :]
    g1 = mod_ref[0, 2:3, :]
    h = (x * (1.0 + s1) + sh1).astype(jnp.bfloat16)
    z = jax.nn.gelu(jnp.dot(h, win_ref[...], preferred_element_type=jnp.float32))
    u = z[:, :A_WIDTH]
    v = _layer_norm(z[:, A_WIDTH:], ng_ref[...], nb_ref[...]).astype(jnp.bfloat16)
    for c in range(x.shape[0] // CHUNK):
        r0 = c * CHUNK
        for hd in range(A_HEADS):
            c0 = hd * A_GROUP
            s = jnp.dot(ws_ref[hd], v[r0:r0 + CHUNK, c0:c0 + A_GROUP], preferred_element_type=jnp.float32)
            s = s + bs_ref[:, hd:hd + 1]
            gated[r0:r0 + CHUNK, c0:c0 + A_GROUP] = (u[r0:r0 + CHUNK, c0:c0 + A_GROUP] * s).astype(jnp.bfloat16)
    mix = jnp.dot(gated[...], wout_ref[...], preferred_element_type=jnp.float32)
    out_ref[...] = _residual_layer_norm(x, g1, mix, lng_ref[...], lnb_ref[...])


def _chunk_mlp_mixer(x, mod, w_in, norm_g, norm_b, w_s, b_s, w_out, ln_g, ln_b, *, rows_per_sample):
    t_total, d = x.shape
    rows = MIXER_ROWS_PER_STEP
    n_mod = mod.shape[0]
    blocks_per_sample = rows_per_sample // rows
    const = lambda *shape: pl.BlockSpec(shape, lambda i: (0,) * len(shape))
    return pl.pallas_call(
        _chunk_mlp_kernel,
        grid=(t_total // rows,),
        in_specs=[pl.BlockSpec((rows, d), lambda i: (i, 0)),
                  pl.BlockSpec((1, 6, d), lambda i: (jnp.minimum(i // blocks_per_sample, n_mod - 1), 0, 0)),
                  const(d, 2 * A_WIDTH), const(1, A_WIDTH), const(1, A_WIDTH),
                  const(A_HEADS, CHUNK, CHUNK), const(CHUNK, A_HEADS), const(A_WIDTH, d),
                  const(1, d), const(1, d)],
        out_specs=pl.BlockSpec((rows, d), lambda i: (i, 0)),
        out_shape=jax.ShapeDtypeStruct((t_total, d), jnp.float32),
        scratch_shapes=[pltpu.VMEM((rows, A_WIDTH), jnp.bfloat16)],
        compiler_params=pltpu.CompilerParams(dimension_semantics=("arbitrary",),
                                             vmem_limit_bytes=48 * 1024 * 1024),
    )(x, mod, w_in.astype(jnp.bfloat16), norm_g.reshape(1, -1), norm_b.reshape(1, -1),
      w_s.astype(jnp.bfloat16), b_s.T, w_out.astype(jnp.bfloat16), ln_g.reshape(1, d), ln_b.reshape(1, d))


GLA_MAIN = 2 * GLA_QK + 2 * GLA_V
GLA_GATE_PAD = 128


def _gla_project_kernel(x_ref, mod_ref, win_ref, wgin_ref, wg_ref, gb_ref, q_ref, k_ref, v_ref, r_ref, g_ref):
    sh1 = mod_ref[0, 0:1, :]
    s1 = mod_ref[0, 1:2, :]
    h = (x_ref[...] * (1.0 + s1) + sh1).astype(jnp.bfloat16)
    z = jnp.dot(h, win_ref[...], preferred_element_type=jnp.float32)
    q_ref[...] = z[:, :GLA_QK] * (GLA_DK ** -0.5)
    k_ref[...] = z[:, GLA_QK:2 * GLA_QK]
    v_ref[...] = z[:, 2 * GLA_QK:2 * GLA_QK + GLA_V]
    r_ref[...] = z[:, 2 * GLA_QK + GLA_V:]
    gl = jnp.dot(h, wgin_ref[...], preferred_element_type=jnp.float32).astype(jnp.bfloat16)
    zg = jnp.dot(gl, wg_ref[...], preferred_element_type=jnp.float32) + gb_ref[...]
    g_ref[...] = jax.nn.log_sigmoid(zg) / GLA_TAU


def _gla_project(x, mod, w_in, w_gate, gate_bias, *, rows_per_sample):
    t_total, d = x.shape
    rows = MIXER_ROWS_PER_STEP
    n_mod = mod.shape[0]
    blocks_per_sample = rows_per_sample // rows
    rank = GLA_GATE_RANK
    w_main = w_in[:, :GLA_MAIN].astype(jnp.bfloat16)
    w_gin = jnp.pad(w_in[:, GLA_MAIN:], ((0, 0), (0, GLA_GATE_PAD - 2 * rank))).astype(jnp.bfloat16)
    w_g2 = jnp.zeros((GLA_GATE_PAD, 2 * GLA_QK), jnp.float32)
    w_g2 = w_g2.at[:rank, :GLA_QK].set(w_gate[0]).at[rank:2 * rank, GLA_QK:].set(w_gate[1]).astype(jnp.bfloat16)
    const = lambda *shape: pl.BlockSpec(shape, lambda i: (0,) * len(shape))
    row = lambda n: pl.BlockSpec((rows, n), lambda i: (i, 0))
    f32 = jnp.float32
    return pl.pallas_call(
        _gla_project_kernel,
        grid=(t_total // rows,),
        in_specs=[row(d),
                  pl.BlockSpec((1, 6, d), lambda i: (jnp.minimum(i // blocks_per_sample, n_mod - 1), 0, 0)),
                  const(d, GLA_MAIN), const(d, GLA_GATE_PAD), const(GLA_GATE_PAD, 2 * GLA_QK),
                  const(1, 2 * GLA_QK)],
        out_specs=[row(GLA_QK), row(GLA_QK), row(GLA_V), row(GLA_V), row(2 * GLA_QK)],
        out_shape=[jax.ShapeDtypeStruct((t_total, GLA_QK), f32), jax.ShapeDtypeStruct((t_total, GLA_QK), f32),
                   jax.ShapeDtypeStruct((t_total, GLA_V), f32), jax.ShapeDtypeStruct((t_total, GLA_V), f32),
                   jax.ShapeDtypeStruct((t_total, 2 * GLA_QK), f32)],
        compiler_params=pltpu.CompilerParams(dimension_semantics=("arbitrary",),
                                             vmem_limit_bytes=48 * 1024 * 1024),
    )(x, mod, w_main, w_gin, w_g2, gate_bias.reshape(1, 2 * GLA_QK))


def _gla_scan_kernel(qf_ref, kf_ref, vf_ref, gf_ref, qb_ref, kb_ref, vb_ref, gb_ref, of_ref, ob_ref,
                     state_f, state_b):
    @pl.when(pl.program_id(1) == 0)
    def _():
        state_f[...] = jnp.zeros_like(state_f)
        state_b[...] = jnp.zeros_like(state_b)

    n = GLA_CHUNK
    row = lax.broadcasted_iota(jnp.int32, (n, n), 0)
    col = lax.broadcasted_iota(jnp.int32, (n, n), 1)
    bf = jnp.bfloat16
    nt = (((1,), (1,)), ((), ()))
    tn = (((0,), (0,)), ((), ()))
    dirs = []
    for refs, backward in (((qf_ref, kf_ref, vf_ref, gf_ref, of_ref, state_f), False),
                           ((qb_ref, kb_ref, vb_ref, gb_ref, ob_ref, state_b), True)):
        q_ref, k_ref, v_ref, g_ref, o_ref, state = refs
        keep = (col >= row) if backward else (col <= row)
        b = jnp.dot(keep.astype(jnp.float32), g_ref[...], preferred_element_type=jnp.float32,
                    precision=lax.Precision.HIGHEST)
        b_end = b[0:1, :] if backward else b[n - 1:n, :]
        kk = k_ref[...]
        dirs.append(dict(keep=keep, o_ref=o_ref, state=state, decay=jnp.exp(b_end),
                         q_in=(q_ref[...] * jnp.exp(b)).astype(bf), k_in=(kk * jnp.exp(-b)).astype(bf),
                         k_end=(kk * jnp.exp(b_end - b)).astype(bf), v=v_ref[...].astype(bf)))
    heads = [(d, hd, slice(hd * GLA_DK, (hd + 1) * GLA_DK), slice(hd * GLA_DV, (hd + 1) * GLA_DV))
             for d in dirs for hd in range(GLA_HEADS)]
    att = [lax.dot_general(d["q_in"][:, ck], d["k_in"][:, ck], nt, preferred_element_type=jnp.float32)
           for d, hd, ck, cv in heads]
    att = [jnp.where(d["keep"], a, 0.0).astype(bf) for (d, hd, ck, cv), a in zip(heads, att)]
    old = [d["state"][hd] for d, hd, ck, cv in heads]
    inter = [lax.dot_general(d["q_in"][:, ck], s.astype(bf), nt, preferred_element_type=jnp.float32)
             for (d, hd, ck, cv), s in zip(heads, old)]
    intra = [jnp.dot(a, d["v"][:, cv], preferred_element_type=jnp.float32) for (d, hd, ck, cv), a in zip(heads, att)]
    vk = [lax.dot_general(d["v"][:, cv], d["k_end"][:, ck], tn, preferred_element_type=jnp.float32)
          for d, hd, ck, cv in heads]
    for (d, hd, ck, cv), o1, o2, s, u in zip(heads, intra, inter, old, vk):
        d["o_ref"][:, cv] = o1 + o2
        d["state"][hd] = d["decay"][:, ck] * s + u


def _gla_scan(q, k, v, g, *, bsz, lat_chunks, ctx_chunks):
    t_total = q.shape[0]
    n = GLA_CHUNK
    steps = ctx_chunks + lat_chunks
    ctx_base = bsz * lat_chunks

    def block(b, j, backward):
        if backward:
            c_ctx, c_lat = ctx_chunks - 1 - j, lat_chunks - 1 - (j - ctx_chunks)
        else:
            c_ctx, c_lat = j, j - ctx_chunks
        return jnp.where(j < ctx_chunks, ctx_base + b * ctx_chunks + c_ctx, b * lat_chunks + c_lat)

    def specs(backward):
        blk = functools.partial(block, backward=backward)
        return [pl.BlockSpec((n, GLA_QK), lambda b, j: (blk(b, j), 0)),
                pl.BlockSpec((n, GLA_QK), lambda b, j: (blk(b, j), 0)),
                pl.BlockSpec((n, GLA_V), lambda b, j: (blk(b, j), 0)),
                pl.BlockSpec((n, GLA_QK), lambda b, j: (blk(b, j), 1 if backward else 0))]

    out_spec = lambda backward: pl.BlockSpec((n, GLA_V), lambda b, j: (block(b, j, backward), 0))
    o_shape = jax.ShapeDtypeStruct((t_total, GLA_V), jnp.float32)
    return pl.pallas_call(
        _gla_scan_kernel,
        grid=(bsz, steps),
        in_specs=specs(False) + specs(True),
        out_specs=[out_spec(False), out_spec(True)],
        out_shape=[o_shape, o_shape],
        scratch_shapes=[pltpu.VMEM((GLA_HEADS, GLA_DV, GLA_DK), jnp.float32),
                        pltpu.VMEM((GLA_HEADS, GLA_DV, GLA_DK), jnp.float32)],
        compiler_params=pltpu.CompilerParams(dimension_semantics=("arbitrary", "arbitrary")),
    )(q, k, v, g, q, k, v, g)


def _gla_readout_kernel(x_ref, mod_ref, of_ref, ob_ref, r_ref, gn_ref, wout_ref, lng_ref, lnb_ref, out_ref, gated):
    x = x_ref[...]
    g1 = mod_ref[0, 2:3, :]
    o = of_ref[...] + ob_ref[...]
    r = r_ref[...]
    for hd in range(GLA_HEADS):
        cv = slice(hd * GLA_DV, (hd + 1) * GLA_DV)
        oh = o[:, cv]
        mu = jnp.mean(oh, axis=-1, keepdims=True)
        var = jnp.mean(jnp.square(oh - mu), axis=-1, keepdims=True)
        y = (oh - mu) * lax.rsqrt(var + LN_EPS) * gn_ref[:, cv]
        gated[:, cv] = (y * jax.nn.silu(r[:, cv])).astype(jnp.bfloat16)
    mix = jnp.dot(gated[...], wout_ref[...], preferred_element_type=jnp.float32)
    out_ref[...] = _residual_layer_norm(x, g1, mix, lng_ref[...], lnb_ref[...])


def _gla_readout(x, mod, o_f, o_b, r, gn_g, w_out, ln_g, ln_b, *, rows_per_sample):
    t_total, d = x.shape
    rows = MIXER_ROWS_PER_STEP
    n_mod = mod.shape[0]
    blocks_per_sample = rows_per_sample // rows
    const = lambda *shape: pl.BlockSpec(shape, lambda i: (0,) * len(shape))
    row = lambda n: pl.BlockSpec((rows, n), lambda i: (i, 0))
    return pl.pallas_call(
        _gla_readout_kernel,
        grid=(t_total // rows,),
        in_specs=[row(d),
                  pl.BlockSpec((1, 6, d), lambda i: (jnp.minimum(i // blocks_per_sample, n_mod - 1), 0, 0)),
                  row(GLA_V), row(GLA_V), row(GLA_V), const(1, GLA_V), const(GLA_V, d), const(1, d), const(1, d)],
        out_specs=row(d),
        out_shape=jax.ShapeDtypeStruct((t_total, d), jnp.float32),
        scratch_shapes=[pltpu.VMEM((rows, GLA_V), jnp.bfloat16)],
        compiler_params=pltpu.CompilerParams(dimension_semantics=("arbitrary",),
                                             vmem_limit_bytes=48 * 1024 * 1024),
    )(x, mod, o_f, o_b, r, gn_g.reshape(1, GLA_V), w_out.astype(jnp.bfloat16), ln_g.reshape(1, d), ln_b.reshape(1, d))


def _gla_mixer(x, mod, w_in, w_gate, gate_bias, gn_g, w_out, ln_g, ln_b, *, bsz, seq, ctx_len, n_out):
    q, k, v, r, g = _gla_project(x, mod, w_in, w_gate, gate_bias, rows_per_sample=seq)
    o_f, o_b = _gla_scan(q, k, v, g, bsz=bsz, lat_chunks=seq // GLA_CHUNK, ctx_chunks=ctx_len // GLA_CHUNK)
    return _gla_readout(x[:n_out], mod, o_f[:n_out], o_b[:n_out], r[:n_out], gn_g, w_out, ln_g, ln_b,
                        rows_per_sample=seq)


def kernel(x, c, ctx, c_ctx, w_mod, b_mod, ln_g, ln_b, a_w_in, a_norm_g, a_norm_b, a_w_s, a_b_s, a_w_out,
           b_w_in, b_w_gate, b_gate_bias, b_gn_g, b_w_out, p_w_q, p_keys, p_u, p_v):
    bsz, seq, d = x.shape
    ctx_len = ctx.shape[1]
    n_lat = bsz * seq
    xc = jnp.concatenate([x.reshape(n_lat, d), ctx.reshape(bsz * ctx_len, d)], axis=0)
    sc = jax.nn.silu(jnp.concatenate([c, c_ctx[None, :]], axis=0))
    mods = _modulation(sc, w_mod, b_mod).reshape(DEPTH, bsz + 1, 6, d)
    for i in range(DEPTH):
        last = i == DEPTH - 1
        j = i // N_MIXERS
        m = mods[i]
        n_tok = n_lat if last else xc.shape[0]
        if i % N_MIXERS == 0:
            x1 = _chunk_mlp_mixer(xc[:n_tok], m, a_w_in[j], a_norm_g[j], a_norm_b[j], a_w_s[j], a_b_s[j],
                                  a_w_out[j], ln_g[i, 0], ln_b[i, 0], rows_per_sample=seq)
        else:
            x1 = _gla_mixer(xc, m, b_w_in[j], b_w_gate[j], b_gate_bias[j], b_gn_g[j], b_w_out[j],
                            ln_g[i, 0], ln_b[i, 0], bsz=bsz, seq=seq, ctx_len=ctx_len, n_out=n_tok)
        idx, wts = _peer_route_pallas(x1, m, p_w_q[i], p_keys[i], tokens_per_sample_block=seq // 128)
        tab = _pack_expert_table(p_u[i], p_v[i])
        xc = _peer_experts(x1, m, idx, wts, tab, ln_g[i, 1], ln_b[i, 1],
                           tokens_per_sample_block=seq // PEER_TOKENS_PER_STEP)
    return xc[:n_lat].reshape(bsz, seq, d)
```

```python
import functools

import jax
import jax.numpy as jnp
from jax import lax
from jax.experimental import pallas as pl
from jax.experimental.pallas import tpu as pltpu

D_MODEL = 1024
DEPTH = 4
N_MIXERS = 2
DN_ALPHA = (2.0 * DEPTH) ** 0.25
LN_EPS = 1e-5
CHUNK = 128
A_WIDTH = D_MODEL
A_HEADS = 8
A_GROUP = A_WIDTH // A_HEADS
GLA_HEADS = 4
GLA_DK = D_MODEL // 2 // GLA_HEADS
GLA_DV = D_MODEL // GLA_HEADS
GLA_QK = GLA_HEADS * GLA_DK
GLA_V = GLA_HEADS * GLA_DV
GLA_GATE_RANK = 16
GLA_TAU = 16.0
GLA_CHUNK = 64
PEER_HEADS = 8
PEER_NKEYS = 128
PEER_QDIM = 128
PEER_HALF = PEER_QDIM // 2
PEER_TOPK = 16
PEER_PAIRS = PEER_HEADS * PEER_TOPK

PEER_TOKENS_PER_STEP = 256
PEER_ROW_BUFFERS = 8


def _layer_norm(x, g, b):
    mu = jnp.mean(x, axis=-1, keepdims=True)
    var = jnp.mean(jnp.square(x - mu), axis=-1, keepdims=True)
    return (x - mu) * lax.rsqrt(var + LN_EPS) * g + b


SUBLANES = 8
LANES = 128


def _sublane_sums(p):
    sub = lax.broadcasted_iota(jnp.int32, (SUBLANES, LANES), 0)

    def merge(a, b, sh, members):
        mask = functools.reduce(jnp.logical_or, [sub == j for j in members])
        return jnp.where(mask, a, b) + pltpu.roll(jnp.where(mask, b, a), sh, axis=0)

    c_e = merge(p[0], p[4], 4, (0, 5, 6, 7))
    d_e = merge(p[2], p[6], 4, (0, 1, 2, 7))
    c_f = merge(p[1], p[5], 4, (0, 1, 6, 7))
    d_f = merge(p[3], p[7], 4, (0, 1, 2, 3))
    e = merge(c_e, d_e, 2, (0, 3, 4, 7))
    f = merge(c_f, d_f, 2, (0, 1, 4, 5))
    return merge(e, f, 1, (0, 2, 4, 6))


def _peer_expert_kernel(zero_ref, idx_ref, idx_next_ref, w_ref, x_ref, mod_ref, lng_ref, lnb_ref, tab_ref,
                        out_ref, *scratch, tb, nbuf):
    rows = scratch[:nbuf]
    h_scr, ffn_scr, act_scr, sems = scratch[nbuf:]
    step = pl.program_id(0)
    last = pl.num_programs(0) - 1
    npairs = rows[0].shape[1]
    ahead = nbuf - 2
    sl = SUBLANES

    def issue(src_idx_ref, t, slot):
        z = zero_ref[0]
        for k in range(npairs):
            e = src_idx_ref[t, k]
            pltpu.make_async_copy(tab_ref.at[e], rows[slot].at[z, k], sems.at[slot]).start(priority=k % 2)

    def wait(slot):
        pltpu.make_async_copy(tab_ref.at[pl.ds(0, npairs)], rows[slot].at[0], sems.at[slot]).wait()

    @pl.when(step == 0)
    def _():
        for t in range(ahead):
            issue(idx_ref, t, t)

    s2 = mod_ref[0, 4]
    sh2 = mod_ref[0, 3]
    g2 = mod_ref[0, 5]
    h_scr[...] = x_ref[...].reshape(tb, sl, LANES) * (1.0 + s2) + sh2

    def token(t, slot, w_col):
        r = rows[slot].at[0]
        h = h_scr[t]
        sums = []
        for g in range(npairs // sl):
            sums.append(_sublane_sums([r[g * sl + j, 0:sl, :] * h for j in range(sl)]))
        s = jnp.sum(jnp.concatenate(sums, axis=0), axis=-1, keepdims=True)
        act_scr[...] = jnp.broadcast_to(jax.nn.gelu(s) * w_col, (npairs, LANES))
        accs = [jnp.zeros((sl, LANES), jnp.float32) for _ in range(4)]
        for k in range(npairs):
            a_k = jnp.broadcast_to(act_scr[pl.ds(k, 1), :], (sl, LANES))
            accs[k % 4] = accs[k % 4] + a_k * r[k, sl:2 * sl, :]
        ffn_scr[t] = (accs[0] + accs[1]) + (accs[2] + accs[3])

    n_groups = tb // nbuf

    def group(j8, is_last):
        w_t = w_ref[pl.ds(pl.multiple_of(j8 * nbuf, nbuf), nbuf), :].T
        for u in range(nbuf):
            t = j8 * nbuf + u
            wait(u)
            if not is_last or u + ahead < nbuf:
                issue(idx_ref, t + ahead, (u + ahead) % nbuf)
            else:
                issue(idx_next_ref, u + ahead - nbuf, (u + ahead) % nbuf)
            token(t, u, w_t[:, u:u + 1])

    def main_body(j8, carry):
        group(j8, False)
        return carry
    lax.fori_loop(0, n_groups - 1, main_body, 0)
    group(n_groups - 1, True)

    @pl.when(step == last)
    def _():
        for t in range(ahead):
            wait(t)

    ffn = (g2 * ffn_scr[...]).reshape(tb, sl * LANES)
    out_ref[...] = _layer_norm(DN_ALPHA * x_ref[...] + ffn, lng_ref[...], lnb_ref[...])


def _peer_experts(x, mod, idx, w, tab, ln_g, ln_b, *, tokens_per_sample_block):
    t_total, d = x.shape
    npairs = idx.shape[1]
    tb = PEER_TOKENS_PER_STEP
    nbuf = PEER_ROW_BUFFERS
    sl = SUBLANES
    assert d == sl * LANES and nbuf == sl and t_total % tb == 0 and tb % nbuf == 0 and npairs % sl == 0
    n_steps = t_total // tb
    n_mod = mod.shape[0]
    kern = functools.partial(_peer_expert_kernel, tb=tb, nbuf=nbuf)
    return pl.pallas_call(
        kern,
        grid=(n_steps,),
        in_specs=[
            pl.BlockSpec(memory_space=pltpu.SMEM),
            pl.BlockSpec((tb, npairs), lambda i: (i, 0), memory_space=pltpu.SMEM),
            pl.BlockSpec((tb, npairs), lambda i: (jnp.minimum(i + 1, n_steps - 1), 0), memory_space=pltpu.SMEM),
            pl.BlockSpec((tb, npairs), lambda i: (i, 0)),
            pl.BlockSpec((tb, d), lambda i: (i, 0)),
            pl.BlockSpec((1, 6, sl, LANES),
                         lambda i: (jnp.minimum(i // tokens_per_sample_block, n_mod - 1), 0, 0, 0)),
            pl.BlockSpec((1, d), lambda i: (0, 0)),
            pl.BlockSpec((1, d), lambda i: (0, 0)),
            pl.BlockSpec(memory_space=pl.ANY),
        ],
        out_specs=pl.BlockSpec((tb, d), lambda i: (i, 0)),
        out_shape=jax.ShapeDtypeStruct((t_total, d), jnp.float32),
        scratch_shapes=[pltpu.VMEM((1, npairs, 2 * sl, LANES), jnp.float32) for _ in range(nbuf)] + [
            pltpu.VMEM((tb, sl, LANES), jnp.float32),
            pltpu.VMEM((tb, sl, LANES), jnp.float32),
            pltpu.VMEM((npairs, LANES), jnp.float32),
            pltpu.SemaphoreType.DMA((nbuf,)),
        ],
        compiler_params=pltpu.CompilerParams(dimension_semantics=("arbitrary",),
                                             vmem_limit_bytes=40 * 1024 * 1024),
    )(jnp.zeros((1,), jnp.int32), idx, idx, w, x, mod.reshape(n_mod, 6, sl, LANES),
      ln_g.reshape(1, d), ln_b.reshape(1, d), tab)


_NEG_INF = float("-inf")


def _topk_rows(s, k):
    n, width = s.shape
    rows = lax.broadcasted_iota(jnp.int32, s.shape, 0)
    out_rows = lax.broadcasted_iota(jnp.int32, (k, width), 0)
    vals = jnp.zeros((k, width), jnp.float32)
    idxs = jnp.zeros((k, width), jnp.int32)
    for r in range(k):
        m = jnp.max(s, axis=0, keepdims=True)
        first = jnp.min(jnp.where(s == m, rows, n), axis=0, keepdims=True)
        vals = jnp.where(out_rows == r, m, vals)
        idxs = jnp.where(out_rows == r, first, idxs)
        s = jnp.where(rows == first, _NEG_INF, s)
    return vals, idxs


def _pair_candidates(s1, s2):
    k = PEER_TOPK
    sub = lax.broadcasted_iota(jnp.int32, (8, 1), 0)
    groups = []

    def add(vals, a, b, keep):
        ok = keep & ((a + 1) * (b + 1) <= k)
        groups.append((jnp.where(ok, vals, _NEG_INF), a * k + b))

    for b0 in (0, 8):
        add(s1[0:1, :] + s2[b0:b0 + 8, :], 0 * sub, b0 + sub, sub >= 0)
    add(s1[8:16, :] + s2[0:1, :], 8 + sub, 0 * sub, sub >= 0)
    for b in range(5):
        add(s1[0:8, :] + s2[b:b + 1, :], sub, b + 0 * sub, sub >= 1)
    add(s1[1:2, :] + s2[0:8, :], 1 + 0 * sub, sub, sub >= 5)
    return groups


def _topk_groups(groups, k):
    width = groups[0][0].shape[1]
    big = PEER_TOPK * PEER_TOPK
    out_rows = lax.broadcasted_iota(jnp.int32, (k, width), 0)
    vals = jnp.zeros((k, width), jnp.float32)
    poss = jnp.zeros((k, width), jnp.int32)
    gv = [g[0] for g in groups]
    gp = [g[1] for g in groups]
    for r in range(k):
        m = functools.reduce(jnp.maximum, gv)
        m = jnp.max(m, axis=0, keepdims=True)
        f = functools.reduce(jnp.minimum, [jnp.where(v == m, p, big) for v, p in zip(gv, gp)])
        f = jnp.min(f, axis=0, keepdims=True)
        vals = jnp.where(out_rows == r, m, vals)
        poss = jnp.where(out_rows == r, f, poss)
        gv = [jnp.where(p == f, _NEG_INF, v) for v, p in zip(gv, gp)]
    return vals, poss


def _select_rows(table, sel):
    out = jnp.zeros_like(table)
    for a in range(table.shape[0]):
        out = jnp.where(sel == a, table[a:a + 1, :], out)
    return out


def _peer_route_kernel(x_ref, mod_ref, wq_ref, keys_ref, idx_ref, w_ref, idx_t, w_t):
    k = PEER_TOPK
    s2 = mod_ref[0, 4:5, :]
    sh2 = mod_ref[0, 3:4, :]
    h = (x_ref[...] * (1.0 + s2) + sh2).astype(jnp.bfloat16)

    def head(hd, carry):
        q = jnp.dot(h, wq_ref[hd], preferred_element_type=jnp.float32).astype(jnp.bfloat16)
        nt = (((1,), (1,)), ((), ()))
        sc1 = lax.dot_general(keys_ref[0], q, nt, preferred_element_type=jnp.float32)
        sc2 = lax.dot_general(keys_ref[1], q, nt, preferred_element_type=jnp.float32)
        v1, i1 = _topk_rows(sc1, k)
        v2, i2 = _topk_rows(sc2, k)
        top_s, pos = _topk_groups(_pair_candidates(v1, v2), k)
        e = jnp.exp(top_s - top_s[0:1, :])
        wts = e / jnp.sum(e, axis=0, keepdims=True)
        a = lax.shift_right_logical(pos, 4)
        b = lax.bitwise_and(pos, k - 1)
        ids = _select_rows(i1, a) * PEER_NKEYS + _select_rows(i2, b)
        row0 = pl.multiple_of(hd * k, k)
        idx_t[pl.ds(row0, k), :] = ids
        w_t[pl.ds(row0, k), :] = wts
        return carry

    lax.fori_loop(0, PEER_HEADS, head, 0, unroll=4)
    idx_ref[...] = idx_t[...].T
    w_ref[...] = w_t[...].T


def _peer_route_pallas(x, mod, w_q, sub_keys, *, tokens_per_sample_block):
    t_total, d = x.shape
    tb = 128
    n_steps = t_total // tb
    n_mod = mod.shape[0]
    wq = w_q.reshape(d, PEER_HEADS, PEER_QDIM).transpose(1, 0, 2).astype(jnp.bfloat16)
    zeros = jnp.zeros((PEER_NKEYS, PEER_HALF), sub_keys.dtype)
    keys = jnp.stack([jnp.concatenate([sub_keys[0], zeros], axis=1),
                      jnp.concatenate([zeros, sub_keys[1]], axis=1)]).astype(jnp.bfloat16)
    return pl.pallas_call(
        _peer_route_kernel,
        grid=(n_steps,),
        in_specs=[
            pl.BlockSpec((tb, d), lambda i: (i, 0)),
            pl.BlockSpec((1, 6, d), lambda i: (jnp.minimum(i // tokens_per_sample_block, n_mod - 1), 0, 0)),
            pl.BlockSpec((PEER_HEADS, d, PEER_QDIM), lambda i: (0, 0, 0)),
            pl.BlockSpec((2, PEER_NKEYS, PEER_QDIM), lambda i: (0, 0, 0)),
        ],
        out_specs=[pl.BlockSpec((tb, PEER_PAIRS), lambda i: (i, 0)),
                   pl.BlockSpec((tb, PEER_PAIRS), lambda i: (i, 0))],
        out_shape=[jax.ShapeDtypeStruct((t_total, PEER_PAIRS), jnp.int32),
                   jax.ShapeDtypeStruct((t_total, PEER_PAIRS), jnp.float32)],
        scratch_shapes=[pltpu.VMEM((PEER_PAIRS, tb), jnp.int32), pltpu.VMEM((PEER_PAIRS, tb), jnp.float32)],
        compiler_params=pltpu.CompilerParams(dimension_semantics=("arbitrary",)),
    )(x, mod, wq, keys)


MOD_COLS_PER_STEP = 1536


def _modulation_kernel(sc_ref, w_ref, b_ref, out_ref):
    acc = jnp.dot(sc_ref[...].astype(jnp.bfloat16), w_ref[0].astype(jnp.bfloat16),
                  preferred_element_type=jnp.float32)
    out_ref[0] = acc + b_ref[0]


def _modulation(sc, w_mod, b_mod):
    n_layers, d, n_out = w_mod.shape
    s = sc.shape[0]
    tn = MOD_COLS_PER_STEP
    return pl.pallas_call(
        _modulation_kernel,
        grid=(n_layers, n_out // tn),
        in_specs=[pl.BlockSpec((s, d), lambda l, j: (0, 0)),
                  pl.BlockSpec((1, d, tn), lambda l, j: (l, 0, j)),
                  pl.BlockSpec((1, 1, tn), lambda l, j: (l, 0, j))],
        out_specs=pl.BlockSpec((1, s, tn), lambda l, j: (l, 0, j)),
        out_shape=jax.ShapeDtypeStruct((n_layers, s, n_out), jnp.float32),
        compiler_params=pltpu.CompilerParams(dimension_semantics=("arbitrary", "arbitrary"),
                                             vmem_limit_bytes=40 * 1024 * 1024),
    )(sc, w_mod, b_mod.reshape(n_layers, 1, n_out))


def _residual_layer_norm(x, g_mod, mix, ln_g, ln_b):
    y = DN_ALPHA * x + g_mod * mix
    return _layer_norm(y, ln_g, ln_b)


MIXER_ROWS_PER_STEP = 512


def _chunk_mlp_kernel(x_ref, mod_ref, win_ref, ng_ref, nb_ref, ws_ref, bs_ref, wout_ref, lng_ref, lnb_ref,
                      out_ref, gated):
    x = x_ref[...]
    sh1 = mod_ref[0, 0:1, :]
    s1 = mod_ref[0, 1:2, :]
    g1 = mod_ref[0, 2:3, :]
    h = (x * (1.0 + s1) + sh1).astype(jnp.bfloat16)
    z = jax.nn.gelu(jnp.dot(h, win_ref[...], preferred_element_type=jnp.float32))
    u = z[:, :A_WIDTH]
    v = _layer_norm(z[:, A_WIDTH:], ng_ref[...], nb_ref[...]).astype(jnp.bfloat16)
    for c in range(x.shape[0] // CHUNK):
        r0 = c * CHUNK
        for hd in range(A_HEADS):
            c0 = hd * A_GROUP
            s = jnp.dot(ws_ref[hd], v[r0:r0 + CHUNK, c0:c0 + A_GROUP], preferred_element_type=jnp.float32)
            s = s + bs_ref[:, hd:hd + 1]
            gated[r0:r0 + CHUNK, c0:c0 + A_GROUP] = (u[r0:r0 + CHUNK, c0:c0 + A_GROUP] * s).astype(jnp.bfloat16)
    mix = jnp.dot(gated[...], wout_ref[...], preferred_element_type=jnp.float32)
    out_ref[...] = _residual_layer_norm(x, g1, mix, lng_ref[...], lnb_ref[...])


def _chunk_mlp_mixer(x, mod, w_in, norm_g, norm_b, w_s, b_s, w_out, ln_g, ln_b, *, rows_per_sample):
    t_total, d = x.shape
    rows = MIXER_ROWS_PER_STEP
    n_mod = mod.shape[0]
    blocks_per_sample = rows_per_sample // rows
    const = lambda *shape: pl.BlockSpec(shape, lambda i: (0,) * len(shape))
    return pl.pallas_call(
        _chunk_mlp_kernel,
        grid=(t_total // rows,),
        in_specs=[pl.BlockSpec((rows, d), lambda i: (i, 0)),
                  pl.BlockSpec((1, 6, d), lambda i: (jnp.minimum(i // blocks_per_sample, n_mod - 1), 0, 0)),
                  const(d, 2 * A_WIDTH), const(1, A_WIDTH), const(1, A_WIDTH),
                  const(A_HEADS, CHUNK, CHUNK), const(CHUNK, A_HEADS), const(A_WIDTH, d),
                  const(1, d), const(1, d)],
        out_specs=pl.BlockSpec((rows, d), lambda i: (i, 0)),
        out_shape=jax.ShapeDtypeStruct((t_total, d), jnp.float32),
        scratch_shapes=[pltpu.VMEM((rows, A_WIDTH), jnp.bfloat16)],
        compiler_params=pltpu.CompilerParams(dimension_semantics=("arbitrary",),
                                             vmem_limit_bytes=48 * 1024 * 1024),
    )(x, mod, w_in.astype(jnp.bfloat16), norm_g.reshape(1, -1), norm_b.reshape(1, -1),
      w_s.astype(jnp.bfloat16), b_s.T, w_out.astype(jnp.bfloat16), ln_g.reshape(1, d), ln_b.reshape(1, d))


GLA_MAIN = 2 * GLA_QK + 2 * GLA_V
GLA_GATE_PAD = 128


def _gla_project_kernel(x_ref, mod_ref, win_ref, wgin_ref, wg_ref, gb_ref, q_ref, k_ref, v_ref, r_ref, g_ref):
    sh1 = mod_ref[0, 0:1, :]
    s1 = mod_ref[0, 1:2, :]
    h = (x_ref[...] * (1.0 + s1) + sh1).astype(jnp.bfloat16)
    z = jnp.dot(h, win_ref[...], preferred_element_type=jnp.float32)
    q_ref[...] = z[:, :GLA_QK] * (GLA_DK ** -0.5)
    k_ref[...] = z[:, GLA_QK:2 * GLA_QK]
    v_ref[...] = z[:, 2 * GLA_QK:2 * GLA_QK + GLA_V]
    r_ref[...] = z[:, 2 * GLA_QK + GLA_V:]
    gl = jnp.dot(h, wgin_ref[...], preferred_element_type=jnp.float32).astype(jnp.bfloat16)
    zg = jnp.dot(gl, wg_ref[...], preferred_element_type=jnp.float32) + gb_ref[...]
    g_ref[...] = jax.nn.log_sigmoid(zg) / GLA_TAU


def _gla_project(x, mod, w_in, w_gate, gate_bias, *, rows_per_sample):
    t_total, d = x.shape
    rows = MIXER_ROWS_PER_STEP
    n_mod = mod.shape[0]
    blocks_per_sample = rows_per_sample // rows
    rank = GLA_GATE_RANK
    w_main = w_in[:, :GLA_MAIN].astype(jnp.bfloat16)
    w_gin = jnp.pad(w_in[:, GLA_MAIN:], ((0, 0), (0, GLA_GATE_PAD - 2 * rank))).astype(jnp.bfloat16)
    w_g2 = jnp.zeros((GLA_GATE_PAD, 2 * GLA_QK), jnp.float32)
    w_g2 = w_g2.at[:rank, :GLA_QK].set(w_gate[0]).at[rank:2 * rank, GLA_QK:].set(w_gate[1]).astype(jnp.bfloat16)
    const = lambda *shape: pl.BlockSpec(shape, lambda i: (0,) * len(shape))
    row = lambda n: pl.BlockSpec((rows, n), lambda i: (i, 0))
    f32 = jnp.float32
    return pl.pallas_call(
        _gla_project_kernel,
        grid=(t_total // rows,),
        in_specs=[row(d),
                  pl.BlockSpec((1, 6, d), lambda i: (jnp.minimum(i // blocks_per_sample, n_mod - 1), 0, 0)),
                  const(d, GLA_MAIN), const(d, GLA_GATE_PAD), const(GLA_GATE_PAD, 2 * GLA_QK),
                  const(1, 2 * GLA_QK)],
        out_specs=[row(GLA_QK), row(GLA_QK), row(GLA_V), row(GLA_V), row(2 * GLA_QK)],
        out_shape=[jax.ShapeDtypeStruct((t_total, GLA_QK), f32), jax.ShapeDtypeStruct((t_total, GLA_QK), f32),
                   jax.ShapeDtypeStruct((t_total, GLA_V), f32), jax.ShapeDtypeStruct((t_total, GLA_V), f32),
                   jax.ShapeDtypeStruct((t_total, 2 * GLA_QK), f32)],
        compiler_params=pltpu.CompilerParams(dimension_semantics=("arbitrary",),
                                             vmem_limit_bytes=48 * 1024 * 1024),
    )(x, mod, w_main, w_gin, w_g2, gate_bias.reshape(1, 2 * GLA_QK))


def _gla_scan_kernel(qf_ref, kf_ref, vf_ref, gf_ref, qb_ref, kb_ref, vb_ref, gb_ref, of_ref, ob_ref,
                     state_f, state_b):
    @pl.when(pl.program_id(1) == 0)
    def _():
        state_f[...] = jnp.zeros_like(state_f)
        state_b[...] = jnp.zeros_like(state_b)

    n = GLA_CHUNK
    row = lax.broadcasted_iota(jnp.int32, (n, n), 0)
    col = lax.broadcasted_iota(jnp.int32, (n, n), 1)
    bf = jnp.bfloat16
    nt = (((1,), (1,)), ((), ()))
    tn = (((0,), (0,)), ((), ()))
    dirs = []
    for refs, backward in (((qf_ref, kf_ref, vf_ref, gf_ref, of_ref, state_f), False),
                           ((qb_ref, kb_ref, vb_ref, gb_ref, ob_ref, state_b), True)):
        q_ref, k_ref, v_ref, g_ref, o_ref, state = refs
        keep = (col >= row) if backward else (col <= row)
        b = jnp.dot(keep.astype(jnp.float32), g_ref[...], preferred_element_type=jnp.float32,
                    precision=lax.Precision.HIGHEST)
        b_end = b[0:1, :] if backward else b[n - 1:n, :]
        kk = k_ref[...]
        dirs.append(dict(keep=keep, o_ref=o_ref, state=state, decay=jnp.exp(b_end),
                         q_in=(q_ref[...] * jnp.exp(b)).astype(bf), k_in=(kk * jnp.exp(-b)).astype(bf),
                         k_end=(kk * jnp.exp(b_end - b)).astype(bf), v=v_ref[...].astype(bf)))
    heads = [(d, hd, slice(hd * GLA_DK, (hd + 1) * GLA_DK), slice(hd * GLA_DV, (hd + 1) * GLA_DV))
             for d in dirs for hd in range(GLA_HEADS)]
    att = [lax.dot_general(d["q_in"][:, ck], d["k_in"][:, ck], nt, preferred_element_type=jnp.float32)
           for d, hd, ck, cv in heads]
    att = [jnp.where(d["keep"], a, 0.0).astype(bf) for (d, hd, ck, cv), a in zip(heads, att)]
    old = [d["state"][hd] for d, hd, ck, cv in heads]
    inter = [lax.dot_general(d["q_in"][:, ck], s.astype(bf), nt, preferred_element_type=jnp.float32)
             for (d, hd, ck, cv), s in zip(heads, old)]
    intra = [jnp.dot(a, d["v"][:, cv], preferred_element_type=jnp.float32) for (d, hd, ck, cv), a in zip(heads, att)]
    vk = [lax.dot_general(d["v"][:, cv], d["k_end"][:, ck], tn, preferred_element_type=jnp.float32)
          for d, hd, ck, cv in heads]
    for (d, hd, ck, cv), o1, o2, s, u in zip(heads, intra, inter, old, vk):
        d["o_ref"][:, cv] = o1 + o2
        d["state"][hd] = d["decay"][:, ck] * s + u


def _gla_scan(q, k, v, g, *, bsz, lat_chunks, ctx_chunks):
    t_total = q.shape[0]
    n = GLA_CHUNK
    steps = ctx_chunks + lat_chunks
    ctx_base = bsz * lat_chunks

    def block(b, j, backward):
        if backward:
            c_ctx, c_lat = ctx_chunks - 1 - j, lat_chunks - 1 - (j - ctx_chunks)
        else:
            c_ctx, c_lat = j, j - ctx_chunks
        return jnp.where(j < ctx_chunks, ctx_base + b * ctx_chunks + c_ctx, b * lat_chunks + c_lat)

    def specs(backward):
        blk = functools.partial(block, backward=backward)
        return [pl.BlockSpec((n, GLA_QK), lambda b, j: (blk(b, j), 0)),
                pl.BlockSpec((n, GLA_QK), lambda b, j: (blk(b, j), 0)),
                pl.BlockSpec((n, GLA_V), lambda b, j: (blk(b, j), 0)),
                pl.BlockSpec((n, GLA_QK), lambda b, j: (blk(b, j), 1 if backward else 0))]

    out_spec = lambda backward: pl.BlockSpec((n, GLA_V), lambda b, j: (block(b, j, backward), 0))
    o_shape = jax.ShapeDtypeStruct((t_total, GLA_V), jnp.float32)
    return pl.pallas_call(
        _gla_scan_kernel,
        grid=(bsz, steps),
        in_specs=specs(False) + specs(True),
        out_specs=[out_spec(False), out_spec(True)],
        out_shape=[o_shape, o_shape],
        scratch_shapes=[pltpu.VMEM((GLA_HEADS, GLA_DV, GLA_DK), jnp.float32),
                        pltpu.VMEM((GLA_HEADS, GLA_DV, GLA_DK), jnp.float32)],
        compiler_params=pltpu.CompilerParams(dimension_semantics=("arbitrary", "arbitrary")),
    )(q, k, v, g, q, k, v, g)


def _gla_readout_kernel(x_ref, mod_ref, of_ref, ob_ref, r_ref, gn_ref, wout_ref, lng_ref, lnb_ref, out_ref, gated):
    x = x_ref[...]
    g1 = mod_ref[0, 2:3, :]
    o = of_ref[...] + ob_ref[...]
    r = r_ref[...]
    for hd in range(GLA_HEADS):
        cv = slice(hd * GLA_DV, (hd + 1) * GLA_DV)
        oh = o[:, cv]
        mu = jnp.mean(oh, axis=-1, keepdims=True)
        var = jnp.mean(jnp.square(oh - mu), axis=-1, keepdims=True)
        y = (oh - mu) * lax.rsqrt(var + LN_EPS) * gn_ref[:, cv]
        gated[:, cv] = (y * jax.nn.silu(r[:, cv])).astype(jnp.bfloat16)
    mix = jnp.dot(gated[...], wout_ref[...], preferred_element_type=jnp.float32)
    out_ref[...] = _residual_layer_norm(x, g1, mix, lng_ref[...], lnb_ref[...])


def _gla_readout(x, mod, o_f, o_b, r, gn_g, w_out, ln_g, ln_b, *, rows_per_sample):
    t_total, d = x.shape
    rows = MIXER_ROWS_PER_STEP
    n_mod = mod.shape[0]
    blocks_per_sample = rows_per_sample // rows
    const = lambda *shape: pl.BlockSpec(shape, lambda i: (0,) * len(shape))
    row = lambda n: pl.BlockSpec((rows, n), lambda i: (i, 0))
    return pl.pallas_call(
        _gla_readout_kernel,
        grid=(t_total // rows,),
        in_specs=[row(d),
                  pl.BlockSpec((1, 6, d), lambda i: (jnp.minimum(i // blocks_per_sample, n_mod - 1), 0, 0)),
                  row(GLA_V), row(GLA_V), row(GLA_V), const(1, GLA_V), const(GLA_V, d), const(1, d), const(1, d)],
        out_specs=row(d),
        out_shape=jax.ShapeDtypeStruct((t_total, d), jnp.float32),
        scratch_shapes=[pltpu.VMEM((rows, GLA_V), jnp.bfloat16)],
        compiler_params=pltpu.CompilerParams(dimension_semantics=("arbitrary",),
                                             vmem_limit_bytes=48 * 1024 * 1024),
    )(x, mod, o_f, o_b, r, gn_g.reshape(1, GLA_V), w_out.astype(jnp.bfloat16), ln_g.reshape(1, d), ln_b.reshape(1, d))


def _gla_mixer(x, mod, w_in, w_gate, gate_bias, gn_g, w_out, ln_g, ln_b, *, bsz, seq, ctx_len, n_out):
    q, k, v, r, g = _gla_project(x, mod, w_in, w_gate, gate_bias, rows_per_sample=seq)
    o_f, o_b = _gla_scan(q, k, v, g, bsz=bsz, lat_chunks=seq // GLA_CHUNK, ctx_chunks=ctx_len // GLA_CHUNK)
    return _gla_readout(x[:n_out], mod, o_f[:n_out], o_b[:n_out], r[:n_out], gn_g, w_out, ln_g, ln_b,
                        rows_per_sample=seq)


def kernel(x, c, ctx, c_ctx, w_mod, b_mod, ln_g, ln_b, a_w_in, a_norm_g, a_norm_b, a_w_s, a_b_s, a_w_out,
           b_w_in, b_w_gate, b_gate_bias, b_gn_g, b_w_out, p_w_q, p_keys, p_u, p_v):
    bsz, seq, d = x.shape
    ctx_len = ctx.shape[1]
    n_lat = bsz * seq
    xc = jnp.concatenate([x.reshape(n_lat, d), ctx.reshape(bsz * ctx_len, d)], axis=0)
    sc = jax.nn.silu(jnp.concatenate([c, c_ctx[None, :]], axis=0))
    mods = _modulation(sc, w_mod, b_mod).reshape(DEPTH, bsz + 1, 6, d)
    for i in range(DEPTH):
        last = i == DEPTH - 1
        j = i // N_MIXERS
        m = mods[i]
        n_tok = n_lat if last else xc.shape[0]
        if i % N_MIXERS == 0:
            x1 = _chunk_mlp_mixer(xc[:n_tok], m, a_w_in[j], a_norm_g[j], a_norm_b[j], a_w_s[j], a_b_s[j],
                                  a_w_out[j], ln_g[i, 0], ln_b[i, 0], rows_per_sample=seq)
        else:
            x1 = _gla_mixer(xc, m, b_w_in[j], b_w_gate[j], b_gate_bias[j], b_gn_g[j], b_w_out[j],
                            ln_g[i, 0], ln_b[i, 0], bsz=bsz, seq=seq, ctx_len=ctx_len, n_out=n_tok)
        idx, wts = _peer_route_pallas(x1, m, p_w_q[i], p_keys[i], tokens_per_sample_block=seq // 128)
        n_exp = p_u.shape[1]
        tab = jnp.concatenate([p_u[i].reshape(n_exp, SUBLANES, LANES), p_v[i].reshape(n_exp, SUBLANES, LANES)],
                              axis=1)
        xc = _peer_experts(x1, m, idx, wts, tab, ln_g[i, 1], ln_b[i, 1],
                           tokens_per_sample_block=seq // PEER_TOKENS_PER_STEP)
    return xc[:n_lat].reshape(bsz, seq, d)
```

```python
import functools

import jax
import jax.numpy as jnp
from jax import lax
from jax.experimental import pallas as pl
from jax.experimental.pallas import tpu as pltpu

D_MODEL = 1024
DEPTH = 4
N_MIXERS = 2
DN_ALPHA = (2.0 * DEPTH) ** 0.25
LN_EPS = 1e-5
CHUNK = 128
A_WIDTH = D_MODEL
A_HEADS = 8
A_GROUP = A_WIDTH // A_HEADS
GLA_HEADS = 4
GLA_DK = D_MODEL // 2 // GLA_HEADS
GLA_DV = D_MODEL // GLA_HEADS
GLA_QK = GLA_HEADS * GLA_DK
GLA_V = GLA_HEADS * GLA_DV
GLA_GATE_RANK = 16
GLA_TAU = 16.0
GLA_CHUNK = 64
PEER_HEADS = 8
PEER_NKEYS = 128
PEER_QDIM = 128
PEER_HALF = PEER_QDIM // 2
PEER_TOPK = 16
PEER_PAIRS = PEER_HEADS * PEER_TOPK

PEER_TOKENS_PER_STEP = 256
PEER_ROW_BUFFERS = 8


def _layer_norm(x, g, b):
    mu = jnp.mean(x, axis=-1, keepdims=True)
    var = jnp.mean(jnp.square(x - mu), axis=-1, keepdims=True)
    return (x - mu) * lax.rsqrt(var + LN_EPS) * g + b


SUBLANES = 8
LANES = 128
V7X_VMEM_BYTES = 64 * 1024 * 1024
GATHER_VMEM_LIMIT = V7X_VMEM_BYTES * 5 // 8
MIXER_VMEM_LIMIT = V7X_VMEM_BYTES * 3 // 4


def _sublane_sums(p):
    sub = lax.broadcasted_iota(jnp.int32, (SUBLANES, LANES), 0)

    def merge(a, b, sh, members):
        mask = functools.reduce(jnp.logical_or, [sub == j for j in members])
        return jnp.where(mask, a, b) + pltpu.roll(jnp.where(mask, b, a), sh, axis=0)

    c_e = merge(p[0], p[4], 4, (0, 5, 6, 7))
    d_e = merge(p[2], p[6], 4, (0, 1, 2, 7))
    c_f = merge(p[1], p[5], 4, (0, 1, 6, 7))
    d_f = merge(p[3], p[7], 4, (0, 1, 2, 3))
    e = merge(c_e, d_e, 2, (0, 3, 4, 7))
    f = merge(c_f, d_f, 2, (0, 1, 4, 5))
    return merge(e, f, 1, (0, 2, 4, 6))


def _peer_expert_kernel(zero_ref, idx_ref, idx_next_ref, w_ref, x_ref, mod_ref, lng_ref, lnb_ref, tab_ref,
                        out_ref, *scratch, tb, nbuf):
    rows = scratch[:nbuf]
    h_scr, ffn_scr, act_scr, sems = scratch[nbuf:]
    step = pl.program_id(0)
    last = pl.num_programs(0) - 1
    npairs = rows[0].shape[1]
    ahead = nbuf - 2
    sl = SUBLANES

    def issue(src_idx_ref, t, slot):
        z = zero_ref[0]
        for k in range(npairs):
            e = src_idx_ref[t, k]
            pltpu.make_async_copy(tab_ref.at[e], rows[slot].at[z, k], sems.at[slot]).start(priority=k % 2)

    def wait(slot):
        pltpu.make_async_copy(tab_ref.at[pl.ds(0, npairs)], rows[slot].at[0], sems.at[slot]).wait()

    @pl.when(step == 0)
    def _():
        for t in range(ahead):
            issue(idx_ref, t, t)

    s2 = mod_ref[0, 4]
    sh2 = mod_ref[0, 3]
    g2 = mod_ref[0, 5]
    h_scr[...] = x_ref[...].reshape(tb, sl, LANES) * (1.0 + s2) + sh2

    def token(t, slot, w_col):
        r = rows[slot].at[0]
        h = h_scr[t]
        sums = []
        for g in range(npairs // sl):
            sums.append(_sublane_sums([r[g * sl + j, 0:sl, :] * h for j in range(sl)]))
        s = jnp.sum(jnp.concatenate(sums, axis=0), axis=-1, keepdims=True)
        act_scr[...] = jnp.broadcast_to(jax.nn.gelu(s) * w_col, (npairs, LANES))
        accs = [jnp.zeros((sl, LANES), jnp.float32) for _ in range(4)]
        for k in range(npairs):
            a_k = jnp.broadcast_to(act_scr[pl.ds(k, 1), :], (sl, LANES))
            accs[k % 4] = accs[k % 4] + a_k * r[k, sl:2 * sl, :]
        ffn_scr[t] = (accs[0] + accs[1]) + (accs[2] + accs[3])

    n_groups = tb // nbuf

    def group(j8, is_last):
        w_t = w_ref[pl.ds(pl.multiple_of(j8 * nbuf, nbuf), nbuf), :].T
        for u in range(nbuf):
            t = j8 * nbuf + u
            wait(u)
            if not is_last or u + ahead < nbuf:
                issue(idx_ref, t + ahead, (u + ahead) % nbuf)
            else:
                issue(idx_next_ref, u + ahead - nbuf, (u + ahead) % nbuf)
            token(t, u, w_t[:, u:u + 1])

    def main_body(j8, carry):
        group(j8, False)
        return carry
    lax.fori_loop(0, n_groups - 1, main_body, 0)
    group(n_groups - 1, True)

    @pl.when(step == last)
    def _():
        for t in range(ahead):
            wait(t)

    ffn = (g2 * ffn_scr[...]).reshape(tb, sl * LANES)
    out_ref[...] = _layer_norm(DN_ALPHA * x_ref[...] + ffn, lng_ref[...], lnb_ref[...])


def _peer_experts(x, mod, idx, w, tab, ln_g, ln_b, *, tokens_per_sample_block):
    t_total, d = x.shape
    npairs = idx.shape[1]
    tb = PEER_TOKENS_PER_STEP
    nbuf = PEER_ROW_BUFFERS
    sl = SUBLANES
    assert d == sl * LANES and nbuf == sl and t_total % tb == 0 and tb % nbuf == 0 and npairs % sl == 0
    n_steps = t_total // tb
    n_mod = mod.shape[0]
    kern = functools.partial(_peer_expert_kernel, tb=tb, nbuf=nbuf)
    return pl.pallas_call(
        kern,
        grid=(n_steps,),
        in_specs=[
            pl.BlockSpec(memory_space=pltpu.SMEM),
            pl.BlockSpec((tb, npairs), lambda i: (i, 0), memory_space=pltpu.SMEM),
            pl.BlockSpec((tb, npairs), lambda i: (jnp.minimum(i + 1, n_steps - 1), 0), memory_space=pltpu.SMEM),
            pl.BlockSpec((tb, npairs), lambda i: (i, 0)),
            pl.BlockSpec((tb, d), lambda i: (i, 0)),
            pl.BlockSpec((1, 6, sl, LANES),
                         lambda i: (jnp.minimum(i // tokens_per_sample_block, n_mod - 1), 0, 0, 0)),
            pl.BlockSpec((1, d), lambda i: (0, 0)),
            pl.BlockSpec((1, d), lambda i: (0, 0)),
            pl.BlockSpec(memory_space=pl.ANY),
        ],
        out_specs=pl.BlockSpec((tb, d), lambda i: (i, 0)),
        out_shape=jax.ShapeDtypeStruct((t_total, d), jnp.float32),
        scratch_shapes=[pltpu.VMEM((1, npairs, 2 * sl, LANES), jnp.float32) for _ in range(nbuf)] + [
            pltpu.VMEM((tb, sl, LANES), jnp.float32),
            pltpu.VMEM((tb, sl, LANES), jnp.float32),
            pltpu.VMEM((npairs, LANES), jnp.float32),
            pltpu.SemaphoreType.DMA((nbuf,)),
        ],
        compiler_params=pltpu.CompilerParams(dimension_semantics=("arbitrary",),
                                             vmem_limit_bytes=GATHER_VMEM_LIMIT),
    )(jnp.zeros((1,), jnp.int32), idx, idx, w, x, mod.reshape(n_mod, 6, sl, LANES),
      ln_g.reshape(1, d), ln_b.reshape(1, d), tab)


_NEG_INF = float("-inf")
ROUTE_TOKENS_PER_STEP = LANES
PEER_TOPK_LOG2 = PEER_TOPK.bit_length() - 1
assert 1 << PEER_TOPK_LOG2 == PEER_TOPK == 2 * SUBLANES


def _topk_rows(s, k):
    n, width = s.shape
    rows = lax.broadcasted_iota(jnp.int32, s.shape, 0)
    out_rows = lax.broadcasted_iota(jnp.int32, (k, width), 0)
    vals = jnp.zeros((k, width), jnp.float32)
    idxs = jnp.zeros((k, width), jnp.int32)
    for r in range(k):
        m = jnp.max(s, axis=0, keepdims=True)
        first = jnp.min(jnp.where(s == m, rows, n), axis=0, keepdims=True)
        vals = jnp.where(out_rows == r, m, vals)
        idxs = jnp.where(out_rows == r, first, idxs)
        s = jnp.where(rows == first, _NEG_INF, s)
    return vals, idxs


def _pair_candidates(s1, s2):
    k = PEER_TOPK
    sl = SUBLANES
    sub = lax.broadcasted_iota(jnp.int32, (sl, 1), 0)
    groups = []

    def add(vals, a, b, keep):
        ok = keep & ((a + 1) * (b + 1) <= k)
        groups.append((jnp.where(ok, vals, _NEG_INF), a * k + b))

    for b0 in (0, sl):
        add(s1[0:1, :] + s2[b0:b0 + sl, :], 0 * sub, b0 + sub, sub >= 0)
    add(s1[sl:k, :] + s2[0:1, :], sl + sub, 0 * sub, sub >= 0)
    for b in range(5):
        add(s1[0:sl, :] + s2[b:b + 1, :], sub, b + 0 * sub, sub >= 1)
    add(s1[1:2, :] + s2[0:sl, :], 1 + 0 * sub, sub, sub >= 5)
    return groups


def _topk_groups(groups, k):
    width = groups[0][0].shape[1]
    big = PEER_TOPK * PEER_TOPK
    out_rows = lax.broadcasted_iota(jnp.int32, (k, width), 0)
    vals = jnp.zeros((k, width), jnp.float32)
    poss = jnp.zeros((k, width), jnp.int32)
    gv = [g[0] for g in groups]
    gp = [g[1] for g in groups]
    for r in range(k):
        m = functools.reduce(jnp.maximum, gv)
        m = jnp.max(m, axis=0, keepdims=True)
        f = functools.reduce(jnp.minimum, [jnp.where(v == m, p, big) for v, p in zip(gv, gp)])
        f = jnp.min(f, axis=0, keepdims=True)
        vals = jnp.where(out_rows == r, m, vals)
        poss = jnp.where(out_rows == r, f, poss)
        gv = [jnp.where(p == f, _NEG_INF, v) for v, p in zip(gv, gp)]
    return vals, poss


def _select_rows(table, sel):
    out = jnp.zeros_like(table)
    for a in range(table.shape[0]):
        out = jnp.where(sel == a, table[a:a + 1, :], out)
    return out


def _peer_route_kernel(x_ref, mod_ref, wq_ref, keys_ref, idx_ref, w_ref, idx_t, w_t):
    k = PEER_TOPK
    s2 = mod_ref[0, 4:5, :]
    sh2 = mod_ref[0, 3:4, :]
    h = (x_ref[...] * (1.0 + s2) + sh2).astype(jnp.bfloat16)

    def head(hd, carry):
        q = jnp.dot(h, wq_ref[hd], preferred_element_type=jnp.float32).astype(jnp.bfloat16)
        nt = (((1,), (1,)), ((), ()))
        sc1 = lax.dot_general(keys_ref[0], q, nt, preferred_element_type=jnp.float32)
        sc2 = lax.dot_general(keys_ref[1], q, nt, preferred_element_type=jnp.float32)
        v1, i1 = _topk_rows(sc1, k)
        v2, i2 = _topk_rows(sc2, k)
        top_s, pos = _topk_groups(_pair_candidates(v1, v2), k)
        e = jnp.exp(top_s - top_s[0:1, :])
        wts = e / jnp.sum(e, axis=0, keepdims=True)
        a = lax.shift_right_logical(pos, PEER_TOPK_LOG2)
        b = lax.bitwise_and(pos, k - 1)
        ids = _select_rows(i1, a) * PEER_NKEYS + _select_rows(i2, b)
        row0 = pl.multiple_of(hd * k, k)
        idx_t[pl.ds(row0, k), :] = ids
        w_t[pl.ds(row0, k), :] = wts
        return carry

    lax.fori_loop(0, PEER_HEADS, head, 0, unroll=4)
    idx_ref[...] = idx_t[...].T
    w_ref[...] = w_t[...].T


def _peer_route_pallas(x, mod, w_q, sub_keys, *, tokens_per_sample_block):
    t_total, d = x.shape
    tb = ROUTE_TOKENS_PER_STEP
    assert t_total % tb == 0
    n_steps = t_total // tb
    n_mod = mod.shape[0]
    wq = w_q.reshape(d, PEER_HEADS, PEER_QDIM).transpose(1, 0, 2).astype(jnp.bfloat16)
    zeros = jnp.zeros((PEER_NKEYS, PEER_HALF), sub_keys.dtype)
    keys = jnp.stack([jnp.concatenate([sub_keys[0], zeros], axis=1),
                      jnp.concatenate([zeros, sub_keys[1]], axis=1)]).astype(jnp.bfloat16)
    return pl.pallas_call(
        _peer_route_kernel,
        grid=(n_steps,),
        in_specs=[
            pl.BlockSpec((tb, d), lambda i: (i, 0)),
            pl.BlockSpec((1, 6, d), lambda i: (jnp.minimum(i // tokens_per_sample_block, n_mod - 1), 0, 0)),
            pl.BlockSpec((PEER_HEADS, d, PEER_QDIM), lambda i: (0, 0, 0)),
            pl.BlockSpec((2, PEER_NKEYS, PEER_QDIM), lambda i: (0, 0, 0)),
        ],
        out_specs=[pl.BlockSpec((tb, PEER_PAIRS), lambda i: (i, 0)),
                   pl.BlockSpec((tb, PEER_PAIRS), lambda i: (i, 0))],
        out_shape=[jax.ShapeDtypeStruct((t_total, PEER_PAIRS), jnp.int32),
                   jax.ShapeDtypeStruct((t_total, PEER_PAIRS), jnp.float32)],
        scratch_shapes=[pltpu.VMEM((PEER_PAIRS, tb), jnp.int32), pltpu.VMEM((PEER_PAIRS, tb), jnp.float32)],
        compiler_params=pltpu.CompilerParams(dimension_semantics=("arbitrary",)),
    )(x, mod, wq, keys)


MOD_COLS_PER_STEP = 1536


def _modulation_kernel(sc_ref, w_ref, b_ref, out_ref):
    acc = jnp.dot(sc_ref[...].astype(jnp.bfloat16), w_ref[0].astype(jnp.bfloat16),
                  preferred_element_type=jnp.float32)
    out_ref[0] = acc + b_ref[0]


def _modulation(sc, w_mod, b_mod):
    n_layers, d, n_out = w_mod.shape
    s = sc.shape[0]
    tn = MOD_COLS_PER_STEP
    return pl.pallas_call(
        _modulation_kernel,
        grid=(n_layers, n_out // tn),
        in_specs=[pl.BlockSpec((s, d), lambda l, j: (0, 0)),
                  pl.BlockSpec((1, d, tn), lambda l, j: (l, 0, j)),
                  pl.BlockSpec((1, 1, tn), lambda l, j: (l, 0, j))],
        out_specs=pl.BlockSpec((1, s, tn), lambda l, j: (l, 0, j)),
        out_shape=jax.ShapeDtypeStruct((n_layers, s, n_out), jnp.float32),
        compiler_params=pltpu.CompilerParams(dimension_semantics=("arbitrary", "arbitrary"),
                                             vmem_limit_bytes=GATHER_VMEM_LIMIT),
    )(sc, w_mod, b_mod.reshape(n_layers, 1, n_out))


def _residual_layer_norm(x, g_mod, mix, ln_g, ln_b):
    y = DN_ALPHA * x + g_mod * mix
    return _layer_norm(y, ln_g, ln_b)


MIXER_ROWS_PER_STEP = 512


def _chunk_mlp_kernel(x_ref, mod_ref, win_ref, ng_ref, nb_ref, ws_ref, bs_ref, wout_ref, lng_ref, lnb_ref,
                      out_ref, gated):
    x = x_ref[...]
    sh1 = mod_ref[0, 0:1, :]
    s1 = mod_ref[0, 1:2, :]
    g1 = mod_ref[0, 2:3, :]
    h = (x * (1.0 + s1) + sh1).astype(jnp.bfloat16)
    z = jax.nn.gelu(jnp.dot(h, win_ref[...], preferred_element_type=jnp.float32))
    u = z[:, :A_WIDTH]
    v = _layer_norm(z[:, A_WIDTH:], ng_ref[...], nb_ref[...]).astype(jnp.bfloat16)
    for c in range(x.shape[0] // CHUNK):
        r0 = c * CHUNK
        for hd in range(A_HEADS):
            c0 = hd * A_GROUP
            s = jnp.dot(ws_ref[hd], v[r0:r0 + CHUNK, c0:c0 + A_GROUP], preferred_element_type=jnp.float32)
            s = s + bs_ref[:, hd:hd + 1]
            gated[r0:r0 + CHUNK, c0:c0 + A_GROUP] = (u[r0:r0 + CHUNK, c0:c0 + A_GROUP] * s).astype(jnp.bfloat16)
    mix = jnp.dot(gated[...], wout_ref[...], preferred_element_type=jnp.float32)
    out_ref[...] = _residual_layer_norm(x, g1, mix, lng_ref[...], lnb_ref[...])


def _chunk_mlp_mixer(x, mod, w_in, norm_g, norm_b, w_s, b_s, w_out, ln_g, ln_b, *, rows_per_sample):
    t_total, d = x.shape
    rows = MIXER_ROWS_PER_STEP
    n_mod = mod.shape[0]
    assert t_total % rows == 0 and rows_per_sample % rows == 0
    blocks_per_sample = rows_per_sample // rows
    const = lambda *shape: pl.BlockSpec(shape, lambda i: (0,) * len(shape))
    return pl.pallas_call(
        _chunk_mlp_kernel,
        grid=(t_total // rows,),
        in_specs=[pl.BlockSpec((rows, d), lambda i: (i, 0)),
                  pl.BlockSpec((1, 6, d), lambda i: (jnp.minimum(i // blocks_per_sample, n_mod - 1), 0, 0)),
                  const(d, 2 * A_WIDTH), const(1, A_WIDTH), const(1, A_WIDTH),
                  const(A_HEADS, CHUNK, CHUNK), const(CHUNK, A_HEADS), const(A_WIDTH, d),
                  const(1, d), const(1, d)],
        out_specs=pl.BlockSpec((rows, d), lambda i: (i, 0)),
        out_shape=jax.ShapeDtypeStruct((t_total, d), jnp.float32),
        scratch_shapes=[pltpu.VMEM((rows, A_WIDTH), jnp.bfloat16)],
        compiler_params=pltpu.CompilerParams(dimension_semantics=("arbitrary",),
                                             vmem_limit_bytes=MIXER_VMEM_LIMIT),
    )(x, mod, w_in.astype(jnp.bfloat16), norm_g.reshape(1, -1), norm_b.reshape(1, -1),
      w_s.astype(jnp.bfloat16), b_s.T, w_out.astype(jnp.bfloat16), ln_g.reshape(1, d), ln_b.reshape(1, d))


GLA_MAIN = 2 * GLA_QK + 2 * GLA_V
GLA_GATE_PAD = 128


def _gla_project_kernel(x_ref, mod_ref, win_ref, wgin_ref, wg_ref, gb_ref, q_ref, k_ref, v_ref, r_ref, g_ref):
    sh1 = mod_ref[0, 0:1, :]
    s1 = mod_ref[0, 1:2, :]
    h = (x_ref[...] * (1.0 + s1) + sh1).astype(jnp.bfloat16)
    z = jnp.dot(h, win_ref[...], preferred_element_type=jnp.float32)
    q_ref[...] = z[:, :GLA_QK] * (GLA_DK ** -0.5)
    k_ref[...] = z[:, GLA_QK:2 * GLA_QK]
    v_ref[...] = z[:, 2 * GLA_QK:2 * GLA_QK + GLA_V]
    r_ref[...] = z[:, 2 * GLA_QK + GLA_V:]
    gl = jnp.dot(h, wgin_ref[...], preferred_element_type=jnp.float32).astype(jnp.bfloat16)
    zg = jnp.dot(gl, wg_ref[...], preferred_element_type=jnp.float32) + gb_ref[...]
    g_ref[...] = jax.nn.log_sigmoid(zg) / GLA_TAU


def _gla_project(x, mod, w_in, w_gate, gate_bias, *, rows_per_sample):
    t_total, d = x.shape
    rows = MIXER_ROWS_PER_STEP
    n_mod = mod.shape[0]
    assert t_total % rows == 0 and rows_per_sample % rows == 0
    blocks_per_sample = rows_per_sample // rows
    rank = GLA_GATE_RANK
    w_main = w_in[:, :GLA_MAIN].astype(jnp.bfloat16)
    w_gin = jnp.pad(w_in[:, GLA_MAIN:], ((0, 0), (0, GLA_GATE_PAD - 2 * rank))).astype(jnp.bfloat16)
    w_g2 = jnp.zeros((GLA_GATE_PAD, 2 * GLA_QK), jnp.float32)
    w_g2 = w_g2.at[:rank, :GLA_QK].set(w_gate[0]).at[rank:2 * rank, GLA_QK:].set(w_gate[1]).astype(jnp.bfloat16)
    const = lambda *shape: pl.BlockSpec(shape, lambda i: (0,) * len(shape))
    row = lambda n: pl.BlockSpec((rows, n), lambda i: (i, 0))
    f32 = jnp.float32
    return pl.pallas_call(
        _gla_project_kernel,
        grid=(t_total // rows,),
        in_specs=[row(d),
                  pl.BlockSpec((1, 6, d), lambda i: (jnp.minimum(i // blocks_per_sample, n_mod - 1), 0, 0)),
                  const(d, GLA_MAIN), const(d, GLA_GATE_PAD), const(GLA_GATE_PAD, 2 * GLA_QK),
                  const(1, 2 * GLA_QK)],
        out_specs=[row(GLA_QK), row(GLA_QK), row(GLA_V), row(GLA_V), row(2 * GLA_QK)],
        out_shape=[jax.ShapeDtypeStruct((t_total, GLA_QK), f32), jax.ShapeDtypeStruct((t_total, GLA_QK), f32),
                   jax.ShapeDtypeStruct((t_total, GLA_V), f32), jax.ShapeDtypeStruct((t_total, GLA_V), f32),
                   jax.ShapeDtypeStruct((t_total, 2 * GLA_QK), f32)],
        compiler_params=pltpu.CompilerParams(dimension_semantics=("arbitrary",),
                                             vmem_limit_bytes=MIXER_VMEM_LIMIT),
    )(x, mod, w_main, w_gin, w_g2, gate_bias.reshape(1, 2 * GLA_QK))


def _gla_scan_kernel(qf_ref, kf_ref, vf_ref, gf_ref, qb_ref, kb_ref, vb_ref, gb_ref, of_ref, ob_ref,
                     state_f, state_b):
    @pl.when(pl.program_id(1) == 0)
    def _():
        state_f[...] = jnp.zeros_like(state_f)
        state_b[...] = jnp.zeros_like(state_b)

    n = GLA_CHUNK
    row = lax.broadcasted_iota(jnp.int32, (n, n), 0)
    col = lax.broadcasted_iota(jnp.int32, (n, n), 1)
    bf = jnp.bfloat16
    nt = (((1,), (1,)), ((), ()))
    tn = (((0,), (0,)), ((), ()))
    dirs = []
    for refs, backward in (((qf_ref, kf_ref, vf_ref, gf_ref, of_ref, state_f), False),
                           ((qb_ref, kb_ref, vb_ref, gb_ref, ob_ref, state_b), True)):
        q_ref, k_ref, v_ref, g_ref, o_ref, state = refs
        keep = (col >= row) if backward else (col <= row)
        b = jnp.dot(keep.astype(jnp.float32), g_ref[...], preferred_element_type=jnp.float32,
                    precision=lax.Precision.HIGHEST)
        b_end = b[0:1, :] if backward else b[n - 1:n, :]
        kk = k_ref[...]
        dirs.append(dict(keep=keep, o_ref=o_ref, state=state, decay=jnp.exp(b_end),
                         q_in=(q_ref[...] * jnp.exp(b)).astype(bf), k_in=(kk * jnp.exp(-b)).astype(bf),
                         k_end=(kk * jnp.exp(b_end - b)).astype(bf), v=v_ref[...].astype(bf)))
    heads = [(d, hd, slice(hd * GLA_DK, (hd + 1) * GLA_DK), slice(hd * GLA_DV, (hd + 1) * GLA_DV))
             for d in dirs for hd in range(GLA_HEADS)]
    att = [lax.dot_general(d["q_in"][:, ck], d["k_in"][:, ck], nt, preferred_element_type=jnp.float32)
           for d, hd, ck, cv in heads]
    att = [jnp.where(d["keep"], a, 0.0).astype(bf) for (d, hd, ck, cv), a in zip(heads, att)]
    old = [d["state"][hd] for d, hd, ck, cv in heads]
    inter = [lax.dot_general(d["q_in"][:, ck], s.astype(bf), nt, preferred_element_type=jnp.float32)
             for (d, hd, ck, cv), s in zip(heads, old)]
    intra = [jnp.dot(a, d["v"][:, cv], preferred_element_type=jnp.float32) for (d, hd, ck, cv), a in zip(heads, att)]
    vk = [lax.dot_general(d["v"][:, cv], d["k_end"][:, ck], tn, preferred_element_type=jnp.float32)
          for d, hd, ck, cv in heads]
    for (d, hd, ck, cv), o1, o2, s, u in zip(heads, intra, inter, old, vk):
        d["o_ref"][:, cv] = o1 + o2
        d["state"][hd] = d["decay"][:, ck] * s + u


def _gla_scan(q, k, v, g, *, bsz, lat_chunks, ctx_chunks):
    t_total = q.shape[0]
    n = GLA_CHUNK
    steps = ctx_chunks + lat_chunks
    ctx_base = bsz * lat_chunks

    def block(b, j, backward):
        if backward:
            c_ctx, c_lat = ctx_chunks - 1 - j, lat_chunks - 1 - (j - ctx_chunks)
        else:
            c_ctx, c_lat = j, j - ctx_chunks
        return jnp.where(j < ctx_chunks, ctx_base + b * ctx_chunks + c_ctx, b * lat_chunks + c_lat)

    def specs(backward):
        blk = functools.partial(block, backward=backward)
        return [pl.BlockSpec((n, GLA_QK), lambda b, j: (blk(b, j), 0)),
                pl.BlockSpec((n, GLA_QK), lambda b, j: (blk(b, j), 0)),
                pl.BlockSpec((n, GLA_V), lambda b, j: (blk(b, j), 0)),
                pl.BlockSpec((n, GLA_QK), lambda b, j: (blk(b, j), 1 if backward else 0))]

    out_spec = lambda backward: pl.BlockSpec((n, GLA_V), lambda b, j: (block(b, j, backward), 0))
    o_shape = jax.ShapeDtypeStruct((t_total, GLA_V), jnp.float32)
    return pl.pallas_call(
        _gla_scan_kernel,
        grid=(bsz, steps),
        in_specs=specs(False) + specs(True),
        out_specs=[out_spec(False), out_spec(True)],
        out_shape=[o_shape, o_shape],
        scratch_shapes=[pltpu.VMEM((GLA_HEADS, GLA_DV, GLA_DK), jnp.float32),
                        pltpu.VMEM((GLA_HEADS, GLA_DV, GLA_DK), jnp.float32)],
        compiler_params=pltpu.CompilerParams(dimension_semantics=("arbitrary", "arbitrary")),
    )(q, k, v, g, q, k, v, g)


def _gla_readout_kernel(x_ref, mod_ref, of_ref, ob_ref, r_ref, gn_ref, wout_ref, lng_ref, lnb_ref, out_ref, gated):
    x = x_ref[...]
    g1 = mod_ref[0, 2:3, :]
    o = of_ref[...] + ob_ref[...]
    r = r_ref[...]
    for hd in range(GLA_HEADS):
        cv = slice(hd * GLA_DV, (hd + 1) * GLA_DV)
        oh = o[:, cv]
        mu = jnp.mean(oh, axis=-1, keepdims=True)
        var = jnp.mean(jnp.square(oh - mu), axis=-1, keepdims=True)
        y = (oh - mu) * lax.rsqrt(var + LN_EPS) * gn_ref[:, cv]
        gated[:, cv] = (y * jax.nn.silu(r[:, cv])).astype(jnp.bfloat16)
    mix = jnp.dot(gated[...], wout_ref[...], preferred_element_type=jnp.float32)
    out_ref[...] = _residual_layer_norm(x, g1, mix, lng_ref[...], lnb_ref[...])


def _gla_readout(x, mod, o_f, o_b, r, gn_g, w_out, ln_g, ln_b, *, rows_per_sample):
    t_total, d = x.shape
    rows = MIXER_ROWS_PER_STEP
    n_mod = mod.shape[0]
    assert t_total % rows == 0 and rows_per_sample % rows == 0
    blocks_per_sample = rows_per_sample // rows
    const = lambda *shape: pl.BlockSpec(shape, lambda i: (0,) * len(shape))
    row = lambda n: pl.BlockSpec((rows, n), lambda i: (i, 0))
    return pl.pallas_call(
        _gla_readout_kernel,
        grid=(t_total // rows,),
        in_specs=[row(d),
                  pl.BlockSpec((1, 6, d), lambda i: (jnp.minimum(i // blocks_per_sample, n_mod - 1), 0, 0)),
                  row(GLA_V), row(GLA_V), row(GLA_V), const(1, GLA_V), const(GLA_V, d), const(1, d), const(1, d)],
        out_specs=row(d),
        out_shape=jax.ShapeDtypeStruct((t_total, d), jnp.float32),
        scratch_shapes=[pltpu.VMEM((rows, GLA_V), jnp.bfloat16)],
        compiler_params=pltpu.CompilerParams(dimension_semantics=("arbitrary",),
                                             vmem_limit_bytes=MIXER_VMEM_LIMIT),
    )(x, mod, o_f, o_b, r, gn_g.reshape(1, GLA_V), w_out.astype(jnp.bfloat16), ln_g.reshape(1, d), ln_b.reshape(1, d))


def _gla_mixer(x, mod, w_in, w_gate, gate_bias, gn_g, w_out, ln_g, ln_b, *, bsz, seq, ctx_len, n_out):
    q, k, v, r, g = _gla_project(x, mod, w_in, w_gate, gate_bias, rows_per_sample=seq)
    o_f, o_b = _gla_scan(q, k, v, g, bsz=bsz, lat_chunks=seq // GLA_CHUNK, ctx_chunks=ctx_len // GLA_CHUNK)
    return _gla_readout(x[:n_out], mod, o_f[:n_out], o_b[:n_out], r[:n_out], gn_g, w_out, ln_g, ln_b,
                        rows_per_sample=seq)


def kernel(x, c, ctx, c_ctx, w_mod, b_mod, ln_g, ln_b, a_w_in, a_norm_g, a_norm_b, a_w_s, a_b_s, a_w_out,
           b_w_in, b_w_gate, b_gate_bias, b_gn_g, b_w_out, p_w_q, p_keys, p_u, p_v):
    bsz, seq, d = x.shape
    ctx_len = ctx.shape[1]
    n_lat = bsz * seq
    for rows in (MIXER_ROWS_PER_STEP, PEER_TOKENS_PER_STEP, ROUTE_TOKENS_PER_STEP, CHUNK, GLA_CHUNK):
        assert seq % rows == 0 and (bsz * ctx_len) % rows == 0 and ctx_len % min(rows, CHUNK) == 0
    xc = jnp.concatenate([x.reshape(n_lat, d), ctx.reshape(bsz * ctx_len, d)], axis=0)
    sc = jax.nn.silu(jnp.concatenate([c, c_ctx[None, :]], axis=0))
    mods = _modulation(sc, w_mod, b_mod).reshape(DEPTH, bsz + 1, 6, d)
    for i in range(DEPTH):
        last = i == DEPTH - 1
        j = i // N_MIXERS
        m = mods[i]
        n_tok = n_lat if last else xc.shape[0]
        if i % N_MIXERS == 0:
            x1 = _chunk_mlp_mixer(xc[:n_tok], m, a_w_in[j], a_norm_g[j], a_norm_b[j], a_w_s[j], a_b_s[j],
                                  a_w_out[j], ln_g[i, 0], ln_b[i, 0], rows_per_sample=seq)
        else:
            x1 = _gla_mixer(xc, m, b_w_in[j], b_w_gate[j], b_gate_bias[j], b_gn_g[j], b_w_out[j],
                            ln_g[i, 0], ln_b[i, 0], bsz=bsz, seq=seq, ctx_len=ctx_len, n_out=n_tok)
        idx, wts = _peer_route_pallas(x1, m, p_w_q[i], p_keys[i], tokens_per_sample_block=seq // ROUTE_TOKENS_PER_STEP)
        n_exp = p_u.shape[1]
        tab = jnp.concatenate([p_u[i].reshape(n_exp, SUBLANES, LANES), p_v[i].reshape(n_exp, SUBLANES, LANES)],
                              axis=1)
        xc = _peer_experts(x1, m, idx, wts, tab, ln_g[i, 1], ln_b[i, 1],
                           tokens_per_sample_block=seq // PEER_TOKENS_PER_STEP)
    return xc[:n_lat].reshape(bsz, seq, d)
```

```python
import functools

import jax
import jax.numpy as jnp
from jax import lax
from jax.experimental import pallas as pl
from jax.experimental.pallas import tpu as pltpu

D_MODEL = 1024
DEPTH = 4
N_MIXERS = 2
DN_ALPHA = (2.0 * DEPTH) ** 0.25
LN_EPS = 1e-5
CHUNK = 128
A_WIDTH = D_MODEL
A_HEADS = 8
A_GROUP = A_WIDTH // A_HEADS
GLA_HEADS = 4
GLA_DK = D_MODEL // 2 // GLA_HEADS
GLA_DV = D_MODEL // GLA_HEADS
GLA_QK = GLA_HEADS * GLA_DK
GLA_V = GLA_HEADS * GLA_DV
GLA_GATE_RANK = 16
GLA_TAU = 16.0
GLA_CHUNK = 64
PEER_HEADS = 8
PEER_NKEYS = 128
PEER_QDIM = 128
PEER_HALF = PEER_QDIM // 2
PEER_TOPK = 16
PEER_PAIRS = PEER_HEADS * PEER_TOPK

PEER_TOKENS_PER_STEP = 256
PEER_ROW_BUFFERS = 8


def _layer_norm(x, g, b):
    mu = jnp.mean(x, axis=-1, keepdims=True)
    var = jnp.mean(jnp.square(x - mu), axis=-1, keepdims=True)
    return (x - mu) * lax.rsqrt(var + LN_EPS) * g + b


SUBLANES = 8
LANES = 128
V7X_VMEM_BYTES = 64 * 1024 * 1024
GATHER_VMEM_LIMIT = V7X_VMEM_BYTES * 5 // 8
MIXER_VMEM_LIMIT = V7X_VMEM_BYTES * 3 // 4


def _sublane_sums(p):
    sub = lax.broadcasted_iota(jnp.int32, (SUBLANES, LANES), 0)

    def merge(a, b, sh, members):
        mask = functools.reduce(jnp.logical_or, [sub == j for j in members])
        return jnp.where(mask, a, b) + pltpu.roll(jnp.where(mask, b, a), sh, axis=0)

    c_e = merge(p[0], p[4], 4, (0, 5, 6, 7))
    d_e = merge(p[2], p[6], 4, (0, 1, 2, 7))
    c_f = merge(p[1], p[5], 4, (0, 1, 6, 7))
    d_f = merge(p[3], p[7], 4, (0, 1, 2, 3))
    e = merge(c_e, d_e, 2, (0, 3, 4, 7))
    f = merge(c_f, d_f, 2, (0, 1, 4, 5))
    return merge(e, f, 1, (0, 2, 4, 6))


def _peer_expert_kernel(zero_ref, idx_ref, idx_next_ref, w_ref, x_ref, mod_ref, lng_ref, lnb_ref, tab_ref,
                        out_ref, *scratch, tb, nbuf):
    rows = scratch[:nbuf]
    h_scr, ffn_scr, act_scr, sems = scratch[nbuf:]
    step = pl.program_id(0)
    last = pl.num_programs(0) - 1
    npairs = rows[0].shape[1]
    ahead = nbuf - 2
    sl = SUBLANES

    def issue(src_idx_ref, t, slot):
        z = zero_ref[0]
        for k in range(npairs):
            e = src_idx_ref[t, k]
            pltpu.make_async_copy(tab_ref.at[e], rows[slot].at[z, k], sems.at[slot]).start(priority=k % 2)

    def wait(slot):
        pltpu.make_async_copy(tab_ref.at[pl.ds(0, npairs)], rows[slot].at[0], sems.at[slot]).wait()

    @pl.when(step == 0)
    def _():
        for t in range(ahead):
            issue(idx_ref, t, t)

    s2 = mod_ref[0, 4]
    sh2 = mod_ref[0, 3]
    g2 = mod_ref[0, 5]
    h_scr[...] = x_ref[...].reshape(tb, sl, LANES) * (1.0 + s2) + sh2

    def token(t, slot, w_col):
        r = rows[slot].at[0]
        h = h_scr[t]
        sums = []
        for g in range(npairs // sl):
            sums.append(_sublane_sums([r[g * sl + j, 0:sl, :] * h for j in range(sl)]))
        s = jnp.sum(jnp.concatenate(sums, axis=0), axis=-1, keepdims=True)
        act_scr[...] = jnp.broadcast_to(jax.nn.gelu(s) * w_col, (npairs, LANES))
        accs = [jnp.zeros((sl, LANES), jnp.float32) for _ in range(4)]
        for k in range(npairs):
            a_k = jnp.broadcast_to(act_scr[pl.ds(k, 1), :], (sl, LANES))
            accs[k % 4] = accs[k % 4] + a_k * r[k, sl:2 * sl, :]
        ffn_scr[t] = (accs[0] + accs[1]) + (accs[2] + accs[3])

    n_groups = tb // nbuf

    def group(j8, is_last):
        w_t = w_ref[pl.ds(pl.multiple_of(j8 * nbuf, nbuf), nbuf), :].T
        for u in range(nbuf):
            t = j8 * nbuf + u
            wait(u)
            if not is_last or u + ahead < nbuf:
                issue(idx_ref, t + ahead, (u + ahead) % nbuf)
            else:
                issue(idx_next_ref, u + ahead - nbuf, (u + ahead) % nbuf)
            token(t, u, w_t[:, u:u + 1])

    def main_body(j8, carry):
        group(j8, False)
        return carry
    lax.fori_loop(0, n_groups - 1, main_body, 0)
    group(n_groups - 1, True)

    @pl.when(step == last)
    def _():
        for t in range(ahead):
            wait(t)

    ffn = (g2 * ffn_scr[...]).reshape(tb, sl * LANES)
    out_ref[...] = _layer_norm(DN_ALPHA * x_ref[...] + ffn, lng_ref[...], lnb_ref[...])


def _peer_experts(x, mod, idx, w, tab, ln_g, ln_b, *, tokens_per_sample_block):
    t_total, d = x.shape
    npairs = idx.shape[1]
    tb = PEER_TOKENS_PER_STEP
    nbuf = PEER_ROW_BUFFERS
    sl = SUBLANES
    assert d == sl * LANES and nbuf == sl and t_total % tb == 0 and tb % nbuf == 0 and npairs % sl == 0
    n_steps = t_total // tb
    n_mod = mod.shape[0]
    kern = functools.partial(_peer_expert_kernel, tb=tb, nbuf=nbuf)
    return pl.pallas_call(
        kern,
        grid=(n_steps,),
        in_specs=[
            pl.BlockSpec(memory_space=pltpu.SMEM),
            pl.BlockSpec((tb, npairs), lambda i: (i, 0), memory_space=pltpu.SMEM),
            pl.BlockSpec((tb, npairs), lambda i: (jnp.minimum(i + 1, n_steps - 1), 0), memory_space=pltpu.SMEM),
            pl.BlockSpec((tb, npairs), lambda i: (i, 0)),
            pl.BlockSpec((tb, d), lambda i: (i, 0)),
            pl.BlockSpec((1, 6, sl, LANES),
                         lambda i: (jnp.minimum(i // tokens_per_sample_block, n_mod - 1), 0, 0, 0)),
            pl.BlockSpec((1, d), lambda i: (0, 0)),
            pl.BlockSpec((1, d), lambda i: (0, 0)),
            pl.BlockSpec(memory_space=pl.ANY),
        ],
        out_specs=pl.BlockSpec((tb, d), lambda i: (i, 0)),
        out_shape=jax.ShapeDtypeStruct((t_total, d), jnp.float32),
        scratch_shapes=[pltpu.VMEM((1, npairs, 2 * sl, LANES), jnp.float32) for _ in range(nbuf)] + [
            pltpu.VMEM((tb, sl, LANES), jnp.float32),
            pltpu.VMEM((tb, sl, LANES), jnp.float32),
            pltpu.VMEM((npairs, LANES), jnp.float32),
            pltpu.SemaphoreType.DMA((nbuf,)),
        ],
        compiler_params=pltpu.CompilerParams(dimension_semantics=("arbitrary",),
                                             vmem_limit_bytes=GATHER_VMEM_LIMIT),
    )(jnp.zeros((1,), jnp.int32), idx, idx, w, x, mod.reshape(n_mod, 6, sl, LANES),
      ln_g.reshape(1, d), ln_b.reshape(1, d), tab)


_NEG_INF = float("-inf")
ROUTE_TOKENS_PER_STEP = LANES
PEER_TOPK_LOG2 = PEER_TOPK.bit_length() - 1
assert 1 << PEER_TOPK_LOG2 == PEER_TOPK == 2 * SUBLANES


def _topk_rows(s, k):
    n, width = s.shape
    rows = lax.broadcasted_iota(jnp.int32, s.shape, 0)
    out_rows = lax.broadcasted_iota(jnp.int32, (k, width), 0)
    vals = jnp.zeros((k, width), jnp.float32)
    idxs = jnp.zeros((k, width), jnp.int32)
    for r in range(k):
        m = jnp.max(s, axis=0, keepdims=True)
        first = jnp.min(jnp.where(s == m, rows, n), axis=0, keepdims=True)
        vals = jnp.where(out_rows == r, m, vals)
        idxs = jnp.where(out_rows == r, first, idxs)
        s = jnp.where(rows == first, _NEG_INF, s)
    return vals, idxs


def _pair_candidates(s1, s2):
    k = PEER_TOPK
    sl = SUBLANES
    sub = lax.broadcasted_iota(jnp.int32, (sl, 1), 0)
    groups = []

    def add(vals, a, b, keep):
        ok = keep & ((a + 1) * (b + 1) <= k)
        groups.append((jnp.where(ok, vals, _NEG_INF), a * k + b))

    for b0 in (0, sl):
        add(s1[0:1, :] + s2[b0:b0 + sl, :], 0 * sub, b0 + sub, sub >= 0)
    add(s1[sl:k, :] + s2[0:1, :], sl + sub, 0 * sub, sub >= 0)
    for b in range(5):
        add(s1[0:sl, :] + s2[b:b + 1, :], sub, b + 0 * sub, sub >= 1)
    add(s1[1:2, :] + s2[0:sl, :], 1 + 0 * sub, sub, sub >= 5)
    return groups


def _topk_groups(groups, k):
    width = groups[0][0].shape[1]
    big = PEER_TOPK * PEER_TOPK
    out_rows = lax.broadcasted_iota(jnp.int32, (k, width), 0)
    vals = jnp.zeros((k, width), jnp.float32)
    poss = jnp.zeros((k, width), jnp.int32)
    gv = [g[0] for g in groups]
    gp = [g[1] for g in groups]
    for r in range(k):
        m = functools.reduce(jnp.maximum, gv)
        m = jnp.max(m, axis=0, keepdims=True)
        f = functools.reduce(jnp.minimum, [jnp.where(v == m, p, big) for v, p in zip(gv, gp)])
        f = jnp.min(f, axis=0, keepdims=True)
        vals = jnp.where(out_rows == r, m, vals)
        poss = jnp.where(out_rows == r, f, poss)
        gv = [jnp.where(p == f, _NEG_INF, v) for v, p in zip(gv, gp)]
    return vals, poss


def _select_rows(table, sel):
    out = jnp.zeros_like(table)
    for a in range(table.shape[0]):
        out = jnp.where(sel == a, table[a:a + 1, :], out)
    return out


def _peer_route_kernel(x_ref, mod_ref, wq_ref, keys_ref, idx_ref, w_ref, idx_t, w_t):
    k = PEER_TOPK
    s2 = mod_ref[0, 4:5, :]
    sh2 = mod_ref[0, 3:4, :]
    h = (x_ref[...] * (1.0 + s2) + sh2).astype(jnp.bfloat16)

    def head(hd, carry):
        q = jnp.dot(h, wq_ref[hd], preferred_element_type=jnp.float32).astype(jnp.bfloat16)
        nt = (((1,), (1,)), ((), ()))
        sc1 = lax.dot_general(keys_ref[0], q, nt, preferred_element_type=jnp.float32)
        sc2 = lax.dot_general(keys_ref[1], q, nt, preferred_element_type=jnp.float32)
        v1, i1 = _topk_rows(sc1, k)
        v2, i2 = _topk_rows(sc2, k)
        top_s, pos = _topk_groups(_pair_candidates(v1, v2), k)
        e = jnp.exp(top_s - top_s[0:1, :])
        wts = e / jnp.sum(e, axis=0, keepdims=True)
        a = lax.shift_right_logical(pos, PEER_TOPK_LOG2)
        b = lax.bitwise_and(pos, k - 1)
        ids = _select_rows(i1, a) * PEER_NKEYS + _select_rows(i2, b)
        row0 = pl.multiple_of(hd * k, k)
        idx_t[pl.ds(row0, k), :] = ids
        w_t[pl.ds(row0, k), :] = wts
        return carry

    lax.fori_loop(0, PEER_HEADS, head, 0, unroll=True)
    idx_ref[...] = idx_t[...].T
    w_ref[...] = w_t[...].T


def _peer_route_pallas(x, mod, w_q, sub_keys, *, tokens_per_sample_block):
    t_total, d = x.shape
    tb = ROUTE_TOKENS_PER_STEP
    assert t_total % tb == 0
    n_steps = t_total // tb
    n_mod = mod.shape[0]
    wq = w_q.reshape(d, PEER_HEADS, PEER_QDIM).transpose(1, 0, 2).astype(jnp.bfloat16)
    zeros = jnp.zeros((PEER_NKEYS, PEER_HALF), sub_keys.dtype)
    keys = jnp.stack([jnp.concatenate([sub_keys[0], zeros], axis=1),
                      jnp.concatenate([zeros, sub_keys[1]], axis=1)]).astype(jnp.bfloat16)
    return pl.pallas_call(
        _peer_route_kernel,
        grid=(n_steps,),
        in_specs=[
            pl.BlockSpec((tb, d), lambda i: (i, 0)),
            pl.BlockSpec((1, 6, d), lambda i: (jnp.minimum(i // tokens_per_sample_block, n_mod - 1), 0, 0)),
            pl.BlockSpec((PEER_HEADS, d, PEER_QDIM), lambda i: (0, 0, 0)),
            pl.BlockSpec((2, PEER_NKEYS, PEER_QDIM), lambda i: (0, 0, 0)),
        ],
        out_specs=[pl.BlockSpec((tb, PEER_PAIRS), lambda i: (i, 0)),
                   pl.BlockSpec((tb, PEER_PAIRS), lambda i: (i, 0))],
        out_shape=[jax.ShapeDtypeStruct((t_total, PEER_PAIRS), jnp.int32),
                   jax.ShapeDtypeStruct((t_total, PEER_PAIRS), jnp.float32)],
        scratch_shapes=[pltpu.VMEM((PEER_PAIRS, tb), jnp.int32), pltpu.VMEM((PEER_PAIRS, tb), jnp.float32)],
        compiler_params=pltpu.CompilerParams(dimension_semantics=("arbitrary",)),
    )(x, mod, wq, keys)


MOD_COLS_PER_STEP = 1536


def _modulation_kernel(sc_ref, w_ref, b_ref, out_ref):
    acc = jnp.dot(sc_ref[...].astype(jnp.bfloat16), w_ref[0].astype(jnp.bfloat16),
                  preferred_element_type=jnp.float32)
    out_ref[0] = acc + b_ref[0]


def _modulation(sc, w_mod, b_mod):
    n_layers, d, n_out = w_mod.shape
    s = sc.shape[0]
    tn = MOD_COLS_PER_STEP
    return pl.pallas_call(
        _modulation_kernel,
        grid=(n_layers, n_out // tn),
        in_specs=[pl.BlockSpec((s, d), lambda l, j: (0, 0)),
                  pl.BlockSpec((1, d, tn), lambda l, j: (l, 0, j)),
                  pl.BlockSpec((1, 1, tn), lambda l, j: (l, 0, j))],
        out_specs=pl.BlockSpec((1, s, tn), lambda l, j: (l, 0, j)),
        out_shape=jax.ShapeDtypeStruct((n_layers, s, n_out), jnp.float32),
        compiler_params=pltpu.CompilerParams(dimension_semantics=("arbitrary", "arbitrary"),
                                             vmem_limit_bytes=GATHER_VMEM_LIMIT),
    )(sc, w_mod, b_mod.reshape(n_layers, 1, n_out))


def _residual_layer_norm(x, g_mod, mix, ln_g, ln_b):
    y = DN_ALPHA * x + g_mod * mix
    return _layer_norm(y, ln_g, ln_b)


MIXER_ROWS_PER_STEP = 512


def _chunk_mlp_kernel(x_ref, mod_ref, win_ref, ng_ref, nb_ref, ws_ref, bs_ref, wout_ref, lng_ref, lnb_ref,
                      out_ref, gated):
    x = x_ref[...]
    sh1 = mod_ref[0, 0:1, :]
    s1 = mod_ref[0, 1:2, :]
    g1 = mod_ref[0, 2:3, :]
    h = (x * (1.0 + s1) + sh1).astype(jnp.bfloat16)
    z = jax.nn.gelu(jnp.dot(h, win_ref[...], preferred_element_type=jnp.float32))
    u = z[:, :A_WIDTH]
    v = _layer_norm(z[:, A_WIDTH:], ng_ref[...], nb_ref[...]).astype(jnp.bfloat16)
    for c in range(x.shape[0] // CHUNK):
        r0 = c * CHUNK
        for hd in range(A_HEADS):
            c0 = hd * A_GROUP
            s = jnp.dot(ws_ref[hd], v[r0:r0 + CHUNK, c0:c0 + A_GROUP], preferred_element_type=jnp.float32)
            s = s + bs_ref[:, hd:hd + 1]
            gated[r0:r0 + CHUNK, c0:c0 + A_GROUP] = (u[r0:r0 + CHUNK, c0:c0 + A_GROUP] * s).astype(jnp.bfloat16)
    mix = jnp.dot(gated[...], wout_ref[...], preferred_element_type=jnp.float32)
    out_ref[...] = _residual_layer_norm(x, g1, mix, lng_ref[...], lnb_ref[...])


def _chunk_mlp_mixer(x, mod, w_in, norm_g, norm_b, w_s, b_s, w_out, ln_g, ln_b, *, rows_per_sample):
    t_total, d = x.shape
    rows = MIXER_ROWS_PER_STEP
    n_mod = mod.shape[0]
    assert t_total % rows == 0 and rows_per_sample % rows == 0
    blocks_per_sample = rows_per_sample // rows
    const = lambda *shape: pl.BlockSpec(shape, lambda i: (0,) * len(shape))
    return pl.pallas_call(
        _chunk_mlp_kernel,
        grid=(t_total // rows,),
        in_specs=[pl.BlockSpec((rows, d), lambda i: (i, 0)),
                  pl.BlockSpec((1, 6, d), lambda i: (jnp.minimum(i // blocks_per_sample, n_mod - 1), 0, 0)),
                  const(d, 2 * A_WIDTH), const(1, A_WIDTH), const(1, A_WIDTH),
                  const(A_HEADS, CHUNK, CHUNK), const(CHUNK, A_HEADS), const(A_WIDTH, d),
                  const(1, d), const(1, d)],
        out_specs=pl.BlockSpec((rows, d), lambda i: (i, 0)),
        out_shape=jax.ShapeDtypeStruct((t_total, d), jnp.float32),
        scratch_shapes=[pltpu.VMEM((rows, A_WIDTH), jnp.bfloat16)],
        compiler_params=pltpu.CompilerParams(dimension_semantics=("arbitrary",),
                                             vmem_limit_bytes=MIXER_VMEM_LIMIT),
    )(x, mod, w_in.astype(jnp.bfloat16), norm_g.reshape(1, -1), norm_b.reshape(1, -1),
      w_s.astype(jnp.bfloat16), b_s.T, w_out.astype(jnp.bfloat16), ln_g.reshape(1, d), ln_b.reshape(1, d))


GLA_MAIN = 2 * GLA_QK + 2 * GLA_V
GLA_GATE_PAD = 128


def _gla_project_kernel(x_ref, mod_ref, win_ref, wgin_ref, wg_ref, gb_ref, q_ref, k_ref, v_ref, r_ref, g_ref):
    sh1 = mod_ref[0, 0:1, :]
    s1 = mod_ref[0, 1:2, :]
    h = (x_ref[...] * (1.0 + s1) + sh1).astype(jnp.bfloat16)
    z = jnp.dot(h, win_ref[...], preferred_element_type=jnp.float32)
    q_ref[...] = z[:, :GLA_QK] * (GLA_DK ** -0.5)
    k_ref[...] = z[:, GLA_QK:2 * GLA_QK]
    v_ref[...] = z[:, 2 * GLA_QK:2 * GLA_QK + GLA_V]
    r_ref[...] = z[:, 2 * GLA_QK + GLA_V:]
    gl = jnp.dot(h, wgin_ref[...], preferred_element_type=jnp.float32).astype(jnp.bfloat16)
    zg = jnp.dot(gl, wg_ref[...], preferred_element_type=jnp.float32) + gb_ref[...]
    g_ref[...] = jax.nn.log_sigmoid(zg) / GLA_TAU


def _gla_project(x, mod, w_in, w_gate, gate_bias, *, rows_per_sample):
    t_total, d = x.shape
    rows = MIXER_ROWS_PER_STEP
    n_mod = mod.shape[0]
    assert t_total % rows == 0 and rows_per_sample % rows == 0
    blocks_per_sample = rows_per_sample // rows
    rank = GLA_GATE_RANK
    w_main = w_in[:, :GLA_MAIN].astype(jnp.bfloat16)
    w_gin = jnp.pad(w_in[:, GLA_MAIN:], ((0, 0), (0, GLA_GATE_PAD - 2 * rank))).astype(jnp.bfloat16)
    w_g2 = jnp.zeros((GLA_GATE_PAD, 2 * GLA_QK), jnp.float32)
    w_g2 = w_g2.at[:rank, :GLA_QK].set(w_gate[0]).at[rank:2 * rank, GLA_QK:].set(w_gate[1]).astype(jnp.bfloat16)
    const = lambda *shape: pl.BlockSpec(shape, lambda i: (0,) * len(shape))
    row = lambda n: pl.BlockSpec((rows, n), lambda i: (i, 0))
    f32 = jnp.float32
    return pl.pallas_call(
        _gla_project_kernel,
        grid=(t_total // rows,),
        in_specs=[row(d),
                  pl.BlockSpec((1, 6, d), lambda i: (jnp.minimum(i // blocks_per_sample, n_mod - 1), 0, 0)),
                  const(d, GLA_MAIN), const(d, GLA_GATE_PAD), const(GLA_GATE_PAD, 2 * GLA_QK),
                  const(1, 2 * GLA_QK)],
        out_specs=[row(GLA_QK), row(GLA_QK), row(GLA_V), row(GLA_V), row(2 * GLA_QK)],
        out_shape=[jax.ShapeDtypeStruct((t_total, GLA_QK), f32), jax.ShapeDtypeStruct((t_total, GLA_QK), f32),
                   jax.ShapeDtypeStruct((t_total, GLA_V), f32), jax.ShapeDtypeStruct((t_total, GLA_V), f32),
                   jax.ShapeDtypeStruct((t_total, 2 * GLA_QK), f32)],
        compiler_params=pltpu.CompilerParams(dimension_semantics=("arbitrary",),
                                             vmem_limit_bytes=MIXER_VMEM_LIMIT),
    )(x, mod, w_main, w_gin, w_g2, gate_bias.reshape(1, 2 * GLA_QK))


def _gla_scan_kernel(qf_ref, kf_ref, vf_ref, gf_ref, qb_ref, kb_ref, vb_ref, gb_ref, of_ref, ob_ref,
                     state_f, state_b):
    @pl.when(pl.program_id(1) == 0)
    def _():
        state_f[...] = jnp.zeros_like(state_f)
        state_b[...] = jnp.zeros_like(state_b)

    n = GLA_CHUNK
    row = lax.broadcasted_iota(jnp.int32, (n, n), 0)
    col = lax.broadcasted_iota(jnp.int32, (n, n), 1)
    bf = jnp.bfloat16
    nt = (((1,), (1,)), ((), ()))
    tn = (((0,), (0,)), ((), ()))
    dirs = []
    for refs, backward in (((qf_ref, kf_ref, vf_ref, gf_ref, of_ref, state_f), False),
                           ((qb_ref, kb_ref, vb_ref, gb_ref, ob_ref, state_b), True)):
        q_ref, k_ref, v_ref, g_ref, o_ref, state = refs
        keep = (col >= row) if backward else (col <= row)
        b = jnp.dot(keep.astype(jnp.float32), g_ref[...], preferred_element_type=jnp.float32,
                    precision=lax.Precision.HIGHEST)
        b_end = b[0:1, :] if backward else b[n - 1:n, :]
        kk = k_ref[...]
        dirs.append(dict(keep=keep, o_ref=o_ref, state=state, decay=jnp.exp(b_end),
                         q_in=(q_ref[...] * jnp.exp(b)).astype(bf), k_in=(kk * jnp.exp(-b)).astype(bf),
                         k_end=(kk * jnp.exp(b_end - b)).astype(bf), v=v_ref[...].astype(bf)))
    heads = [(d, hd, slice(hd * GLA_DK, (hd + 1) * GLA_DK), slice(hd * GLA_DV, (hd + 1) * GLA_DV))
             for d in dirs for hd in range(GLA_HEADS)]
    att = [lax.dot_general(d["q_in"][:, ck], d["k_in"][:, ck], nt, preferred_element_type=jnp.float32)
           for d, hd, ck, cv in heads]
    att = [jnp.where(d["keep"], a, 0.0).astype(bf) for (d, hd, ck, cv), a in zip(heads, att)]
    old = [d["state"][hd] for d, hd, ck, cv in heads]
    inter = [lax.dot_general(d["q_in"][:, ck], s.astype(bf), nt, preferred_element_type=jnp.float32)
             for (d, hd, ck, cv), s in zip(heads, old)]
    intra = [jnp.dot(a, d["v"][:, cv], preferred_element_type=jnp.float32) for (d, hd, ck, cv), a in zip(heads, att)]
    vk = [lax.dot_general(d["v"][:, cv], d["k_end"][:, ck], tn, preferred_element_type=jnp.float32)
          for d, hd, ck, cv in heads]
    for (d, hd, ck, cv), o1, o2, s, u in zip(heads, intra, inter, old, vk):
        d["o_ref"][:, cv] = o1 + o2
        d["state"][hd] = d["decay"][:, ck] * s + u


def _gla_scan(q, k, v, g, *, bsz, lat_chunks, ctx_chunks):
    t_total = q.shape[0]
    n = GLA_CHUNK
    steps = ctx_chunks + lat_chunks
    ctx_base = bsz * lat_chunks

    def block(b, j, backward):
        if backward:
            c_ctx, c_lat = ctx_chunks - 1 - j, lat_chunks - 1 - (j - ctx_chunks)
        else:
            c_ctx, c_lat = j, j - ctx_chunks
        return jnp.where(j < ctx_chunks, ctx_base + b * ctx_chunks + c_ctx, b * lat_chunks + c_lat)

    def specs(backward):
        blk = functools.partial(block, backward=backward)
        return [pl.BlockSpec((n, GLA_QK), lambda b, j: (blk(b, j), 0)),
                pl.BlockSpec((n, GLA_QK), lambda b, j: (blk(b, j), 0)),
                pl.BlockSpec((n, GLA_V), lambda b, j: (blk(b, j), 0)),
                pl.BlockSpec((n, GLA_QK), lambda b, j: (blk(b, j), 1 if backward else 0))]

    out_spec = lambda backward: pl.BlockSpec((n, GLA_V), lambda b, j: (block(b, j, backward), 0))
    o_shape = jax.ShapeDtypeStruct((t_total, GLA_V), jnp.float32)
    return pl.pallas_call(
        _gla_scan_kernel,
        grid=(bsz, steps),
        in_specs=specs(False) + specs(True),
        out_specs=[out_spec(False), out_spec(True)],
        out_shape=[o_shape, o_shape],
        scratch_shapes=[pltpu.VMEM((GLA_HEADS, GLA_DV, GLA_DK), jnp.float32),
                        pltpu.VMEM((GLA_HEADS, GLA_DV, GLA_DK), jnp.float32)],
        compiler_params=pltpu.CompilerParams(dimension_semantics=("arbitrary", "arbitrary")),
    )(q, k, v, g, q, k, v, g)


def _gla_readout_kernel(x_ref, mod_ref, of_ref, ob_ref, r_ref, gn_ref, wout_ref, lng_ref, lnb_ref, out_ref, gated):
    x = x_ref[...]
    g1 = mod_ref[0, 2:3, :]
    o = of_ref[...] + ob_ref[...]
    r = r_ref[...]
    for hd in range(GLA_HEADS):
        cv = slice(hd * GLA_DV, (hd + 1) * GLA_DV)
        oh = o[:, cv]
        mu = jnp.mean(oh, axis=-1, keepdims=True)
        var = jnp.mean(jnp.square(oh - mu), axis=-1, keepdims=True)
        y = (oh - mu) * lax.rsqrt(var + LN_EPS) * gn_ref[:, cv]
        gated[:, cv] = (y * jax.nn.silu(r[:, cv])).astype(jnp.bfloat16)
    mix = jnp.dot(gated[...], wout_ref[...], preferred_element_type=jnp.float32)
    out_ref[...] = _residual_layer_norm(x, g1, mix, lng_ref[...], lnb_ref[...])


def _gla_readout(x, mod, o_f, o_b, r, gn_g, w_out, ln_g, ln_b, *, rows_per_sample):
    t_total, d = x.shape
    rows = MIXER_ROWS_PER_STEP
    n_mod = mod.shape[0]
    assert t_total % rows == 0 and rows_per_sample % rows == 0
    blocks_per_sample = rows_per_sample // rows
    const = lambda *shape: pl.BlockSpec(shape, lambda i: (0,) * len(shape))
    row = lambda n: pl.BlockSpec((rows, n), lambda i: (i, 0))
    return pl.pallas_call(
        _gla_readout_kernel,
        grid=(t_total // rows,),
        in_specs=[row(d),
                  pl.BlockSpec((1, 6, d), lambda i: (jnp.minimum(i // blocks_per_sample, n_mod - 1), 0, 0)),
                  row(GLA_V), row(GLA_V), row(GLA_V), const(1, GLA_V), const(GLA_V, d), const(1, d), const(1, d)],
        out_specs=row(d),
        out_shape=jax.ShapeDtypeStruct((t_total, d), jnp.float32),
        scratch_shapes=[pltpu.VMEM((rows, GLA_V), jnp.bfloat16)],
        compiler_params=pltpu.CompilerParams(dimension_semantics=("arbitrary",),
                                             vmem_limit_bytes=MIXER_VMEM_LIMIT),
    )(x, mod, o_f, o_b, r, gn_g.reshape(1, GLA_V), w_out.astype(jnp.bfloat16), ln_g.reshape(1, d), ln_b.reshape(1, d))


def _gla_mixer(x, mod, w_in, w_gate, gate_bias, gn_g, w_out, ln_g, ln_b, *, bsz, seq, ctx_len, n_out):
    q, k, v, r, g = _gla_project(x, mod, w_in, w_gate, gate_bias, rows_per_sample=seq)
    o_f, o_b = _gla_scan(q, k, v, g, bsz=bsz, lat_chunks=seq // GLA_CHUNK, ctx_chunks=ctx_len // GLA_CHUNK)
    return _gla_readout(x[:n_out], mod, o_f[:n_out], o_b[:n_out], r[:n_out], gn_g, w_out, ln_g, ln_b,
                        rows_per_sample=seq)


def kernel(x, c, ctx, c_ctx, w_mod, b_mod, ln_g, ln_b, a_w_in, a_norm_g, a_norm_b, a_w_s, a_b_s, a_w_out,
           b_w_in, b_w_gate, b_gate_bias, b_gn_g, b_w_out, p_w_q, p_keys, p_u, p_v):
    bsz, seq, d = x.shape
    ctx_len = ctx.shape[1]
    n_lat = bsz * seq
    for rows in (MIXER_ROWS_PER_STEP, PEER_TOKENS_PER_STEP, ROUTE_TOKENS_PER_STEP, CHUNK, GLA_CHUNK):
        assert seq % rows == 0 and (bsz * ctx_len) % rows == 0 and ctx_len % min(rows, CHUNK) == 0
    xc = jnp.concatenate([x.reshape(n_lat, d), ctx.reshape(bsz * ctx_len, d)], axis=0)
    sc = jax.nn.silu(jnp.concatenate([c, c_ctx[None, :]], axis=0))
    mods = _modulation(sc, w_mod, b_mod).reshape(DEPTH, bsz + 1, 6, d)
    for i in range(DEPTH):
        last = i == DEPTH - 1
        j = i // N_MIXERS
        m = mods[i]
        n_tok = n_lat if last else xc.shape[0]
        if i % N_MIXERS == 0:
            x1 = _chunk_mlp_mixer(xc[:n_tok], m, a_w_in[j], a_norm_g[j], a_norm_b[j], a_w_s[j], a_b_s[j],
                                  a_w_out[j], ln_g[i, 0], ln_b[i, 0], rows_per_sample=seq)
        else:
            x1 = _gla_mixer(xc, m, b_w_in[j], b_w_gate[j], b_gate_bias[j], b_gn_g[j], b_w_out[j],
                            ln_g[i, 0], ln_b[i, 0], bsz=bsz, seq=seq, ctx_len=ctx_len, n_out=n_tok)
        idx, wts = _peer_route_pallas(x1, m, p_w_q[i], p_keys[i], tokens_per_sample_block=seq // ROUTE_TOKENS_PER_STEP)
        n_exp = p_u.shape[1]
        tab = jnp.concatenate([p_u[i].reshape(n_exp, SUBLANES, LANES), p_v[i].reshape(n_exp, SUBLANES, LANES)],
                              axis=1)
        xc = _peer_experts(x1, m, idx, wts, tab, ln_g[i, 1], ln_b[i, 1],
                           tokens_per_sample_block=seq // PEER_TOKENS_PER_STEP)
    return xc[:n_lat].reshape(bsz, seq, d)
```

```python
import functools

import jax
import jax.numpy as jnp
from jax import lax
from jax.experimental import pallas as pl
from jax.experimental.pallas import tpu as pltpu

D_MODEL = 1024
DEPTH = 4
N_MIXERS = 2
DN_ALPHA = (2.0 * DEPTH) ** 0.25
LN_EPS = 1e-5
CHUNK = 128
A_WIDTH = D_MODEL
A_HEADS = 8
A_GROUP = A_WIDTH // A_HEADS
GLA_HEADS = 4
GLA_DK = D_MODEL // 2 // GLA_HEADS
GLA_DV = D_MODEL // GLA_HEADS
GLA_QK = GLA_HEADS * GLA_DK
GLA_V = GLA_HEADS * GLA_DV
GLA_GATE_RANK = 16
GLA_TAU = 16.0
GLA_CHUNK = 64
PEER_HEADS = 8
PEER_NKEYS = 128
PEER_QDIM = 128
PEER_HALF = PEER_QDIM // 2
PEER_TOPK = 16
PEER_PAIRS = PEER_HEADS * PEER_TOPK

PEER_TOKENS_PER_STEP = 256
PEER_ROW_BUFFERS = 8


def _layer_norm(x, g, b):
    mu = jnp.mean(x, axis=-1, keepdims=True)
    var = jnp.mean(jnp.square(x - mu), axis=-1, keepdims=True)
    return (x - mu) * lax.rsqrt(var + LN_EPS) * g + b


SUBLANES = 8
LANES = 128
V7X_VMEM_BYTES = 64 * 1024 * 1024
GATHER_VMEM_LIMIT = V7X_VMEM_BYTES * 5 // 8
MIXER_VMEM_LIMIT = V7X_VMEM_BYTES * 3 // 4


def _sublane_sums(p):
    sub = lax.broadcasted_iota(jnp.int32, (SUBLANES, LANES), 0)

    def merge(a, b, sh, members):
        mask = functools.reduce(jnp.logical_or, [sub == j for j in members])
        return jnp.where(mask, a, b) + pltpu.roll(jnp.where(mask, b, a), sh, axis=0)

    c_e = merge(p[0], p[4], 4, (0, 5, 6, 7))
    d_e = merge(p[2], p[6], 4, (0, 1, 2, 7))
    c_f = merge(p[1], p[5], 4, (0, 1, 6, 7))
    d_f = merge(p[3], p[7], 4, (0, 1, 2, 3))
    e = merge(c_e, d_e, 2, (0, 3, 4, 7))
    f = merge(c_f, d_f, 2, (0, 1, 4, 5))
    return merge(e, f, 1, (0, 2, 4, 6))


def _peer_expert_kernel(zero_ref, idx_ref, idx_next_ref, w_ref, x_ref, mod_ref, lng_ref, lnb_ref, tab_ref,
                        out_ref, *scratch, tb, nbuf):
    rows = scratch[:nbuf]
    h_scr, ffn_scr, act_scr, sems = scratch[nbuf:]
    step = pl.program_id(0)
    last = pl.num_programs(0) - 1
    npairs = rows[0].shape[1]
    ahead = nbuf - 2
    sl = SUBLANES

    def issue(src_idx_ref, t, slot):
        z = zero_ref[0]
        for k in range(npairs):
            e = src_idx_ref[t, k]
            pltpu.make_async_copy(tab_ref.at[e], rows[slot].at[z, k], sems.at[slot]).start(priority=k % 2)

    def wait(slot):
        pltpu.make_async_copy(tab_ref.at[pl.ds(0, npairs)], rows[slot].at[0], sems.at[slot]).wait()

    @pl.when(step == 0)
    def _():
        for t in range(ahead):
            issue(idx_ref, t, t)

    s2 = mod_ref[0, 4]
    sh2 = mod_ref[0, 3]
    g2 = mod_ref[0, 5]
    h_scr[...] = x_ref[...].reshape(tb, sl, LANES) * (1.0 + s2) + sh2

    def token(t, slot, w_col):
        r = rows[slot].at[0]
        h = h_scr[t]
        sums = []
        for g in range(npairs // sl):
            sums.append(_sublane_sums([r[g * sl + j, 0:sl, :] * h for j in range(sl)]))
        s = jnp.sum(jnp.concatenate(sums, axis=0), axis=-1, keepdims=True)
        act_scr[...] = jnp.broadcast_to(jax.nn.gelu(s) * w_col, (npairs, LANES))
        accs = [jnp.zeros((sl, LANES), jnp.float32) for _ in range(4)]
        for k in range(npairs):
            a_k = jnp.broadcast_to(act_scr[pl.ds(k, 1), :], (sl, LANES))
            accs[k % 4] = accs[k % 4] + a_k * r[k, sl:2 * sl, :]
        ffn_scr[t] = (accs[0] + accs[1]) + (accs[2] + accs[3])

    n_groups = tb // nbuf

    def group(j8, is_last):
        w_t = w_ref[pl.ds(pl.multiple_of(j8 * nbuf, nbuf), nbuf), :].T
        for u in range(nbuf):
            t = j8 * nbuf + u
            wait(u)
            if not is_last or u + ahead < nbuf:
                issue(idx_ref, t + ahead, (u + ahead) % nbuf)
            else:
                issue(idx_next_ref, u + ahead - nbuf, (u + ahead) % nbuf)
            token(t, u, w_t[:, u:u + 1])

    def main_body(j8, carry):
        group(j8, False)
        return carry
    lax.fori_loop(0, n_groups - 1, main_body, 0)
    group(n_groups - 1, True)

    @pl.when(step == last)
    def _():
        for t in range(ahead):
            wait(t)

    ffn = (g2 * ffn_scr[...]).reshape(tb, sl * LANES)
    out_ref[...] = _layer_norm(DN_ALPHA * x_ref[...] + ffn, lng_ref[...], lnb_ref[...])


def _peer_experts(x, mod, idx, w, tab, ln_g, ln_b, *, tokens_per_sample_block):
    t_total, d = x.shape
    npairs = idx.shape[1]
    tb = PEER_TOKENS_PER_STEP
    nbuf = PEER_ROW_BUFFERS
    sl = SUBLANES
    assert d == sl * LANES and nbuf == sl and t_total % tb == 0 and tb % nbuf == 0 and npairs % sl == 0
    n_steps = t_total // tb
    n_mod = mod.shape[0]
    kern = functools.partial(_peer_expert_kernel, tb=tb, nbuf=nbuf)
    return pl.pallas_call(
        kern,
        grid=(n_steps,),
        in_specs=[
            pl.BlockSpec(memory_space=pltpu.SMEM),
            pl.BlockSpec((tb, npairs), lambda i: (i, 0), memory_space=pltpu.SMEM),
            pl.BlockSpec((tb, npairs), lambda i: (jnp.minimum(i + 1, n_steps - 1), 0), memory_space=pltpu.SMEM),
            pl.BlockSpec((tb, npairs), lambda i: (i, 0)),
            pl.BlockSpec((tb, d), lambda i: (i, 0)),
            pl.BlockSpec((1, 6, sl, LANES),
                         lambda i: (jnp.minimum(i // tokens_per_sample_block, n_mod - 1), 0, 0, 0)),
            pl.BlockSpec((1, d), lambda i: (0, 0)),
            pl.BlockSpec((1, d), lambda i: (0, 0)),
            pl.BlockSpec(memory_space=pl.ANY),
        ],
        out_specs=pl.BlockSpec((tb, d), lambda i: (i, 0)),
        out_shape=jax.ShapeDtypeStruct((t_total, d), jnp.float32),
        scratch_shapes=[pltpu.VMEM((1, npairs, 2 * sl, LANES), jnp.float32) for _ in range(nbuf)] + [
            pltpu.VMEM((tb, sl, LANES), jnp.float32),
            pltpu.VMEM((tb, sl, LANES), jnp.float32),
            pltpu.VMEM((npairs, LANES), jnp.float32),
            pltpu.SemaphoreType.DMA((nbuf,)),
        ],
        compiler_params=pltpu.CompilerParams(dimension_semantics=("arbitrary",),
                                             vmem_limit_bytes=GATHER_VMEM_LIMIT),
    )(jnp.zeros((1,), jnp.int32), idx, idx, w, x, mod.reshape(n_mod, 6, sl, LANES),
      ln_g.reshape(1, d), ln_b.reshape(1, d), tab)


_NEG_INF = float("-inf")
ROUTE_TOKENS_PER_STEP = LANES
PEER_TOPK_LOG2 = PEER_TOPK.bit_length() - 1
assert 1 << PEER_TOPK_LOG2 == PEER_TOPK == 2 * SUBLANES


def _topk_rows(s, k):
    n, width = s.shape
    rows = lax.broadcasted_iota(jnp.int32, s.shape, 0)
    out_rows = lax.broadcasted_iota(jnp.int32, (k, width), 0)
    vals = jnp.zeros((k, width), jnp.float32)
    idxs = jnp.zeros((k, width), jnp.int32)
    for r in range(k):
        m = jnp.max(s, axis=0, keepdims=True)
        first = jnp.min(jnp.where(s == m, rows, n), axis=0, keepdims=True)
        vals = jnp.where(out_rows == r, m, vals)
        idxs = jnp.where(out_rows == r, first, idxs)
        s = jnp.where(rows == first, _NEG_INF, s)
    return vals, idxs


def _pair_candidates(s1, s2):
    k = PEER_TOPK
    sl = SUBLANES
    sub = lax.broadcasted_iota(jnp.int32, (sl, 1), 0)
    groups = []

    def add(vals, a, b, keep):
        ok = keep & ((a + 1) * (b + 1) <= k)
        groups.append((jnp.where(ok, vals, _NEG_INF), a * k + b))

    for b0 in (0, sl):
        add(s1[0:1, :] + s2[b0:b0 + sl, :], 0 * sub, b0 + sub, sub >= 0)
    add(s1[sl:k, :] + s2[0:1, :], sl + sub, 0 * sub, sub >= 0)
    for b in range(5):
        add(s1[0:sl, :] + s2[b:b + 1, :], sub, b + 0 * sub, sub >= 1)
    add(s1[1:2, :] + s2[0:sl, :], 1 + 0 * sub, sub, sub >= 5)
    return groups


def _topk_groups(groups, k):
    width = groups[0][0].shape[1]
    big = PEER_TOPK * PEER_TOPK
    out_rows = lax.broadcasted_iota(jnp.int32, (k, width), 0)
    vals = jnp.zeros((k, width), jnp.float32)
    poss = jnp.zeros((k, width), jnp.int32)
    gv = [g[0] for g in groups]
    gp = [g[1] for g in groups]
    for r in range(k):
        m = functools.reduce(jnp.maximum, gv)
        m = jnp.max(m, axis=0, keepdims=True)
        f = functools.reduce(jnp.minimum, [jnp.where(v == m, p, big) for v, p in zip(gv, gp)])
        f = jnp.min(f, axis=0, keepdims=True)
        vals = jnp.where(out_rows == r, m, vals)
        poss = jnp.where(out_rows == r, f, poss)
        gv = [jnp.where(p == f, _NEG_INF, v) for v, p in zip(gv, gp)]
    return vals, poss


def _select_rows(table, sel):
    out = jnp.zeros_like(table)
    for a in range(table.shape[0]):
        out = jnp.where(sel == a, table[a:a + 1, :], out)
    return out


def _peer_route_kernel(x_ref, mod_ref, wq_ref, keys_ref, idx_ref, w_ref, idx_t, w_t):
    k = PEER_TOPK
    s2 = mod_ref[0, 4:5, :]
    sh2 = mod_ref[0, 3:4, :]
    h = (x_ref[...] * (1.0 + s2) + sh2).astype(jnp.bfloat16)

    def head(hd, carry):
        q = jnp.dot(h, wq_ref[hd], preferred_element_type=jnp.float32).astype(jnp.bfloat16)
        nt = (((1,), (1,)), ((), ()))
        sc1 = lax.dot_general(keys_ref[0], q, nt, preferred_element_type=jnp.float32)
        sc2 = lax.dot_general(keys_ref[1], q, nt, preferred_element_type=jnp.float32)
        v1, i1 = _topk_rows(sc1, k)
        v2, i2 = _topk_rows(sc2, k)
        top_s, pos = _topk_groups(_pair_candidates(v1, v2), k)
        e = jnp.exp(top_s - top_s[0:1, :])
        wts = e / jnp.sum(e, axis=0, keepdims=True)
        a = lax.shift_right_logical(pos, PEER_TOPK_LOG2)
        b = lax.bitwise_and(pos, k - 1)
        ids = _select_rows(i1, a) * PEER_NKEYS + _select_rows(i2, b)
        row0 = pl.multiple_of(hd * k, k)
        idx_t[pl.ds(row0, k), :] = ids
        w_t[pl.ds(row0, k), :] = wts
        return carry

    lax.fori_loop(0, PEER_HEADS, head, 0, unroll=True)
    idx_ref[...] = idx_t[...].T
    w_ref[...] = w_t[...].T


def _peer_route_pallas(x, mod, w_q, sub_keys, *, tokens_per_sample_block):
    t_total, d = x.shape
    tb = ROUTE_TOKENS_PER_STEP
    assert t_total % tb == 0
    n_steps = t_total // tb
    n_mod = mod.shape[0]
    wq = w_q.reshape(d, PEER_HEADS, PEER_QDIM).transpose(1, 0, 2).astype(jnp.bfloat16)
    zeros = jnp.zeros((PEER_NKEYS, PEER_HALF), sub_keys.dtype)
    keys = jnp.stack([jnp.concatenate([sub_keys[0], zeros], axis=1),
                      jnp.concatenate([zeros, sub_keys[1]], axis=1)]).astype(jnp.bfloat16)
    return pl.pallas_call(
        _peer_route_kernel,
        grid=(n_steps,),
        in_specs=[
            pl.BlockSpec((tb, d), lambda i: (i, 0)),
            pl.BlockSpec((1, 6, d), lambda i: (jnp.minimum(i // tokens_per_sample_block, n_mod - 1), 0, 0)),
            pl.BlockSpec((PEER_HEADS, d, PEER_QDIM), lambda i: (0, 0, 0)),
            pl.BlockSpec((2, PEER_NKEYS, PEER_QDIM), lambda i: (0, 0, 0)),
        ],
        out_specs=[pl.BlockSpec((tb, PEER_PAIRS), lambda i: (i, 0)),
                   pl.BlockSpec((tb, PEER_PAIRS), lambda i: (i, 0))],
        out_shape=[jax.ShapeDtypeStruct((t_total, PEER_PAIRS), jnp.int32),
                   jax.ShapeDtypeStruct((t_total, PEER_PAIRS), jnp.float32)],
        scratch_shapes=[pltpu.VMEM((PEER_PAIRS, tb), jnp.int32), pltpu.VMEM((PEER_PAIRS, tb), jnp.float32)],
        compiler_params=pltpu.CompilerParams(dimension_semantics=("arbitrary",)),
    )(x, mod, wq, keys)


MOD_COLS_PER_STEP = 1536


def _modulation_kernel(sc_ref, w_ref, b_ref, out_ref):
    acc = jnp.dot(sc_ref[...].astype(jnp.bfloat16), w_ref[0].astype(jnp.bfloat16),
                  preferred_element_type=jnp.float32)
    out_ref[0] = acc + b_ref[0]


def _modulation(sc, w_mod, b_mod):
    n_layers, d, n_out = w_mod.shape
    s = sc.shape[0]
    tn = MOD_COLS_PER_STEP
    return pl.pallas_call(
        _modulation_kernel,
        grid=(n_layers, n_out // tn),
        in_specs=[pl.BlockSpec((s, d), lambda l, j: (0, 0)),
                  pl.BlockSpec((1, d, tn), lambda l, j: (l, 0, j)),
                  pl.BlockSpec((1, 1, tn), lambda l, j: (l, 0, j))],
        out_specs=pl.BlockSpec((1, s, tn), lambda l, j: (l, 0, j)),
        out_shape=jax.ShapeDtypeStruct((n_layers, s, n_out), jnp.float32),
        compiler_params=pltpu.CompilerParams(dimension_semantics=("arbitrary", "arbitrary"),
                                             vmem_limit_bytes=GATHER_VMEM_LIMIT),
    )(sc, w_mod, b_mod.reshape(n_layers, 1, n_out))


def _residual_layer_norm(x, g_mod, mix, ln_g, ln_b):
    y = DN_ALPHA * x + g_mod * mix
    return _layer_norm(y, ln_g, ln_b)


MIXER_ROWS_PER_STEP = 512


def _chunk_mlp_kernel(x_ref, mod_ref, win_ref, ng_ref, nb_ref, ws_ref, bs_ref, wout_ref, lng_ref, lnb_ref,
                      out_ref, gated):
    x = x_ref[...]
    sh1 = mod_ref[0, 0:1, :]
    s1 = mod_ref[0, 1:2, :]
    g1 = mod_ref[0, 2:3, :]
    h = (x * (1.0 + s1) + sh1).astype(jnp.bfloat16)
    z = jax.nn.gelu(jnp.dot(h, win_ref[...], preferred_element_type=jnp.float32))
    u = z[:, :A_WIDTH]
    v = _layer_norm(z[:, A_WIDTH:], ng_ref[...], nb_ref[...]).astype(jnp.bfloat16)
    n_chunks = x.shape[0] // CHUNK
    spatial = []
    for hd in range(A_HEADS):
        c0 = hd * A_GROUP
        v_h = jnp.concatenate([v[c * CHUNK:(c + 1) * CHUNK, c0:c0 + A_GROUP] for c in range(n_chunks)], axis=1)
        spatial.append(jnp.dot(ws_ref[hd], v_h, preferred_element_type=jnp.float32))
    for hd in range(A_HEADS):
        c0 = hd * A_GROUP
        bias = bs_ref[:, hd:hd + 1]
        for c in range(n_chunks):
            r0 = c * CHUNK
            s = spatial[hd][:, c * A_GROUP:(c + 1) * A_GROUP] + bias
            gated[r0:r0 + CHUNK, c0:c0 + A_GROUP] = (u[r0:r0 + CHUNK, c0:c0 + A_GROUP] * s).astype(jnp.bfloat16)
    mix = jnp.dot(gated[...], wout_ref[...], preferred_element_type=jnp.float32)
    out_ref[...] = _residual_layer_norm(x, g1, mix, lng_ref[...], lnb_ref[...])


def _chunk_mlp_mixer(x, mod, w_in, norm_g, norm_b, w_s, b_s, w_out, ln_g, ln_b, *, rows_per_sample):
    t_total, d = x.shape
    rows = MIXER_ROWS_PER_STEP
    n_mod = mod.shape[0]
    assert t_total % rows == 0 and rows_per_sample % rows == 0
    blocks_per_sample = rows_per_sample // rows
    const = lambda *shape: pl.BlockSpec(shape, lambda i: (0,) * len(shape))
    return pl.pallas_call(
        _chunk_mlp_kernel,
        grid=(t_total // rows,),
        in_specs=[pl.BlockSpec((rows, d), lambda i: (i, 0)),
                  pl.BlockSpec((1, 6, d), lambda i: (jnp.minimum(i // blocks_per_sample, n_mod - 1), 0, 0)),
                  const(d, 2 * A_WIDTH), const(1, A_WIDTH), const(1, A_WIDTH),
                  const(A_HEADS, CHUNK, CHUNK), const(CHUNK, A_HEADS), const(A_WIDTH, d),
                  const(1, d), const(1, d)],
        out_specs=pl.BlockSpec((rows, d), lambda i: (i, 0)),
        out_shape=jax.ShapeDtypeStruct((t_total, d), jnp.float32),
        scratch_shapes=[pltpu.VMEM((rows, A_WIDTH), jnp.bfloat16)],
        compiler_params=pltpu.CompilerParams(dimension_semantics=("arbitrary",),
                                             vmem_limit_bytes=MIXER_VMEM_LIMIT),
    )(x, mod, w_in.astype(jnp.bfloat16), norm_g.reshape(1, -1), norm_b.reshape(1, -1),
      w_s.astype(jnp.bfloat16), b_s.T, w_out.astype(jnp.bfloat16), ln_g.reshape(1, d), ln_b.reshape(1, d))


GLA_MAIN = 2 * GLA_QK + 2 * GLA_V
GLA_GATE_PAD = 128


def _gla_project_kernel(x_ref, mod_ref, win_ref, wgin_ref, wg_ref, gb_ref, q_ref, k_ref, v_ref, r_ref, g_ref):
    sh1 = mod_ref[0, 0:1, :]
    s1 = mod_ref[0, 1:2, :]
    h = (x_ref[...] * (1.0 + s1) + sh1).astype(jnp.bfloat16)
    z = jnp.dot(h, win_ref[...], preferred_element_type=jnp.float32)
    q_ref[...] = z[:, :GLA_QK] * (GLA_DK ** -0.5)
    k_ref[...] = z[:, GLA_QK:2 * GLA_QK]
    v_ref[...] = z[:, 2 * GLA_QK:2 * GLA_QK + GLA_V]
    r_ref[...] = z[:, 2 * GLA_QK + GLA_V:]
    gl = jnp.dot(h, wgin_ref[...], preferred_element_type=jnp.float32).astype(jnp.bfloat16)
    zg = jnp.dot(gl, wg_ref[...], preferred_element_type=jnp.float32) + gb_ref[...]
    g_ref[...] = jax.nn.log_sigmoid(zg) / GLA_TAU


def _gla_project(x, mod, w_in, w_gate, gate_bias, *, rows_per_sample):
    t_total, d = x.shape
    rows = MIXER_ROWS_PER_STEP
    n_mod = mod.shape[0]
    assert t_total % rows == 0 and rows_per_sample % rows == 0
    blocks_per_sample = rows_per_sample // rows
    rank = GLA_GATE_RANK
    w_main = w_in[:, :GLA_MAIN].astype(jnp.bfloat16)
    w_gin = jnp.pad(w_in[:, GLA_MAIN:], ((0, 0), (0, GLA_GATE_PAD - 2 * rank))).astype(jnp.bfloat16)
    w_g2 = jnp.zeros((GLA_GATE_PAD, 2 * GLA_QK), jnp.float32)
    w_g2 = w_g2.at[:rank, :GLA_QK].set(w_gate[0]).at[rank:2 * rank, GLA_QK:].set(w_gate[1]).astype(jnp.bfloat16)
    const = lambda *shape: pl.BlockSpec(shape, lambda i: (0,) * len(shape))
    row = lambda n: pl.BlockSpec((rows, n), lambda i: (i, 0))
    f32 = jnp.float32
    return pl.pallas_call(
        _gla_project_kernel,
        grid=(t_total // rows,),
        in_specs=[row(d),
                  pl.BlockSpec((1, 6, d), lambda i: (jnp.minimum(i // blocks_per_sample, n_mod - 1), 0, 0)),
                  const(d, GLA_MAIN), const(d, GLA_GATE_PAD), const(GLA_GATE_PAD, 2 * GLA_QK),
                  const(1, 2 * GLA_QK)],
        out_specs=[row(GLA_QK), row(GLA_QK), row(GLA_V), row(GLA_V), row(2 * GLA_QK)],
        out_shape=[jax.ShapeDtypeStruct((t_total, GLA_QK), f32), jax.ShapeDtypeStruct((t_total, GLA_QK), f32),
                   jax.ShapeDtypeStruct((t_total, GLA_V), f32), jax.ShapeDtypeStruct((t_total, GLA_V), f32),
                   jax.ShapeDtypeStruct((t_total, 2 * GLA_QK), f32)],
        compiler_params=pltpu.CompilerParams(dimension_semantics=("arbitrary",),
                                             vmem_limit_bytes=MIXER_VMEM_LIMIT),
    )(x, mod, w_main, w_gin, w_g2, gate_bias.reshape(1, 2 * GLA_QK))


def _gla_scan_kernel(qf_ref, kf_ref, vf_ref, gf_ref, qb_ref, kb_ref, vb_ref, gb_ref, of_ref, ob_ref,
                     state_f, state_b):
    @pl.when(pl.program_id(1) == 0)
    def _():
        state_f[...] = jnp.zeros_like(state_f)
        state_b[...] = jnp.zeros_like(state_b)

    n = GLA_CHUNK
    row = lax.broadcasted_iota(jnp.int32, (n, n), 0)
    col = lax.broadcasted_iota(jnp.int32, (n, n), 1)
    bf = jnp.bfloat16
    nt = (((1,), (1,)), ((), ()))
    tn = (((0,), (0,)), ((), ()))
    dirs = []
    for refs, backward in (((qf_ref, kf_ref, vf_ref, gf_ref, of_ref, state_f), False),
                           ((qb_ref, kb_ref, vb_ref, gb_ref, ob_ref, state_b), True)):
        q_ref, k_ref, v_ref, g_ref, o_ref, state = refs
        keep = (col >= row) if backward else (col <= row)
        b = jnp.dot(keep.astype(jnp.float32), g_ref[...], preferred_element_type=jnp.float32,
                    precision=lax.Precision.HIGHEST)
        b_end = b[0:1, :] if backward else b[n - 1:n, :]
        kk = k_ref[...]
        dirs.append(dict(keep=keep, o_ref=o_ref, state=state, decay=jnp.exp(b_end),
                         q_in=(q_ref[...] * jnp.exp(b)).astype(bf), k_in=(kk * jnp.exp(-b)).astype(bf),
                         k_end=(kk * jnp.exp(b_end - b)).astype(bf), v=v_ref[...].astype(bf)))
    heads = [(d, hd, slice(hd * GLA_DK, (hd + 1) * GLA_DK), slice(hd * GLA_DV, (hd + 1) * GLA_DV))
             for d in dirs for hd in range(GLA_HEADS)]
    att = [lax.dot_general(d["q_in"][:, ck], d["k_in"][:, ck], nt, preferred_element_type=jnp.float32)
           for d, hd, ck, cv in heads]
    att = [jnp.where(d["keep"], a, 0.0).astype(bf) for (d, hd, ck, cv), a in zip(heads, att)]
    old = [d["state"][hd] for d, hd, ck, cv in heads]
    inter = [lax.dot_general(d["q_in"][:, ck], s.astype(bf), nt, preferred_element_type=jnp.float32)
             for (d, hd, ck, cv), s in zip(heads, old)]
    intra = [jnp.dot(a, d["v"][:, cv], preferred_element_type=jnp.float32) for (d, hd, ck, cv), a in zip(heads, att)]
    vk = [lax.dot_general(d["v"][:, cv], d["k_end"][:, ck], tn, preferred_element_type=jnp.float32)
          for d, hd, ck, cv in heads]
    for (d, hd, ck, cv), o1, o2, s, u in zip(heads, intra, inter, old, vk):
        d["o_ref"][:, cv] = o1 + o2
        d["state"][hd] = d["decay"][:, ck] * s + u


def _gla_scan(q, k, v, g, *, bsz, lat_chunks, ctx_chunks):
    t_total = q.shape[0]
    n = GLA_CHUNK
    steps = ctx_chunks + lat_chunks
    ctx_base = bsz * lat_chunks

    def block(b, j, backward):
        if backward:
            c_ctx, c_lat = ctx_chunks - 1 - j, lat_chunks - 1 - (j - ctx_chunks)
        else:
            c_ctx, c_lat = j, j - ctx_chunks
        return jnp.where(j < ctx_chunks, ctx_base + b * ctx_chunks + c_ctx, b * lat_chunks + c_lat)

    def specs(backward):
        blk = functools.partial(block, backward=backward)
        return [pl.BlockSpec((n, GLA_QK), lambda b, j: (blk(b, j), 0)),
                pl.BlockSpec((n, GLA_QK), lambda b, j: (blk(b, j), 0)),
                pl.BlockSpec((n, GLA_V), lambda b, j: (blk(b, j), 0)),
                pl.BlockSpec((n, GLA_QK), lambda b, j: (blk(b, j), 1 if backward else 0))]

    out_spec = lambda backward: pl.BlockSpec((n, GLA_V), lambda b, j: (block(b, j, backward), 0))
    o_shape = jax.ShapeDtypeStruct((t_total, GLA_V), jnp.float32)
    return pl.pallas_call(
        _gla_scan_kernel,
        grid=(bsz, steps),
        in_specs=specs(False) + specs(True),
        out_specs=[out_spec(False), out_spec(True)],
        out_shape=[o_shape, o_shape],
        scratch_shapes=[pltpu.VMEM((GLA_HEADS, GLA_DV, GLA_DK), jnp.float32),
                        pltpu.VMEM((GLA_HEADS, GLA_DV, GLA_DK), jnp.float32)],
        compiler_params=pltpu.CompilerParams(dimension_semantics=("arbitrary", "arbitrary")),
    )(q, k, v, g, q, k, v, g)


def _gla_readout_kernel(x_ref, mod_ref, of_ref, ob_ref, r_ref, gn_ref, wout_ref, lng_ref, lnb_ref, out_ref, gated):
    x = x_ref[...]
    g1 = mod_ref[0, 2:3, :]
    o = of_ref[...] + ob_ref[...]
    r = r_ref[...]
    for hd in range(GLA_HEADS):
        cv = slice(hd * GLA_DV, (hd + 1) * GLA_DV)
        oh = o[:, cv]
        mu = jnp.mean(oh, axis=-1, keepdims=True)
        var = jnp.mean(jnp.square(oh - mu), axis=-1, keepdims=True)
        y = (oh - mu) * lax.rsqrt(var + LN_EPS) * gn_ref[:, cv]
        gated[:, cv] = (y * jax.nn.silu(r[:, cv])).astype(jnp.bfloat16)
    mix = jnp.dot(gated[...], wout_ref[...], preferred_element_type=jnp.float32)
    out_ref[...] = _residual_layer_norm(x, g1, mix, lng_ref[...], lnb_ref[...])


def _gla_readout(x, mod, o_f, o_b, r, gn_g, w_out, ln_g, ln_b, *, rows_per_sample):
    t_total, d = x.shape
    rows = MIXER_ROWS_PER_STEP
    n_mod = mod.shape[0]
    assert t_total % rows == 0 and rows_per_sample % rows == 0
    blocks_per_sample = rows_per_sample // rows
    const = lambda *shape: pl.BlockSpec(shape, lambda i: (0,) * len(shape))
    row = lambda n: pl.BlockSpec((rows, n), lambda i: (i, 0))
    return pl.pallas_call(
        _gla_readout_kernel,
        grid=(t_total // rows,),
        in_specs=[row(d),
                  pl.BlockSpec((1, 6, d), lambda i: (jnp.minimum(i // blocks_per_sample, n_mod - 1), 0, 0)),
                  row(GLA_V), row(GLA_V), row(GLA_V), const(1, GLA_V), const(GLA_V, d), const(1, d), const(1, d)],
        out_specs=row(d),
        out_shape=jax.ShapeDtypeStruct((t_total, d), jnp.float32),
        scratch_shapes=[pltpu.VMEM((rows, GLA_V), jnp.bfloat16)],
        compiler_params=pltpu.CompilerParams(dimension_semantics=("arbitrary",),
                                             vmem_limit_bytes=MIXER_VMEM_LIMIT),
    )(x, mod, o_f, o_b, r, gn_g.reshape(1, GLA_V), w_out.astype(jnp.bfloat16), ln_g.reshape(1, d), ln_b.reshape(1, d))


def _gla_mixer(x, mod, w_in, w_gate, gate_bias, gn_g, w_out, ln_g, ln_b, *, bsz, seq, ctx_len, n_out):
    q, k, v, r, g = _gla_project(x, mod, w_in, w_gate, gate_bias, rows_per_sample=seq)
    o_f, o_b = _gla_scan(q, k, v, g, bsz=bsz, lat_chunks=seq // GLA_CHUNK, ctx_chunks=ctx_len // GLA_CHUNK)
    return _gla_readout(x[:n_out], mod, o_f[:n_out], o_b[:n_out], r[:n_out], gn_g, w_out, ln_g, ln_b,
                        rows_per_sample=seq)


def kernel(x, c, ctx, c_ctx, w_mod, b_mod, ln_g, ln_b, a_w_in, a_norm_g, a_norm_b, a_w_s, a_b_s, a_w_out,
           b_w_in, b_w_gate, b_gate_bias, b_gn_g, b_w_out, p_w_q, p_keys, p_u, p_v):
    bsz, seq, d = x.shape
    ctx_len = ctx.shape[1]
    n_lat = bsz * seq
    for rows in (MIXER_ROWS_PER_STEP, PEER_TOKENS_PER_STEP, ROUTE_TOKENS_PER_STEP, CHUNK, GLA_CHUNK):
        assert seq % rows == 0 and (bsz * ctx_len) % rows == 0 and ctx_len % min(rows, CHUNK) == 0
    xc = jnp.concatenate([x.reshape(n_lat, d), ctx.reshape(bsz * ctx_len, d)], axis=0)
    sc = jax.nn.silu(jnp.concatenate([c, c_ctx[None, :]], axis=0))
    mods = _modulation(sc, w_mod, b_mod).reshape(DEPTH, bsz + 1, 6, d)
    for i in range(DEPTH):
        last = i == DEPTH - 1
        j = i // N_MIXERS
        m = mods[i]
        n_tok = n_lat if last else xc.shape[0]
        if i % N_MIXERS == 0:
            x1 = _chunk_mlp_mixer(xc[:n_tok], m, a_w_in[j], a_norm_g[j], a_norm_b[j], a_w_s[j], a_b_s[j],
                                  a_w_out[j], ln_g[i, 0], ln_b[i, 0], rows_per_sample=seq)
        else:
            x1 = _gla_mixer(xc, m, b_w_in[j], b_w_gate[j], b_gate_bias[j], b_gn_g[j], b_w_out[j],
                            ln_g[i, 0], ln_b[i, 0], bsz=bsz, seq=seq, ctx_len=ctx_len, n_out=n_tok)
        idx, wts = _peer_route_pallas(x1, m, p_w_q[i], p_keys[i], tokens_per_sample_block=seq // ROUTE_TOKENS_PER_STEP)
        n_exp = p_u.shape[1]
        tab = jnp.concatenate([p_u[i].reshape(n_exp, SUBLANES, LANES), p_v[i].reshape(n_exp, SUBLANES, LANES)],
                              axis=1)
        xc = _peer_experts(x1, m, idx, wts, tab, ln_g[i, 1], ln_b[i, 1],
                           tokens_per_sample_block=seq // PEER_TOKENS_PER_STEP)
    return xc[:n_lat].reshape(bsz, seq, d)
```

```python
import functools

import jax
import jax.numpy as jnp
from jax import lax
from jax.experimental import pallas as pl
from jax.experimental.pallas import tpu as pltpu

D_MODEL = 1024
DEPTH = 4
N_MIXERS = 2
DN_ALPHA = (2.0 * DEPTH) ** 0.25
LN_EPS = 1e-5
CHUNK = 128
A_WIDTH = D_MODEL
A_HEADS = 8
A_GROUP = A_WIDTH // A_HEADS
GLA_HEADS = 4
GLA_DK = D_MODEL // 2 // GLA_HEADS
GLA_DV = D_MODEL // GLA_HEADS
GLA_QK = GLA_HEADS * GLA_DK
GLA_V = GLA_HEADS * GLA_DV
GLA_GATE_RANK = 16
GLA_TAU = 16.0
GLA_CHUNK = 64
PEER_HEADS = 8
PEER_NKEYS = 128
PEER_QDIM = 128
PEER_HALF = PEER_QDIM // 2
PEER_TOPK = 16
PEER_PAIRS = PEER_HEADS * PEER_TOPK

PEER_TOKENS_PER_STEP = 256
PEER_ROW_BUFFERS = 8


def _layer_norm(x, g, b):
    mu = jnp.mean(x, axis=-1, keepdims=True)
    var = jnp.mean(jnp.square(x - mu), axis=-1, keepdims=True)
    return (x - mu) * lax.rsqrt(var + LN_EPS) * g + b


SUBLANES = 8
LANES = 128
V7X_VMEM_BYTES = 64 * 1024 * 1024
GATHER_VMEM_LIMIT = V7X_VMEM_BYTES * 5 // 8
MIXER_VMEM_LIMIT = V7X_VMEM_BYTES * 3 // 4


def _sublane_sums(p):
    sub = lax.broadcasted_iota(jnp.int32, (SUBLANES, LANES), 0)

    def merge(a, b, sh, members):
        mask = functools.reduce(jnp.logical_or, [sub == j for j in members])
        return jnp.where(mask, a, b) + pltpu.roll(jnp.where(mask, b, a), sh, axis=0)

    c_e = merge(p[0], p[4], 4, (0, 5, 6, 7))
    d_e = merge(p[2], p[6], 4, (0, 1, 2, 7))
    c_f = merge(p[1], p[5], 4, (0, 1, 6, 7))
    d_f = merge(p[3], p[7], 4, (0, 1, 2, 3))
    e = merge(c_e, d_e, 2, (0, 3, 4, 7))
    f = merge(c_f, d_f, 2, (0, 1, 4, 5))
    return merge(e, f, 1, (0, 2, 4, 6))


def _peer_expert_kernel(zero_ref, idx_ref, idx_next_ref, w_ref, x_ref, mod_ref, lng_ref, lnb_ref, tab_ref,
                        out_ref, *scratch, tb, nbuf):
    rows = scratch[:nbuf]
    h_scr, ffn_scr, act_scr, sems = scratch[nbuf:]
    step = pl.program_id(0)
    last = pl.num_programs(0) - 1
    npairs = rows[0].shape[1]
    ahead = nbuf - 2
    sl = SUBLANES

    def issue(src_idx_ref, t, slot):
        z = zero_ref[0]
        for k in range(npairs):
            e = src_idx_ref[t, k]
            pltpu.make_async_copy(tab_ref.at[e], rows[slot].at[z, k], sems.at[slot]).start(priority=k % 2)

    def wait(slot):
        pltpu.make_async_copy(tab_ref.at[pl.ds(0, npairs)], rows[slot].at[0], sems.at[slot]).wait()

    @pl.when(step == 0)
    def _():
        for t in range(ahead):
            issue(idx_ref, t, t)

    s2 = mod_ref[0, 4]
    sh2 = mod_ref[0, 3]
    g2 = mod_ref[0, 5]
    h_scr[...] = x_ref[...].reshape(tb, sl, LANES) * (1.0 + s2) + sh2

    def token(t, slot, w_col):
        r = rows[slot].at[0]
        h = h_scr[t]
        sums = []
        for g in range(npairs // sl):
            sums.append(_sublane_sums([r[g * sl + j, 0:sl, :] * h for j in range(sl)]))
        s = jnp.sum(jnp.concatenate(sums, axis=0), axis=-1, keepdims=True)
        act_scr[...] = jnp.broadcast_to(jax.nn.gelu(s) * w_col, (npairs, LANES))
        accs = [jnp.zeros((sl, LANES), jnp.float32) for _ in range(4)]
        for k in range(npairs):
            a_k = jnp.broadcast_to(act_scr[pl.ds(k, 1), :], (sl, LANES))
            accs[k % 4] = accs[k % 4] + a_k * r[k, sl:2 * sl, :]
        ffn_scr[t] = (accs[0] + accs[1]) + (accs[2] + accs[3])

    n_groups = tb // nbuf

    def group(j8, is_last):
        w_t = w_ref[pl.ds(pl.multiple_of(j8 * nbuf, nbuf), nbuf), :].T
        for u in range(nbuf):
            t = j8 * nbuf + u
            wait(u)
            if not is_last or u + ahead < nbuf:
                issue(idx_ref, t + ahead, (u + ahead) % nbuf)
            else:
                issue(idx_next_ref, u + ahead - nbuf, (u + ahead) % nbuf)
            token(t, u, w_t[:, u:u + 1])

    def main_body(j8, carry):
        group(j8, False)
        return carry
    lax.fori_loop(0, n_groups - 1, main_body, 0)
    group(n_groups - 1, True)

    @pl.when(step == last)
    def _():
        for t in range(ahead):
            wait(t)

    ffn = (g2 * ffn_scr[...]).reshape(tb, sl * LANES)
    out_ref[...] = _layer_norm(DN_ALPHA * x_ref[...] + ffn, lng_ref[...], lnb_ref[...])


def _peer_experts(x, mod, idx, w, tab, ln_g, ln_b, *, tokens_per_sample_block):
    t_total, d = x.shape
    npairs = idx.shape[1]
    tb = PEER_TOKENS_PER_STEP
    nbuf = PEER_ROW_BUFFERS
    sl = SUBLANES
    assert d == sl * LANES and nbuf == sl and t_total % tb == 0 and tb % nbuf == 0 and npairs % sl == 0
    n_steps = t_total // tb
    n_mod = mod.shape[0]
    kern = functools.partial(_peer_expert_kernel, tb=tb, nbuf=nbuf)
    return pl.pallas_call(
        kern,
        grid=(n_steps,),
        in_specs=[
            pl.BlockSpec(memory_space=pltpu.SMEM),
            pl.BlockSpec((tb, npairs), lambda i: (i, 0), memory_space=pltpu.SMEM),
            pl.BlockSpec((tb, npairs), lambda i: (jnp.minimum(i + 1, n_steps - 1), 0), memory_space=pltpu.SMEM),
            pl.BlockSpec((tb, npairs), lambda i: (i, 0)),
            pl.BlockSpec((tb, d), lambda i: (i, 0)),
            pl.BlockSpec((1, 6, sl, LANES),
                         lambda i: (jnp.minimum(i // tokens_per_sample_block, n_mod - 1), 0, 0, 0)),
            pl.BlockSpec((1, d), lambda i: (0, 0)),
            pl.BlockSpec((1, d), lambda i: (0, 0)),
            pl.BlockSpec(memory_space=pl.ANY),
        ],
        out_specs=pl.BlockSpec((tb, d), lambda i: (i, 0)),
        out_shape=jax.ShapeDtypeStruct((t_total, d), jnp.float32),
        scratch_shapes=[pltpu.VMEM((1, npairs, 2 * sl, LANES), jnp.float32) for _ in range(nbuf)] + [
            pltpu.VMEM((tb, sl, LANES), jnp.float32),
            pltpu.VMEM((tb, sl, LANES), jnp.float32),
            pltpu.VMEM((npairs, LANES), jnp.float32),
            pltpu.SemaphoreType.DMA((nbuf,)),
        ],
        compiler_params=pltpu.CompilerParams(dimension_semantics=("arbitrary",),
                                             vmem_limit_bytes=GATHER_VMEM_LIMIT),
    )(jnp.zeros((1,), jnp.int32), idx, idx, w, x, mod.reshape(n_mod, 6, sl, LANES),
      ln_g.reshape(1, d), ln_b.reshape(1, d), tab)


_NEG_INF = float("-inf")
ROUTE_TOKENS_PER_STEP = LANES
PEER_TOPK_LOG2 = PEER_TOPK.bit_length() - 1
assert 1 << PEER_TOPK_LOG2 == PEER_TOPK == 2 * SUBLANES


def _topk_rows(s, k):
    n, width = s.shape
    rows = lax.broadcasted_iota(jnp.int32, s.shape, 0)
    out_rows = lax.broadcasted_iota(jnp.int32, (k, width), 0)
    vals = jnp.zeros((k, width), jnp.float32)
    idxs = jnp.zeros((k, width), jnp.int32)
    for r in range(k):
        m = jnp.max(s, axis=0, keepdims=True)
        first = jnp.min(jnp.where(s == m, rows, n), axis=0, keepdims=True)
        vals = jnp.where(out_rows == r, m, vals)
        idxs = jnp.where(out_rows == r, first, idxs)
        s = jnp.where(rows == first, _NEG_INF, s)
    return vals, idxs


def _pair_candidates(s1, s2):
    k = PEER_TOPK
    sl = SUBLANES
    sub = lax.broadcasted_iota(jnp.int32, (sl, 1), 0)
    groups = []

    def add(vals, a, b, keep):
        ok = keep & ((a + 1) * (b + 1) <= k)
        groups.append((jnp.where(ok, vals, _NEG_INF), a * k + b))

    for b0 in (0, sl):
        add(s1[0:1, :] + s2[b0:b0 + sl, :], 0 * sub, b0 + sub, sub >= 0)
    add(s1[sl:k, :] + s2[0:1, :], sl + sub, 0 * sub, sub >= 0)
    for b in range(5):
        add(s1[0:sl, :] + s2[b:b + 1, :], sub, b + 0 * sub, sub >= 1)
    add(s1[1:2, :] + s2[0:sl, :], 1 + 0 * sub, sub, sub >= 5)
    return groups


def _topk_groups(groups, k):
    width = groups[0][0].shape[1]
    big = PEER_TOPK * PEER_TOPK
    out_rows = lax.broadcasted_iota(jnp.int32, (k, width), 0)
    vals = jnp.zeros((k, width), jnp.float32)
    poss = jnp.zeros((k, width), jnp.int32)
    gv = [g[0] for g in groups]
    gp = [g[1] for g in groups]
    for r in range(k):
        m = functools.reduce(jnp.maximum, gv)
        m = jnp.max(m, axis=0, keepdims=True)
        f = functools.reduce(jnp.minimum, [jnp.where(v == m, p, big) for v, p in zip(gv, gp)])
        f = jnp.min(f, axis=0, keepdims=True)
        vals = jnp.where(out_rows == r, m, vals)
        poss = jnp.where(out_rows == r, f, poss)
        gv = [jnp.where(p == f, _NEG_INF, v) for v, p in zip(gv, gp)]
    return vals, poss


def _select_rows(table, sel):
    out = jnp.zeros_like(table)
    for a in range(table.shape[0]):
        out = jnp.where(sel == a, table[a:a + 1, :], out)
    return out


def _peer_route_kernel(x_ref, mod_ref, wq_ref, keys_ref, idx_ref, w_ref, idx_t, w_t):
    k = PEER_TOPK
    s2 = mod_ref[0, 4:5, :]
    sh2 = mod_ref[0, 3:4, :]
    h = (x_ref[...] * (1.0 + s2) + sh2).astype(jnp.bfloat16)

    def head(hd, carry):
        q = jnp.dot(h, wq_ref[hd], preferred_element_type=jnp.float32).astype(jnp.bfloat16)
        nt = (((1,), (1,)), ((), ()))
        sc1 = lax.dot_general(keys_ref[0], q, nt, preferred_element_type=jnp.float32)
        sc2 = lax.dot_general(keys_ref[1], q, nt, preferred_element_type=jnp.float32)
        v1, i1 = _topk_rows(sc1, k)
        v2, i2 = _topk_rows(sc2, k)
        top_s, pos = _topk_groups(_pair_candidates(v1, v2), k)
        e = jnp.exp(top_s - top_s[0:1, :])
        wts = e / jnp.sum(e, axis=0, keepdims=True)
        a = lax.shift_right_logical(pos, PEER_TOPK_LOG2)
        b = lax.bitwise_and(pos, k - 1)
        ids = _select_rows(i1, a) * PEER_NKEYS + _select_rows(i2, b)
        row0 = pl.multiple_of(hd * k, k)
        idx_t[pl.ds(row0, k), :] = ids
        w_t[pl.ds(row0, k), :] = wts
        return carry

    lax.fori_loop(0, PEER_HEADS, head, 0, unroll=True)
    idx_ref[...] = idx_t[...].T
    w_ref[...] = w_t[...].T


def _peer_route_pallas(x, mod, w_q, sub_keys, *, tokens_per_sample_block):
    t_total, d = x.shape
    tb = ROUTE_TOKENS_PER_STEP
    assert t_total % tb == 0
    n_steps = t_total // tb
    n_mod = mod.shape[0]
    wq = w_q.reshape(d, PEER_HEADS, PEER_QDIM).transpose(1, 0, 2).astype(jnp.bfloat16)
    zeros = jnp.zeros((PEER_NKEYS, PEER_HALF), sub_keys.dtype)
    keys = jnp.stack([jnp.concatenate([sub_keys[0], zeros], axis=1),
                      jnp.concatenate([zeros, sub_keys[1]], axis=1)]).astype(jnp.bfloat16)
    return pl.pallas_call(
        _peer_route_kernel,
        grid=(n_steps,),
        in_specs=[
            pl.BlockSpec((tb, d), lambda i: (i, 0)),
            pl.BlockSpec((1, 6, d), lambda i: (jnp.minimum(i // tokens_per_sample_block, n_mod - 1), 0, 0)),
            pl.BlockSpec((PEER_HEADS, d, PEER_QDIM), lambda i: (0, 0, 0)),
            pl.BlockSpec((2, PEER_NKEYS, PEER_QDIM), lambda i: (0, 0, 0)),
        ],
        out_specs=[pl.BlockSpec((tb, PEER_PAIRS), lambda i: (i, 0)),
                   pl.BlockSpec((tb, PEER_PAIRS), lambda i: (i, 0))],
        out_shape=[jax.ShapeDtypeStruct((t_total, PEER_PAIRS), jnp.int32),
                   jax.ShapeDtypeStruct((t_total, PEER_PAIRS), jnp.float32)],
        scratch_shapes=[pltpu.VMEM((PEER_PAIRS, tb), jnp.int32), pltpu.VMEM((PEER_PAIRS, tb), jnp.float32)],
        compiler_params=pltpu.CompilerParams(dimension_semantics=("arbitrary",)),
    )(x, mod, wq, keys)


MOD_COLS_PER_STEP = 1536


def _modulation_kernel(sc_ref, w_ref, b_ref, out_ref):
    acc = jnp.dot(sc_ref[...].astype(jnp.bfloat16), w_ref[0].astype(jnp.bfloat16),
                  preferred_element_type=jnp.float32)
    out_ref[0] = acc + b_ref[0]


def _modulation(sc, w_mod, b_mod):
    n_layers, d, n_out = w_mod.shape
    s = sc.shape[0]
    tn = MOD_COLS_PER_STEP
    return pl.pallas_call(
        _modulation_kernel,
        grid=(n_layers, n_out // tn),
        in_specs=[pl.BlockSpec((s, d), lambda l, j: (0, 0)),
                  pl.BlockSpec((1, d, tn), lambda l, j: (l, 0, j)),
                  pl.BlockSpec((1, 1, tn), lambda l, j: (l, 0, j))],
        out_specs=pl.BlockSpec((1, s, tn), lambda l, j: (l, 0, j)),
        out_shape=jax.ShapeDtypeStruct((n_layers, s, n_out), jnp.float32),
        compiler_params=pltpu.CompilerParams(dimension_semantics=("arbitrary", "arbitrary"),
                                             vmem_limit_bytes=GATHER_VMEM_LIMIT),
    )(sc, w_mod, b_mod.reshape(n_layers, 1, n_out))


def _residual_layer_norm(x, g_mod, mix, ln_g, ln_b):
    y = DN_ALPHA * x + g_mod * mix
    return _layer_norm(y, ln_g, ln_b)


MIXER_ROWS_PER_STEP = 512


def _chunk_mlp_kernel(x_ref, mod_ref, win_ref, ng_ref, nb_ref, ws_ref, bs_ref, wout_ref, lng_ref, lnb_ref,
                      out_ref, gated):
    x = x_ref[...]
    sh1 = mod_ref[0, 0:1, :]
    s1 = mod_ref[0, 1:2, :]
    g1 = mod_ref[0, 2:3, :]
    h = (x * (1.0 + s1) + sh1).astype(jnp.bfloat16)
    z = jax.nn.gelu(jnp.dot(h, win_ref[...], preferred_element_type=jnp.float32))
    u = z[:, :A_WIDTH]
    v = _layer_norm(z[:, A_WIDTH:], ng_ref[...], nb_ref[...]).astype(jnp.bfloat16)
    for c in range(x.shape[0] // CHUNK):
        r0 = c * CHUNK
        for hd in range(A_HEADS):
            c0 = hd * A_GROUP
            s = jnp.dot(ws_ref[hd], v[r0:r0 + CHUNK, c0:c0 + A_GROUP], preferred_element_type=jnp.float32)
            s = s + bs_ref[:, hd:hd + 1]
            gated[r0:r0 + CHUNK, c0:c0 + A_GROUP] = (u[r0:r0 + CHUNK, c0:c0 + A_GROUP] * s).astype(jnp.bfloat16)
    mix = jnp.dot(gated[...], wout_ref[...], preferred_element_type=jnp.float32)
    out_ref[...] = _residual_layer_norm(x, g1, mix, lng_ref[...], lnb_ref[...])


def _chunk_mlp_mixer(x, mod, w_in, norm_g, norm_b, w_s, b_s, w_out, ln_g, ln_b, *, rows_per_sample):
    t_total, d = x.shape
    rows = MIXER_ROWS_PER_STEP
    n_mod = mod.shape[0]
    assert t_total % rows == 0 and rows_per_sample % rows == 0
    blocks_per_sample = rows_per_sample // rows
    const = lambda *shape: pl.BlockSpec(shape, lambda i: (0,) * len(shape))
    return pl.pallas_call(
        _chunk_mlp_kernel,
        grid=(t_total // rows,),
        in_specs=[pl.BlockSpec((rows, d), lambda i: (i, 0)),
                  pl.BlockSpec((1, 6, d), lambda i: (jnp.minimum(i // blocks_per_sample, n_mod - 1), 0, 0)),
                  const(d, 2 * A_WIDTH), const(1, A_WIDTH), const(1, A_WIDTH),
                  const(A_HEADS, CHUNK, CHUNK), const(CHUNK, A_HEADS), const(A_WIDTH, d),
                  const(1, d), const(1, d)],
        out_specs=pl.BlockSpec((rows, d), lambda i: (i, 0)),
        out_shape=jax.ShapeDtypeStruct((t_total, d), jnp.float32),
        scratch_shapes=[pltpu.VMEM((rows, A_WIDTH), jnp.bfloat16)],
        compiler_params=pltpu.CompilerParams(dimension_semantics=("arbitrary",),
                                             vmem_limit_bytes=MIXER_VMEM_LIMIT),
    )(x, mod, w_in.astype(jnp.bfloat16), norm_g.reshape(1, -1), norm_b.reshape(1, -1),
      w_s.astype(jnp.bfloat16), b_s.T, w_out.astype(jnp.bfloat16), ln_g.reshape(1, d), ln_b.reshape(1, d))


GLA_MAIN = 2 * GLA_QK + 2 * GLA_V
GLA_CHUNKS_PER_STEP = 2
GLA_GATE_PAD = 128


def _gla_project_kernel(x_ref, mod_ref, win_ref, wgin_ref, wg_ref, gb_ref, q_ref, k_ref, v_ref, r_ref, g_ref):
    sh1 = mod_ref[0, 0:1, :]
    s1 = mod_ref[0, 1:2, :]
    h = (x_ref[...] * (1.0 + s1) + sh1).astype(jnp.bfloat16)
    z = jnp.dot(h, win_ref[...], preferred_element_type=jnp.float32)
    q_ref[...] = z[:, :GLA_QK] * (GLA_DK ** -0.5)
    k_ref[...] = z[:, GLA_QK:2 * GLA_QK]
    v_ref[...] = z[:, 2 * GLA_QK:2 * GLA_QK + GLA_V]
    r_ref[...] = z[:, 2 * GLA_QK + GLA_V:]
    gl = jnp.dot(h, wgin_ref[...], preferred_element_type=jnp.float32).astype(jnp.bfloat16)
    zg = jnp.dot(gl, wg_ref[...], preferred_element_type=jnp.float32) + gb_ref[...]
    g_ref[...] = jax.nn.log_sigmoid(zg) / GLA_TAU


def _gla_project(x, mod, w_in, w_gate, gate_bias, *, rows_per_sample):
    t_total, d = x.shape
    rows = MIXER_ROWS_PER_STEP
    n_mod = mod.shape[0]
    assert t_total % rows == 0 and rows_per_sample % rows == 0
    blocks_per_sample = rows_per_sample // rows
    rank = GLA_GATE_RANK
    w_main = w_in[:, :GLA_MAIN].astype(jnp.bfloat16)
    w_gin = jnp.pad(w_in[:, GLA_MAIN:], ((0, 0), (0, GLA_GATE_PAD - 2 * rank))).astype(jnp.bfloat16)
    w_g2 = jnp.zeros((GLA_GATE_PAD, 2 * GLA_QK), jnp.float32)
    w_g2 = w_g2.at[:rank, :GLA_QK].set(w_gate[0]).at[rank:2 * rank, GLA_QK:].set(w_gate[1]).astype(jnp.bfloat16)
    const = lambda *shape: pl.BlockSpec(shape, lambda i: (0,) * len(shape))
    row = lambda n: pl.BlockSpec((rows, n), lambda i: (i, 0))
    f32 = jnp.float32
    return pl.pallas_call(
        _gla_project_kernel,
        grid=(t_total // rows,),
        in_specs=[row(d),
                  pl.BlockSpec((1, 6, d), lambda i: (jnp.minimum(i // blocks_per_sample, n_mod - 1), 0, 0)),
                  const(d, GLA_MAIN), const(d, GLA_GATE_PAD), const(GLA_GATE_PAD, 2 * GLA_QK),
                  const(1, 2 * GLA_QK)],
        out_specs=[row(GLA_QK), row(GLA_QK), row(GLA_V), row(GLA_V), row(2 * GLA_QK)],
        out_shape=[jax.ShapeDtypeStruct((t_total, GLA_QK), f32), jax.ShapeDtypeStruct((t_total, GLA_QK), f32),
                   jax.ShapeDtypeStruct((t_total, GLA_V), f32), jax.ShapeDtypeStruct((t_total, GLA_V), f32),
                   jax.ShapeDtypeStruct((t_total, 2 * GLA_QK), f32)],
        compiler_params=pltpu.CompilerParams(dimension_semantics=("arbitrary",),
                                             vmem_limit_bytes=MIXER_VMEM_LIMIT),
    )(x, mod, w_main, w_gin, w_g2, gate_bias.reshape(1, 2 * GLA_QK))


def _gla_scan_kernel(qf_ref, kf_ref, vf_ref, gf_ref, qb_ref, kb_ref, vb_ref, gb_ref, of_ref, ob_ref,
                     state_f, state_b):
    @pl.when(pl.program_id(1) == 0)
    def _():
        state_f[...] = jnp.zeros_like(state_f)
        state_b[...] = jnp.zeros_like(state_b)

    n = GLA_CHUNK
    subs = qf_ref.shape[0] // n
    for sub in range(subs):
        lo_f, lo_b = sub * n, (subs - 1 - sub) * n
        view = lambda refs, lo: tuple(r.at[pl.ds(lo, n)] for r in refs)
        _gla_chunk_pair(view((qf_ref, kf_ref, vf_ref, gf_ref, of_ref), lo_f) + (state_f,),
                        view((qb_ref, kb_ref, vb_ref, gb_ref, ob_ref), lo_b) + (state_b,))


def _gla_chunk_pair(fwd, bwd):
    n = GLA_CHUNK
    row = lax.broadcasted_iota(jnp.int32, (n, n), 0)
    col = lax.broadcasted_iota(jnp.int32, (n, n), 1)
    bf = jnp.bfloat16
    nt = (((1,), (1,)), ((), ()))
    tn = (((0,), (0,)), ((), ()))
    dirs = []
    for refs, backward in ((fwd, False), (bwd, True)):
        q_ref, k_ref, v_ref, g_ref, o_ref, state = refs
        keep = (col >= row) if backward else (col <= row)
        b = jnp.dot(keep.astype(jnp.float32), g_ref[...], preferred_element_type=jnp.float32,
                    precision=lax.Precision.HIGHEST)
        b_end = b[0:1, :] if backward else b[n - 1:n, :]
        kk = k_ref[...]
        dirs.append(dict(keep=keep, o_ref=o_ref, state=state, decay=jnp.exp(b_end),
                         q_in=(q_ref[...] * jnp.exp(b)).astype(bf), k_in=(kk * jnp.exp(-b)).astype(bf),
                         k_end=(kk * jnp.exp(b_end - b)).astype(bf), v=v_ref[...].astype(bf)))
    heads = [(d, hd, slice(hd * GLA_DK, (hd + 1) * GLA_DK), slice(hd * GLA_DV, (hd + 1) * GLA_DV))
             for d in dirs for hd in range(GLA_HEADS)]
    att = [lax.dot_general(d["q_in"][:, ck], d["k_in"][:, ck], nt, preferred_element_type=jnp.float32)
           for d, hd, ck, cv in heads]
    att = [jnp.where(d["keep"], a, 0.0).astype(bf) for (d, hd, ck, cv), a in zip(heads, att)]
    old = [d["state"][hd] for d, hd, ck, cv in heads]
    inter = [lax.dot_general(d["q_in"][:, ck], s.astype(bf), nt, preferred_element_type=jnp.float32)
             for (d, hd, ck, cv), s in zip(heads, old)]
    intra = [jnp.dot(a, d["v"][:, cv], preferred_element_type=jnp.float32) for (d, hd, ck, cv), a in zip(heads, att)]
    vk = [lax.dot_general(d["v"][:, cv], d["k_end"][:, ck], tn, preferred_element_type=jnp.float32)
          for d, hd, ck, cv in heads]
    for (d, hd, ck, cv), o1, o2, s, u in zip(heads, intra, inter, old, vk):
        d["o_ref"][:, cv] = o1 + o2
        d["state"][hd] = d["decay"][:, ck] * s + u


def _gla_scan(q, k, v, g, *, bsz, lat_chunks, ctx_chunks):
    t_total = q.shape[0]
    per = GLA_CHUNKS_PER_STEP
    assert lat_chunks % per == 0 and ctx_chunks % per == 0
    lat_chunks, ctx_chunks = lat_chunks // per, ctx_chunks // per
    n = GLA_CHUNK * per
    steps = ctx_chunks + lat_chunks
    ctx_base = bsz * lat_chunks

    def block(b, j, backward):
        if backward:
            c_ctx, c_lat = ctx_chunks - 1 - j, lat_chunks - 1 - (j - ctx_chunks)
        else:
            c_ctx, c_lat = j, j - ctx_chunks
        return jnp.where(j < ctx_chunks, ctx_base + b * ctx_chunks + c_ctx, b * lat_chunks + c_lat)

    def specs(backward):
        blk = functools.partial(block, backward=backward)
        return [pl.BlockSpec((n, GLA_QK), lambda b, j: (blk(b, j), 0)),
                pl.BlockSpec((n, GLA_QK), lambda b, j: (blk(b, j), 0)),
                pl.BlockSpec((n, GLA_V), lambda b, j: (blk(b, j), 0)),
                pl.BlockSpec((n, GLA_QK), lambda b, j: (blk(b, j), 1 if backward else 0))]

    out_spec = lambda backward: pl.BlockSpec((n, GLA_V), lambda b, j: (block(b, j, backward), 0))
    o_shape = jax.ShapeDtypeStruct((t_total, GLA_V), jnp.float32)
    return pl.pallas_call(
        _gla_scan_kernel,
        grid=(bsz, steps),
        in_specs=specs(False) + specs(True),
        out_specs=[out_spec(False), out_spec(True)],
        out_shape=[o_shape, o_shape],
        scratch_shapes=[pltpu.VMEM((GLA_HEADS, GLA_DV, GLA_DK), jnp.float32),
                        pltpu.VMEM((GLA_HEADS, GLA_DV, GLA_DK), jnp.float32)],
        compiler_params=pltpu.CompilerParams(dimension_semantics=("arbitrary", "arbitrary")),
    )(q, k, v, g, q, k, v, g)


def _gla_readout_kernel(x_ref, mod_ref, of_ref, ob_ref, r_ref, gn_ref, wout_ref, lng_ref, lnb_ref, out_ref, gated):
    x = x_ref[...]
    g1 = mod_ref[0, 2:3, :]
    o = of_ref[...] + ob_ref[...]
    r = r_ref[...]
    for hd in range(GLA_HEADS):
        cv = slice(hd * GLA_DV, (hd + 1) * GLA_DV)
        oh = o[:, cv]
        mu = jnp.mean(oh, axis=-1, keepdims=True)
        var = jnp.mean(jnp.square(oh - mu), axis=-1, keepdims=True)
        y = (oh - mu) * lax.rsqrt(var + LN_EPS) * gn_ref[:, cv]
        gated[:, cv] = (y * jax.nn.silu(r[:, cv])).astype(jnp.bfloat16)
    mix = jnp.dot(gated[...], wout_ref[...], preferred_element_type=jnp.float32)
    out_ref[...] = _residual_layer_norm(x, g1, mix, lng_ref[...], lnb_ref[...])


def _gla_readout(x, mod, o_f, o_b, r, gn_g, w_out, ln_g, ln_b, *, rows_per_sample):
    t_total, d = x.shape
    rows = MIXER_ROWS_PER_STEP
    n_mod = mod.shape[0]
    assert t_total % rows == 0 and rows_per_sample % rows == 0
    blocks_per_sample = rows_per_sample // rows
    const = lambda *shape: pl.BlockSpec(shape, lambda i: (0,) * len(shape))
    row = lambda n: pl.BlockSpec((rows, n), lambda i: (i, 0))
    return pl.pallas_call(
        _gla_readout_kernel,
        grid=(t_total // rows,),
        in_specs=[row(d),
                  pl.BlockSpec((1, 6, d), lambda i: (jnp.minimum(i // blocks_per_sample, n_mod - 1), 0, 0)),
                  row(GLA_V), row(GLA_V), row(GLA_V), const(1, GLA_V), const(GLA_V, d), const(1, d), const(1, d)],
        out_specs=row(d),
        out_shape=jax.ShapeDtypeStruct((t_total, d), jnp.float32),
        scratch_shapes=[pltpu.VMEM((rows, GLA_V), jnp.bfloat16)],
        compiler_params=pltpu.CompilerParams(dimension_semantics=("arbitrary",),
                                             vmem_limit_bytes=MIXER_VMEM_LIMIT),
    )(x, mod, o_f, o_b, r, gn_g.reshape(1, GLA_V), w_out.astype(jnp.bfloat16), ln_g.reshape(1, d), ln_b.reshape(1, d))


def _gla_mixer(x, mod, w_in, w_gate, gate_bias, gn_g, w_out, ln_g, ln_b, *, bsz, seq, ctx_len, n_out):
    q, k, v, r, g = _gla_project(x, mod, w_in, w_gate, gate_bias, rows_per_sample=seq)
    o_f, o_b = _gla_scan(q, k, v, g, bsz=bsz, lat_chunks=seq // GLA_CHUNK, ctx_chunks=ctx_len // GLA_CHUNK)
    return _gla_readout(x[:n_out], mod, o_f[:n_out], o_b[:n_out], r[:n_out], gn_g, w_out, ln_g, ln_b,
                        rows_per_sample=seq)


def kernel(x, c, ctx, c_ctx, w_mod, b_mod, ln_g, ln_b, a_w_in, a_norm_g, a_norm_b, a_w_s, a_b_s, a_w_out,
           b_w_in, b_w_gate, b_gate_bias, b_gn_g, b_w_out, p_w_q, p_keys, p_u, p_v):
    bsz, seq, d = x.shape
    ctx_len = ctx.shape[1]
    n_lat = bsz * seq
    for rows in (MIXER_ROWS_PER_STEP, PEER_TOKENS_PER_STEP, ROUTE_TOKENS_PER_STEP, CHUNK, GLA_CHUNK):
        assert seq % rows == 0 and (bsz * ctx_len) % rows == 0 and ctx_len % min(rows, CHUNK) == 0
    xc = jnp.concatenate([x.reshape(n_lat, d), ctx.reshape(bsz * ctx_len, d)], axis=0)
    sc = jax.nn.silu(jnp.concatenate([c, c_ctx[None, :]], axis=0))
    mods = _modulation(sc, w_mod, b_mod).reshape(DEPTH, bsz + 1, 6, d)
    for i in range(DEPTH):
        last = i == DEPTH - 1
        j = i // N_MIXERS
        m = mods[i]
        n_tok = n_lat if last else xc.shape[0]
        if i % N_MIXERS == 0:
            x1 = _chunk_mlp_mixer(xc[:n_tok], m, a_w_in[j], a_norm_g[j], a_norm_b[j], a_w_s[j], a_b_s[j],
                                  a_w_out[j], ln_g[i, 0], ln_b[i, 0], rows_per_sample=seq)
        else:
            x1 = _gla_mixer(xc, m, b_w_in[j], b_w_gate[j], b_gate_bias[j], b_gn_g[j], b_w_out[j],
                            ln_g[i, 0], ln_b[i, 0], bsz=bsz, seq=seq, ctx_len=ctx_len, n_out=n_tok)
        idx, wts = _peer_route_pallas(x1, m, p_w_q[i], p_keys[i], tokens_per_sample_block=seq // ROUTE_TOKENS_PER_STEP)
        n_exp = p_u.shape[1]
        tab = jnp.concatenate([p_u[i].reshape(n_exp, SUBLANES, LANES), p_v[i].reshape(n_exp, SUBLANES, LANES)],
                              axis=1)
        xc = _peer_experts(x1, m, idx, wts, tab, ln_g[i, 1], ln_b[i, 1],
                           tokens_per_sample_block=seq // PEER_TOKENS_PER_STEP)
    return xc[:n_lat].reshape(bsz, seq, d)
```
